```python
import math
import jax
import jax.numpy as jnp
from jax import lax
import numpy as np

D_MODEL = 2048
BATCH = 4
SEQ = 2048
DEPTH = 2
DEC_BATCH = 8
DEC_SEQ = 1
PAST_LEN = 16384
PAGE_SIZE = 128

HEAD_DIM = 128
NSA_HEADS = 8
NSA_KV_HEADS = 2
NSA_GROUP = NSA_HEADS // NSA_KV_HEADS
FOX_HEADS = 8
CMP_LEN = 32
CMP_STRIDE = 16
SEL_BLOCK = 64
SEL_TOPN = 16
N_LOCAL_BLOCKS = 2
WINDOW = 512
Q_BLOCK = 128
SEL_Q_BLOCK = 64
N_BUCKETS = 32
MAX_DISTANCE = 128
PLE_DIM = 256
D_FF = -(-8 * D_MODEL // (3 * 256)) * 256
IN_SIZES = (NSA_HEADS * HEAD_DIM, 6 * NSA_KV_HEADS * HEAD_DIM, 3 * NSA_HEADS,
            3 * FOX_HEADS * HEAD_DIM, FOX_HEADS, 2 * D_MODEL)
D_IN = sum(IN_SIZES)
EPS = 1e-6
SCALE = HEAD_DIM ** -0.5
NEG_INF = -1e30
FORCE_BONUS = 1e4

kernel_name = 'nsa_fox_hybrid_decode_step'


def rmsnorm(x, gain):
    xf = x.astype(jnp.float32)
    y = xf * lax.rsqrt(jnp.mean(xf * xf, axis=-1, keepdims=True) + EPS)
    return (y * gain.astype(jnp.float32)).astype(x.dtype)


def masked_softmax(s, mask, axis=-1):
    p = jax.nn.softmax(jnp.where(mask, s, NEG_INF), axis=axis)
    return jnp.where(jnp.any(mask, axis=axis, keepdims=True), p, 0.0)


def t5_bucket(dist):
    n = jnp.maximum(dist, 0)
    exact = N_BUCKETS // 2
    rel = jnp.log(jnp.maximum(n, 1).astype(jnp.float32) / exact) / math.log(MAX_DISTANCE / exact)
    large = jnp.minimum(exact + (rel * (N_BUCKETS - exact)).astype(jnp.int32), N_BUCKETS - 1)
    return jnp.where(n < exact, n, large)


def head_bias(table, dist):
    b = table[t5_bucket(dist)].astype(jnp.float32).reshape(dist.shape + (NSA_KV_HEADS, NSA_GROUP))
    return jnp.moveaxis(b, (-2, -1), (0, 1))


def over_query_blocks(fn, n_q, block):
    qb = math.gcd(n_q, block)
    out = lax.map(lambda start: fn(start, qb), jnp.arange(n_q // qb) * qb)
    return tuple(jnp.moveaxis(o, 0, 1).reshape((o.shape[1], n_q) + o.shape[3:]) for o in out)


def compress(tok, pos, w1, b1, w2):
    B_, L = tok.shape[:2]
    ratio = CMP_LEN // CMP_STRIDE
    n_chunks = L // CMP_STRIDE
    n_cb = n_chunks - ratio + 1
    ch = tok[:, :n_chunks * CMP_STRIDE].reshape(B_, n_chunks, CMP_STRIDE, NSA_KV_HEADS, HEAD_DIM)
    blocks = jnp.concatenate([ch[:, r:r + n_cb] for r in range(ratio)], axis=2)
    blocks = blocks + pos[:, None, :]
    flat = jnp.swapaxes(blocks, 2, 3).reshape(B_, n_cb, NSA_KV_HEADS, CMP_LEN * HEAD_DIM)
    return jax.nn.gelu(flat @ w1 + b1) @ w2


def nsa_cmp_sel(q, kc_tok, vc_tok, ks_tok, vs_tok, table, cmp_pos, w_cmp1, b_cmp1, w_cmp2, gain_kc):
    B_, n_q = q.shape[:2]
    L = ks_tok.shape[1]
    kc = rmsnorm(compress(kc_tok, cmp_pos[0], w_cmp1[0], b_cmp1[0], w_cmp2[0]), gain_kc)
    vc = compress(vc_tok, cmp_pos[1], w_cmp1[1], b_cmp1[1], w_cmp2[1])
    n_cb = kc.shape[1]
    ends = jnp.arange(n_cb) * CMP_STRIDE + (CMP_LEN - 1)
    n_sb = -(-L // SEL_BLOCK)
    n_sel = min(SEL_TOPN, n_sb)

    def to_blocks(t):
        t = jnp.pad(t, ((0, 0), (0, n_sb * SEL_BLOCK - L), (0, 0), (0, 0)))
        return jnp.moveaxis(t.reshape(B_, n_sb, SEL_BLOCK, NSA_KV_HEADS, HEAD_DIM), 3, 1)
    ks_b, vs_b = to_blocks(ks_tok), to_blocks(vs_tok)
    jb = jnp.arange(n_sb)
    n_ov = (SEL_BLOCK + CMP_LEN) // CMP_STRIDE - 1
    ov = ((jb * SEL_BLOCK - CMP_LEN) // CMP_STRIDE + 1)[:, None] + jnp.arange(n_ov)[None, :]
    ov_valid = (ov >= 0) & (ov < n_cb)
    ov = jnp.clip(ov, 0, n_cb - 1)
    tbl_g = table.reshape(N_BUCKETS, NSA_KV_HEADS, NSA_GROUP)
    g_idx = jnp.arange(NSA_KV_HEADS)[None, :, None, None, None]
    take_blocks = jax.vmap(jax.vmap(lambda blk, ix: blk[ix]))

    def block(start, qb):
        qblk = lax.dynamic_slice_in_dim(q, start, qb, axis=1)
        qpos = (L - n_q) + start + jnp.arange(qb)
        dist_c = qpos[:, None] - ends[None, :]
        s_c = jnp.einsum('bqgrd,bkgd->bgrqk', qblk, kc).astype(jnp.float32) * SCALE + head_bias(table, dist_c)
        p_c = masked_softmax(s_c, dist_c >= 0)
        o_c = jnp.einsum('bgrqk,bkgd->bqgrd', p_c.astype(vc.dtype), vc)
        imp_c = p_c.sum(axis=2)
        imp_s = jnp.where(ov_valid, jnp.take(imp_c, ov, axis=-1), 0.0).sum(axis=-1)
        back = (qpos // SEL_BLOCK)[:, None] - jb[None, :]
        forced = (jb[None, :] == 0) | ((back >= 0) & (back < N_LOCAL_BLOCKS))
        score = jnp.where(back >= 0, imp_s + jnp.where(forced, FORCE_BONUS, 0.0), -1.0)
        _, sel = lax.top_k(score, n_sel)
        kg, vg = take_blocks(ks_b, sel), take_blocks(vs_b, sel)
        dist_s = qpos[:, None, None] - (sel[..., None] * SEL_BLOCK + jnp.arange(SEL_BLOCK))
        bias_s = jnp.moveaxis(tbl_g[t5_bucket(dist_s), g_idx].astype(jnp.float32), -1, 2)
        s_s = jnp.einsum('bqgrd,bgqnkd->bgrqnk', qblk, kg).astype(jnp.float32) * SCALE + bias_s
        p_s = masked_softmax(s_s, (dist_s >= 0)[:, :, None], axis=(-2, -1))
        o_s = jnp.einsum('bgrqnk,bgqnkd->bqgrd', p_s.astype(vg.dtype), vg)
        return (o_c, o_s)

    return over_query_blocks(block, n_q, SEL_Q_BLOCK)


def window_attend(q, qpos, k, v, kpos, table):
    dist = qpos[:, None] - kpos[None, :]
    mask = (dist >= 0) & (dist < WINDOW) & (kpos >= 0)[None, :]
    s = jnp.einsum('bqgrd,bkgd->bgrqk', q, k).astype(jnp.float32) * SCALE + head_bias(table, dist)
    p = masked_softmax(s, mask)
    return jnp.einsum('bgrqk,bkgd->bqgrd', p.astype(v.dtype), v)


def window_prompt(q, kw, vw, table):
    n_q = q.shape[1]
    pad = ((0, 0), (WINDOW, 0), (0, 0), (0, 0))
    kp, vp = jnp.pad(kw, pad), jnp.pad(vw, pad)

    def block(start, qb):
        span = WINDOW + qb
        o = window_attend(lax.dynamic_slice_in_dim(q, start, qb, axis=1), start + jnp.arange(qb),
                          lax.dynamic_slice_in_dim(kp, start, span, axis=1),
                          lax.dynamic_slice_in_dim(vp, start, span, axis=1),
                          start - WINDOW + jnp.arange(span), table)
        return (o,)

    (o,) = over_query_blocks(block, n_q, Q_BLOCK)
    return o


def fox_attend(q, k, v, c):
    n_q, L = q.shape[1], k.shape[1]
    kpos = jnp.arange(L)
    c_k = jnp.swapaxes(c, 1, 2)

    def block(start, qb):
        qblk = lax.dynamic_slice_in_dim(q, start, qb, axis=1)
        qpos = (L - n_q) + start + jnp.arange(qb)
        c_q = lax.dynamic_slice_in_dim(c_k, L - n_q + start, qb, axis=2)
        s = jnp.einsum('bqhd,bkhd->bhqk', qblk, k).astype(jnp.float32) * SCALE + (c_q[..., :, None] - c_k[..., None, :])
        p = masked_softmax(s, kpos[None, :] <= qpos[:, None])
        return (jnp.einsum('bhqk,bkhd->bqhd', p.astype(v.dtype), v),)

    (o,) = over_query_blocks(block, n_q, Q_BLOCK)
    return o


def project_in(x, gain, w_in_l, b_f, g_nsa, g_fox):
    B_, T = x.shape[:2]
    h = rmsnorm(x, gain)
    splits = [sum(IN_SIZES[:i + 1]) for i in range(len(IN_SIZES) - 1)]
    q_a, kv_a, br, qkv_b, f_b, mg = jnp.split(h @ w_in_l, splits, axis=-1)
    q_a = rmsnorm(q_a.reshape(B_, T, NSA_KV_HEADS, NSA_GROUP, HEAD_DIM), g_nsa[0])
    kc, vc, ks, vs, kw, vw = jnp.moveaxis(kv_a.reshape(B_, T, 6, NSA_KV_HEADS, HEAD_DIM), 2, 0)
    ks = rmsnorm(ks, g_nsa[2])
    kw = rmsnorm(kw, g_nsa[3])
    br = jax.nn.sigmoid(br.reshape(B_, T, NSA_KV_HEADS, NSA_GROUP, 3))
    q_b, k_b, v_b = jnp.moveaxis(qkv_b.reshape(B_, T, 3, FOX_HEADS, HEAD_DIM), 2, 0)
    q_b = rmsnorm(q_b, g_fox[0])
    k_b = rmsnorm(k_b, g_fox[1])
    logf = jax.nn.log_sigmoid((f_b + b_f).astype(jnp.float32))
    ga, gb = jnp.split(jax.nn.sigmoid(mg), 2, axis=-1)
    return q_a, kc, vc, ks, vs, kw, vw, br, q_b, k_b, v_b, logf, ga, gb


def combine_nsa(br, o_c, o_s, o_w):
    return br[..., 0:1] * o_c + br[..., 1:2] * o_s + br[..., 2:3] * o_w


def layer_tail(x, o_nsa, o_fox, ga, gb, pe, gains, wbn, wbf, wo, wfi, wfo, wple, wpg):
    B_, T = x.shape[:2]
    merged = ga * (o_nsa.reshape(B_, T, -1) @ wbn) + gb * (o_fox.reshape(B_, T, -1) @ wbf)
    x = x + merged @ wo
    g, u = jnp.split(rmsnorm(x, gains[1]) @ wfi, 2, axis=-1)
    x = x + (jax.nn.silu(g) * u) @ wfo
    return x + jax.nn.sigmoid(rmsnorm(x, gains[2]) @ wpg) * (pe @ wple)


def gather_pages(pool, page_table, layer):
    g = pool[page_table, layer]
    return g.reshape((g.shape[0], g.shape[1] * g.shape[2]) + g.shape[3:])


def setup_inputs(seed: int = 0) -> dict:
    key = jax.random.key(seed)
    ks = jax.random.split(key, 32)
    n_pages = PAST_LEN // PAGE_SIZE
    n_used = DEC_BATCH * n_pages
    n_pool = n_used + n_used // 4
    w_buf = min(WINDOW, PAST_LEN)

    def nrm(i, shape, scale=1.0):
        return jax.random.normal(ks[i], shape, jnp.float32) * scale

    def gain(i, shape):
        return 1.0 + nrm(i, shape, 0.05)

    page_table = jax.random.permutation(ks[8], n_pool)[:n_used].reshape(DEC_BATCH, n_pages).astype(jnp.int32)
    return {
        'x_prompt': nrm(0, (BATCH, SEQ, D_MODEL)),
        'x_sample': nrm(1, (DEC_BATCH, DEC_SEQ, D_MODEL)),
        'cache_nsa_kv': nrm(2, (n_pool, DEPTH, PAGE_SIZE, 4, NSA_KV_HEADS, HEAD_DIM)),
        'cache_fox_kv': nrm(3, (n_pool, DEPTH, PAGE_SIZE, 2, FOX_HEADS, HEAD_DIM)),
        'cache_fox_logf': jax.nn.log_sigmoid(nrm(4, (n_pool, DEPTH, PAGE_SIZE, FOX_HEADS))),
        'state_nsa_window': nrm(5, (DEPTH, DEC_BATCH, w_buf, 2, NSA_KV_HEADS, HEAD_DIM)),
        'page_table': page_table,
        'p_prompt': nrm(6, (DEPTH, BATCH, SEQ, PLE_DIM)),
        'p_sample': nrm(7, (DEPTH, DEC_BATCH, DEC_SEQ, PLE_DIM)),
        'rel_bias_table': nrm(9, (N_BUCKETS, NSA_HEADS), 0.5),
        'norm_gains': gain(10, (DEPTH, 3, D_MODEL)),
        'w_in': nrm(11, (DEPTH, D_MODEL, D_IN), D_MODEL ** -0.5),
        'b_forget': nrm(12, (DEPTH, FOX_HEADS), 0.1),
        'qk_gain_nsa': gain(13, (DEPTH, 4, HEAD_DIM)),
        'qk_gain_fox': gain(14, (DEPTH, 2, HEAD_DIM)),
        'cmp_pos': nrm(15, (DEPTH, 2, CMP_LEN, HEAD_DIM), 0.1),
        'w_cmp1': nrm(16, (DEPTH, 2, CMP_LEN * HEAD_DIM, HEAD_DIM), (CMP_LEN * HEAD_DIM) ** -0.5),
        'b_cmp1': nrm(17, (DEPTH, 2, HEAD_DIM), 0.02),
        'w_cmp2': nrm(18, (DEPTH, 2, HEAD_DIM, HEAD_DIM), HEAD_DIM ** -0.5),
        'w_branch_nsa': nrm(19, (DEPTH, NSA_HEADS * HEAD_DIM, D_MODEL), (NSA_HEADS * HEAD_DIM) ** -0.5),
        'w_branch_fox': nrm(20, (DEPTH, FOX_HEADS * HEAD_DIM, D_MODEL), (FOX_HEADS * HEAD_DIM) ** -0.5),
        'w_out': nrm(21, (DEPTH, D_MODEL, D_MODEL), D_MODEL ** -0.5),
        'w_ffn_in': nrm(22, (DEPTH, D_MODEL, 2 * D_FF), D_MODEL ** -0.5),
        'w_ffn_out': nrm(23, (DEPTH, D_FF, D_MODEL), D_FF ** -0.5),
        'w_ple': nrm(24, (DEPTH, PLE_DIM, D_MODEL), PLE_DIM ** -0.5),
        'w_ple_gate': nrm(25, (DEPTH, D_MODEL, D_MODEL), D_MODEL ** -0.5),
    }


def reference(x_prompt, x_sample, cache_nsa_kv, cache_fox_kv, cache_fox_logf, state_nsa_window,
              page_table, p_prompt, p_sample, rel_bias_table, norm_gains, w_in, b_forget,
              qk_gain_nsa, qk_gain_fox, cmp_pos, w_cmp1, b_cmp1, w_cmp2, w_branch_nsa,
              w_branch_fox, w_out, w_ffn_in, w_ffn_out, w_ple, w_ple_gate):
    xp, xs = x_prompt, x_sample
    n_qp, n_qs = xp.shape[1], xs.shape[1]
    w_buf = state_nsa_window.shape[2]
    nsa_kv_p, win_p, fox_kv_p, logf_p = [], [], [], []
    nsa_kv_s, win_s, fox_kv_s, logf_s = [], [], [], []
    for l in range(DEPTH):
        tail_w = (norm_gains[l], w_branch_nsa[l], w_branch_fox[l], w_out[l], w_ffn_in[l],
                  w_ffn_out[l], w_ple[l], w_ple_gate[l])
        cmp_w = (rel_bias_table, cmp_pos[l], w_cmp1[l], b_cmp1[l], w_cmp2[l], qk_gain_nsa[l, 1])
        qa, kc, vc, ksel, vsel, kw, vw, br, qf, kf, vf, logf, ga, gb = project_in(
            xp, norm_gains[l, 0], w_in[l], b_forget[l], qk_gain_nsa[l], qk_gain_fox[l])
        o_c, o_s = nsa_cmp_sel(qa, kc, vc, ksel, vsel, *cmp_w)
        o_w = window_prompt(qa, kw, vw, rel_bias_table)
        o_f = fox_attend(qf, kf, vf, jnp.cumsum(logf, axis=1))
        xp = layer_tail(xp, combine_nsa(br, o_c, o_s, o_w), o_f, ga, gb, p_prompt[l], *tail_w)
        nsa_kv_p.append(jnp.stack([kc, vc, ksel, vsel], axis=2))
        win_p.append(jnp.stack([kw, vw], axis=2)[:, -min(WINDOW, n_qp):])
        fox_kv_p.append(jnp.stack([kf, vf], axis=2))
        logf_p.append(logf)
        qa, kc, vc, ksel, vsel, kw, vw, br, qf, kf, vf, logf, ga, gb = project_in(
            xs, norm_gains[l, 0], w_in[l], b_forget[l], qk_gain_nsa[l], qk_gain_fox[l])
        new_nsa = jnp.stack([kc, vc, ksel, vsel], axis=2)
        full = jnp.concatenate([gather_pages(cache_nsa_kv, page_table, l), new_nsa], axis=1)
        o_c, o_s = nsa_cmp_sel(qa, full[:, :, 0], full[:, :, 1], full[:, :, 2], full[:, :, 3], *cmp_w)
        win = jnp.concatenate([state_nsa_window[l], jnp.stack([kw, vw], axis=2)], axis=1)
        o_w = window_attend(qa, PAST_LEN + jnp.arange(n_qs), win[:, :, 0], win[:, :, 1],
                            PAST_LEN - w_buf + jnp.arange(w_buf + n_qs), rel_bias_table)
        fpast = gather_pages(cache_fox_kv, page_table, l)
        kf_all = jnp.concatenate([fpast[:, :, 0], kf], axis=1)
        vf_all = jnp.concatenate([fpast[:, :, 1], vf], axis=1)
        c = jnp.cumsum(jnp.concatenate(
            [gather_pages(cache_fox_logf, page_table, l).astype(jnp.float32), logf], axis=1), axis=1)
        o_f = fox_attend(qf, kf_all, vf_all, c)
        xs = layer_tail(xs, combine_nsa(br, o_c, o_s, o_w), o_f, ga, gb, p_sample[l], *tail_w)
        nsa_kv_s.append(new_nsa)
        win_s.append(win[:, -min(WINDOW, w_buf + n_qs):])
        fox_kv_s.append(jnp.stack([kf, vf], axis=2))
        logf_s.append(logf)
    return (xp, xs, jnp.stack(nsa_kv_p), jnp.stack(win_p), jnp.stack(fox_kv_p), jnp.stack(logf_p),
            jnp.stack(nsa_kv_s), jnp.stack(win_s), jnp.stack(fox_kv_s), jnp.stack(logf_s))
```

```python
import functools
import math

import jax
import jax.numpy as jnp
from jax import lax
from jax.experimental import pallas as pl
from jax.experimental.pallas import tpu as pltpu

F32 = jnp.float32
BF16 = jnp.bfloat16
I32 = jnp.int32

HEAD_DIM = 128
NSA_HEADS = 8
NSA_KV_HEADS = 2
NSA_GROUP = NSA_HEADS // NSA_KV_HEADS
FOX_HEADS = 8
CMP_LEN = 32
CMP_STRIDE = 16
SEL_BLOCK = 64
SEL_TOPN = 16
N_LOCAL_BLOCKS = 2
WINDOW = 512
N_BUCKETS = 32
BUCKET_EXACT = N_BUCKETS // 2
MAX_DISTANCE = 128
EPS = 1e-6
SCALE = HEAD_DIM ** -0.5
NEG_INF = -1e30
FORCE_BONUS = 1e4

LANES = 128
SUBLANES = 8
VMEM_LIMIT_MB = 52

G = NSA_KV_HEADS
R = NSA_GROUP
HD = HEAD_DIM
BR_LANES = 3 * NSA_HEADS
LOGF_LANE0 = BR_LANES
CHUNKS_PER_PAGE = 128 // CMP_STRIDE


def _cparams(sem, vmem_mb=VMEM_LIMIT_MB):
    return pltpu.CompilerParams(dimension_semantics=sem, vmem_limit_bytes=vmem_mb * 1024 * 1024)


def _pick(n, prefs):
    for p in prefs:
        if n % p == 0:
            return p
    return n


def _iota(shape, dim):
    return lax.broadcasted_iota(I32, shape, dim)


def _dot(a, b):
    return jnp.dot(a, b, preferred_element_type=F32)


def _dot_nt(a, b):
    return lax.dot_general(a, b, (((1,), (1,)), ((), ())), preferred_element_type=F32)


def _split3(x):
    hi = x.astype(BF16)
    r1 = x - hi.astype(F32)
    mid = r1.astype(BF16)
    lo = (r1 - mid.astype(F32)).astype(BF16)
    return hi, mid, lo


def _dot_split_lhs(x, m01):
    hi, mid, lo = _split3(x)
    return _dot(hi, m01) + _dot(mid, m01) + _dot(lo, m01)


def _dot_split_rhs(m01, x):
    hi, mid, lo = _split3(x)
    return _dot(m01, hi) + _dot(m01, mid) + _dot(m01, lo)


def _rms(x, gain):
    return x * lax.rsqrt(jnp.mean(x * x, axis=-1, keepdims=True) + EPS) * gain


def _bucket(dist):
    n = jnp.maximum(dist, 0)
    rel = jnp.log(jnp.maximum(n, 1).astype(F32) / float(BUCKET_EXACT)) / math.log(MAX_DISTANCE / BUCKET_EXACT)
    large = jnp.minimum(BUCKET_EXACT + (rel * float(N_BUCKETS - BUCKET_EXACT)).astype(I32), N_BUCKETS - 1)
    return jnp.where(n < BUCKET_EXACT, n, large)


def _lookup(bucket, vals):
    out = vals[N_BUCKETS - 1]
    for k in range(N_BUCKETS - 1):
        out = jnp.where(bucket == k, vals[k], out)
    return out


def _rms_kernel(x_ref, g_ref, o_ref):
    o_ref[...] = _rms(x_ref[...], g_ref[...]).astype(o_ref.dtype)


def rms_bf16(x, gain):
    M, D = x.shape
    tm = _pick(M, (512, 256, 128, 8))
    return pl.pallas_call(
        _rms_kernel,
        grid=(M // tm,),
        in_specs=[pl.BlockSpec((tm, D), lambda i: (i, 0)), pl.BlockSpec((1, D), lambda i: (0, 0))],
        out_specs=pl.BlockSpec((tm, D), lambda i: (i, 0)),
        out_shape=jax.ShapeDtypeStruct((M, D), BF16),
        compiler_params=_cparams(("parallel",)),
        name="rms",
    )(x, gain.reshape(1, D))


def _proj_kernel(h_ref, w_ref, g_ref, b_ref, o_ref, *, modes):
    h = h_ref[...]
    nt = len(modes)
    t = 0
    while t < nt:
        wd = 2 if t + 1 < nt else 1
        acc = _dot(h, w_ref[:, t * LANES:(t + wd) * LANES])
        for u in range(wd):
            c0 = (t + u) * LANES
            v = acc[:, u * LANES:(u + 1) * LANES]
            mode = modes[t + u]
            if mode == "norm":
                v = _rms(v, g_ref[:, c0:c0 + LANES])
            elif mode == "sigmoid":
                v = jax.nn.sigmoid(v)
            elif mode == "small":
                lane = _iota(v.shape, 1)
                z = v + b_ref[:, c0:c0 + LANES]
                logsig = jnp.minimum(z, 0.0) - jnp.log(1.0 + jnp.exp(-jnp.abs(z)))
                v = jnp.where(lane < BR_LANES, jax.nn.sigmoid(v),
                              jnp.where(lane < LOGF_LANE0 + FOX_HEADS, logsig, 0.0))
            o_ref[:, c0:c0 + LANES] = v
        t += wd


def proj(h, w, gains, bias, modes, tn):
    M, K = h.shape
    N = w.shape[1]
    tm = _pick(M, (512, 256, 128, 8))
    assert len(modes) * LANES == tn and N % tn == 0
    return pl.pallas_call(
        functools.partial(_proj_kernel, modes=modes),
        grid=(M // tm, N // tn),
        in_specs=[pl.BlockSpec((tm, K), lambda i, j: (i, 0)),
                  pl.BlockSpec((K, tn), lambda i, j: (0, j)),
                  pl.BlockSpec((1, tn), lambda i, j: (0, j)),
                  pl.BlockSpec((1, tn), lambda i, j: (0, j))],
        out_specs=pl.BlockSpec((tm, tn), lambda i, j: (i, j)),
        out_shape=jax.ShapeDtypeStruct((M, N), F32),
        compiler_params=_cparams(("parallel", "arbitrary")),
        name="proj",
    )(h, w, gains, bias)


def _merge_kernel(on_ref, of_ref, ga_ref, gb_ref, wn_ref, wf_ref, o_ref):
    a = _dot(on_ref[...], wn_ref[...])
    b = _dot(of_ref[...], wf_ref[...])
    o_ref[...] = (ga_ref[...] * a + gb_ref[...] * b).astype(o_ref.dtype)


def merge(o_nsa, o_fox, gates, wbn, wbf):
    M, K = o_nsa.shape
    D = wbn.shape[1]
    tm = _pick(M, (512, 256, 128, 8))
    tn = _pick(D, (1024, 512, 256, 128))
    nj = D // tn
    return pl.pallas_call(
        _merge_kernel,
        grid=(M // tm, nj),
        in_specs=[pl.BlockSpec((tm, K), lambda i, j: (i, 0)),
                  pl.BlockSpec((tm, K), lambda i, j: (i, 0)),
                  pl.BlockSpec((tm, tn), lambda i, j: (i, j)),
                  pl.BlockSpec((tm, tn), lambda i, j: (i, j + nj)),
                  pl.BlockSpec((K, tn), lambda i, j: (0, j)),
                  pl.BlockSpec((K, tn), lambda i, j: (0, j))],
        out_specs=pl.BlockSpec((tm, tn), lambda i, j: (i, j)),
        out_shape=jax.ShapeDtypeStruct((M, D), BF16),
        compiler_params=_cparams(("parallel", "arbitrary")),
        name="merge",
    )(o_nsa, o_fox, gates, gates, wbn, wbf)


def _wo_kernel(m_ref, w_ref, x_ref, g_ref, x1_ref, h1_ref):
    x1 = x_ref[...] + _dot(m_ref[...], w_ref[...])
    x1_ref[...] = x1
    h1_ref[...] = _rms(x1, g_ref[...]).astype(h1_ref.dtype)


def out_proj(merged, wo, x, gain1):
    M, D = x.shape
    tm = _pick(M, (256, 128, 8))
    return pl.pallas_call(
        _wo_kernel,
        grid=(M // tm,),
        in_specs=[pl.BlockSpec((tm, D), lambda i: (i, 0)),
                  pl.BlockSpec((D, D), lambda i: (0, 0)),
                  pl.BlockSpec((tm, D), lambda i: (i, 0)),
                  pl.BlockSpec((1, D), lambda i: (0, 0))],
        out_specs=[pl.BlockSpec((tm, D), lambda i: (i, 0)), pl.BlockSpec((tm, D), lambda i: (i, 0))],
        out_shape=[jax.ShapeDtypeStruct((M, D), F32), jax.ShapeDtypeStruct((M, D), BF16)],
        compiler_params=_cparams(("parallel",)),
        name="out_proj",
    )(merged, wo, x, gain1.reshape(1, D))


def _ffn_in_kernel(h_ref, wg_ref, wu_ref, o_ref):
    h = h_ref[...]
    gt = _dot(h, wg_ref[...])
    up = _dot(h, wu_ref[...])
    o_ref[...] = (gt * jax.nn.sigmoid(gt) * up).astype(o_ref.dtype)


def ffn_in(h1, wfi):
    M, D = h1.shape
    F = wfi.shape[1] // 2
    tm = _pick(M, (1024, 512, 256, 128, 8))
    tn = _pick(F, (512, 256, 128))
    nj = F // tn
    return pl.pallas_call(
        _ffn_in_kernel,
        grid=(M // tm, nj),
        in_specs=[pl.BlockSpec((tm, D), lambda i, j: (i, 0)),
                  pl.BlockSpec((D, tn), lambda i, j: (0, j)),
                  pl.BlockSpec((D, tn), lambda i, j: (0, j + nj))],
        out_specs=pl.BlockSpec((tm, tn), lambda i, j: (i, j)),
        out_shape=jax.ShapeDtypeStruct((M, F), BF16),
        compiler_params=_cparams(("parallel", "arbitrary")),
        name="ffn_in",
    )(h1, wfi, wfi)


def _ffn_out_kernel(a_ref, w_ref, x_ref, o_ref):
    o_ref[...] = x_ref[...] + _dot(a_ref[...], w_ref[...])


def ffn_out(act, wfo, x1):
    M, F = act.shape
    D = wfo.shape[1]
    tm = _pick(M, (512, 256, 128, 8))
    tn = _pick(D, (512, 256, 128))
    return pl.pallas_call(
        _ffn_out_kernel,
        grid=(M // tm, D // tn),
        in_specs=[pl.BlockSpec((tm, F), lambda i, j: (i, 0)),
                  pl.BlockSpec((F, tn), lambda i, j: (0, j)),
                  pl.BlockSpec((tm, tn), lambda i, j: (i, j))],
        out_specs=pl.BlockSpec((tm, tn), lambda i, j: (i, j)),
        out_shape=jax.ShapeDtypeStruct((M, D), F32),
        compiler_params=_cparams(("parallel", "arbitrary")),
        name="ffn_out",
    )(act, wfo, x1)


def _ple_kernel(x_ref, g_ref, wpg_ref, pe_ref, wple_ref, gn_ref, o_ref, hn_ref):
    x = x_ref[...]
    h2 = _rms(x, g_ref[...]).astype(BF16)
    gate = jax.nn.sigmoid(_dot(h2, wpg_ref[...]))
    x3 = x + gate * _dot(pe_ref[...].astype(BF16), wple_ref[...])
    o_ref[...] = x3
    hn_ref[...] = _rms(x3, gn_ref[...]).astype(hn_ref.dtype)


def ple_gate(x2, gain2, wpg, pe, wple, gain_next):
    M, D = x2.shape
    Pd = pe.shape[1]
    tm = _pick(M, (256, 128, 8))
    return pl.pallas_call(
        _ple_kernel,
        grid=(M // tm,),
        in_specs=[pl.BlockSpec((tm, D), lambda i: (i, 0)),
                  pl.BlockSpec((1, D), lambda i: (0, 0)),
                  pl.BlockSpec((D, D), lambda i: (0, 0)),
                  pl.BlockSpec((tm, Pd), lambda i: (i, 0)),
                  pl.BlockSpec((Pd, D), lambda i: (0, 0)),
                  pl.BlockSpec((1, D), lambda i: (0, 0))],
        out_specs=[pl.BlockSpec((tm, D), lambda i: (i, 0)), pl.BlockSpec((tm, D), lambda i: (i, 0))],
        out_shape=[jax.ShapeDtypeStruct((M, D), F32), jax.ShapeDtypeStruct((M, D), BF16)],
        compiler_params=_cparams(("parallel",)),
        name="ple_gate",
    )(x2, gain2.reshape(1, D), wpg, pe, wple, gain_next.reshape(1, D))


def _bias_tiles_kernel(tbl_ref, o_ref):
    g = pl.program_id(0)
    ii = _iota((LANES, LANES), 0)
    jj = _iota((LANES, LANES), 1)
    for s in range(2):
        b = _bucket(s * LANES + ii - jj)
        for r in range(R):
            vals = [tbl_ref[k, g * R + r] for k in range(N_BUCKETS)]
            o_ref[0, s, r * LANES:(r + 1) * LANES, :] = _lookup(b, vals)
    for r in range(R):
        o_ref[0, 2, r * LANES:(r + 1) * LANES, :] = jnp.full((LANES, LANES), tbl_ref[N_BUCKETS - 1, g * R + r], F32)


def bias_tiles(table):
    return pl.pallas_call(
        _bias_tiles_kernel,
        grid=(G,),
        in_specs=[pl.BlockSpec(memory_space=pltpu.SMEM)],
        out_specs=pl.BlockSpec((1, 3, R * LANES, LANES), lambda g: (g, 0, 0, 0)),
        out_shape=jax.ShapeDtypeStruct((G, 3, R * LANES, LANES), F32),
        compiler_params=_cparams(("arbitrary",)),
        name="bias_tiles",
    )(table)


def _cumsum_kernel(x_ref, ct_ref, cr_ref):
    T = x_ref.shape[1]
    ii = _iota((LANES, LANES), 0)
    jj = _iota((LANES, LANES), 1)
    tri = jnp.where(jj <= ii, 1.0, 0.0).astype(BF16)
    carry = jnp.zeros((1, LANES), F32)
    for blk in range(T // LANES):
        sl = slice(blk * LANES, (blk + 1) * LANES)
        c = _dot_split_rhs(tri, x_ref[0, sl, :]) + carry
        ct_ref[0, sl, :] = c
        cr_ref[0, :, sl] = c.T[LOGF_LANE0:LOGF_LANE0 + FOX_HEADS, :]
        carry = c[LANES - 1:LANES, :]


def cumsum_logf(small):
    B, T, _ = small.shape
    return pl.pallas_call(
        _cumsum_kernel,
        grid=(B,),
        in_specs=[pl.BlockSpec((1, T, LANES), lambda b: (b, 0, 0))],
        out_specs=[pl.BlockSpec((1, T, LANES), lambda b: (b, 0, 0)),
                   pl.BlockSpec((1, FOX_HEADS, T), lambda b: (b, 0, 0))],
        out_shape=[jax.ShapeDtypeStruct((B, T, LANES), F32), jax.ShapeDtypeStruct((B, FOX_HEADS, T), F32)],
        compiler_params=_cparams(("parallel",)),
        name="cumsum_logf",
    )(small)


def _fox_prompt_kernel(q_ref, k_ref, v_ref, ct_ref, cr_ref, o_ref, *, tq):
    h = pl.program_id(1)
    qi = pl.program_id(2)
    q = q_ref[0].astype(BF16)
    lane = _iota((tq, LANES), 1)
    cq = jnp.sum(jnp.where(lane == LOGF_LANE0 + h, ct_ref[0], 0.0), axis=1, keepdims=True)
    ii = _iota((tq, tq), 0)
    jj = _iota((tq, tq), 1)

    def body(kb, carry):
        m, l, acc = carry
        k0 = pl.multiple_of(kb * tq, tq)
        kblk = k_ref[0, pl.ds(k0, tq), :].astype(BF16)
        vblk = v_ref[0, pl.ds(k0, tq), :].astype(BF16)
        ck = cr_ref[0, h, pl.ds(kb, 1), :]
        s = _dot_nt(q, kblk) * SCALE + (cq - ck)
        mask = (kb - qi) * tq + jj <= ii
        s = jnp.where(mask, s, NEG_INF)
        m_new = jnp.maximum(m, jnp.max(s, axis=1, keepdims=True))
        alpha = jnp.exp(m - m_new)
        p = jnp.where(mask, jnp.exp(s - m_new), 0.0)
        l = alpha * l + jnp.sum(p, axis=1, keepdims=True)
        acc = alpha * acc + _dot(p.astype(BF16), vblk)
        return m_new, l, acc

    init = (jnp.full((tq, 1), NEG_INF, F32), jnp.zeros((tq, 1), F32), jnp.zeros((tq, HD), F32))
    m, l, acc = lax.fori_loop(0, qi + 1, body, init)
    o_ref[0] = (acc / jnp.where(l > 0.0, l, 1.0)).astype(o_ref.dtype)


def fox_prompt(qf, foxkv, c_tok, c_row):
    B, T, _ = qf.shape
    tq = _pick(T, (256, 128))
    c_row4 = c_row.reshape(B, FOX_HEADS, T // tq, tq)
    return pl.pallas_call(
        functools.partial(_fox_prompt_kernel, tq=tq),
        grid=(B, FOX_HEADS, T // tq),
        in_specs=[pl.BlockSpec((1, tq, HD), lambda b, h, i: (b, i, h)),
                  pl.BlockSpec((1, T, HD), lambda b, h, i: (b, 0, h)),
                  pl.BlockSpec((1, T, HD), lambda b, h, i: (b, 0, FOX_HEADS + h)),
                  pl.BlockSpec((1, tq, LANES), lambda b, h, i: (b, i, 0)),
                  pl.BlockSpec((1, FOX_HEADS, T // tq, tq), lambda b, h, i: (b, 0, 0, 0))],
        out_specs=pl.BlockSpec((1, tq, HD), lambda b, h, i: (b, i, h)),
        out_shape=jax.ShapeDtypeStruct((B, T, FOX_HEADS * HD), BF16),
        compiler_params=_cparams(("parallel", "parallel", "arbitrary")),
        name="fox_prompt",
    )(qf, foxkv, foxkv, c_tok, c_row4)


def _cmp_p_kernel(pt_ref, *refs, npg):
    page_refs = refs[:npg * 2 * G]
    w_ref, o_ref, lhs_ref = refs[npg * 2 * G:]
    for i in range(npg):
        for c in range(2 * G):
            for t in range(CMP_STRIDE):
                lhs_ref[c, i * CHUNKS_PER_PAGE:(i + 1) * CHUNKS_PER_PAGE, t * HD:(t + 1) * HD] = (
                    page_refs[i * 2 * G + c][pl.ds(t, CHUNKS_PER_PAGE, stride=CMP_STRIDE), :])
    for kv in range(2):
        for g in range(G):
            o_ref[0, kv, g] = _dot(lhs_ref[kv * G + g].astype(BF16), w_ref[kv])


def compress_partial(pages, page_table, layer, w1cat):
    NB, n_pages = page_table.shape
    npg = _pick(n_pages, (16, 8, 4, 2, 1))
    rows = npg * CHUNKS_PER_PAGE
    kdim = CMP_STRIDE * HD
    in_specs = []
    for i in range(npg):
        for c in range(2 * G):
            in_specs.append(pl.BlockSpec((None, None, 128, HD),
                                         lambda b, p, pt, i=i, c=c: (pt[b, p * npg + i], layer, 0, c)))
    in_specs.append(pl.BlockSpec((2, kdim, 2 * HD), lambda b, p, pt: (0, 0, 0)))
    return pl.pallas_call(
        functools.partial(_cmp_p_kernel, npg=npg),
        grid_spec=pltpu.PrefetchScalarGridSpec(
            num_scalar_prefetch=1,
            grid=(NB, n_pages // npg),
            in_specs=in_specs,
            out_specs=pl.BlockSpec((1, 2, G, rows, 2 * HD), lambda b, p, pt: (b, 0, 0, p, 0)),
            scratch_shapes=[pltpu.VMEM((2 * G, rows, kdim), F32)]),
        out_shape=jax.ShapeDtypeStruct((NB, 2, G, n_pages * CHUNKS_PER_PAGE, 2 * HD), F32),
        compiler_params=_cparams(("parallel", "arbitrary")),
        name="compress_partial",
    )(page_table, *([pages] * (npg * 2 * G)), w1cat)


def _gelu_tanh(x):
    return x * (0.5 * (1.0 + jnp.tanh(math.sqrt(2.0 / math.pi) * (x + 0.044715 * (x * x * x)))))


def _cmp_fin_kernel(p_ref, pos_ref, w1_ref, b1_ref, w2_ref, gk_ref, kc_ref, vc_ref):
    NC = p_ref.shape[3]
    row = _iota((NC, HD), 0)
    for kv in range(2):
        w1 = w1_ref[kv]
        posterm = (_dot(pos_ref[kv, 0], w1[:, :HD]) + _dot(pos_ref[kv, 1], w1[:, HD:]))[0:1, :] + b1_ref[kv]
        for g in range(G):
            P = p_ref[0, kv, g]
            pre = P[:, :HD] + pltpu.roll(P[:, HD:], NC - 1, 0) + posterm
            o = _dot(_gelu_tanh(pre).astype(BF16), w2_ref[kv])
            if kv == 0:
                o = _rms(o, gk_ref[...])
            o = jnp.where(row < NC - 1, o, 0.0)
            if kv == 0:
                kc_ref[0, g] = o
            else:
                vc_ref[0, g] = o


def compress_finish(part, pos_ab, w1cat, b1, w2, gain_kc):
    NB, _, _, NC, _ = part.shape
    kdim = CMP_STRIDE * HD
    full = lambda shape: pl.BlockSpec(shape, lambda b: (0,) * len(shape))
    return pl.pallas_call(
        _cmp_fin_kernel,
        grid=(NB,),
        in_specs=[pl.BlockSpec((1, 2, G, NC, 2 * HD), lambda b: (b, 0, 0, 0, 0)),
                  full((2, 2, SUBLANES, kdim)), full((2, kdim, 2 * HD)), full((2, 1, HD)),
                  full((2, HD, HD)), full((1, HD))],
        out_specs=[pl.BlockSpec((1, G, NC, HD), lambda b: (b, 0, 0, 0)),
                   pl.BlockSpec((1, G, NC, HD), lambda b: (b, 0, 0, 0))],
        out_shape=[jax.ShapeDtypeStruct((NB, G, NC, HD), F32), jax.ShapeDtypeStruct((NB, G, NC, HD), F32)],
        compiler_params=_cparams(("parallel",)),
        name="compress_finish",
    )(part, pos_ab, w1cat, b1, w2, gain_kc)


def _sel_overlap_matrix(nc, nsb, n_cb):
    ci = _iota((nc, nsb), 0)
    jb = _iota((nc, nsb), 1)
    ratio = SEL_BLOCK // CMP_STRIDE
    first = ratio * jb - (CMP_LEN // CMP_STRIDE) + 1
    n_ov = (SEL_BLOCK + CMP_LEN) // CMP_STRIDE - 1
    hit = (ci >= first) & (ci < first + n_ov) & (ci < n_cb)
    return jnp.where(hit, 1.0, 0.0).astype(BF16)


def _div_pow2(x, d):
    if isinstance(x, int):
        return x // d
    return lax.shift_right_arithmetic(x, jnp.full(x.shape, d.bit_length() - 1, I32))


def _block_scores(imp_s, qpos, jb, n_sb):
    back = _div_pow2(qpos, SEL_BLOCK) - jb
    forced = (jb == 0) | ((back >= 0) & (back < N_LOCAL_BLOCKS))
    score = jnp.where(back >= 0, imp_s + jnp.where(forced, FORCE_BONUS, 0.0), -1.0)
    return jnp.where(jb < n_sb, score, -2.0)


def _nsa_prompt_kernel(tbl_ref, q_ref, kc_ref, vc_ref, ks_ref, vs_ref, kw_ref, vw_ref, br_ref, bt_ref,
                       o_ref, exp_ref, *, T, n_cb, n_sb, n_sel):
    g = pl.program_id(1)
    qi = pl.program_id(2)
    tq = LANES
    qf = q_ref[0]
    qs = jnp.concatenate([qf[:, r * HD:(r + 1) * HD] for r in range(R)], axis=0).astype(BF16)
    ii = _iota((tq, LANES), 0)
    jj = _iota((tq, LANES), 1)
    qpos = qi * tq + ii

    dist_c = qpos - (jj * CMP_STRIDE + (CMP_LEN - 1))
    valid_c = dist_c >= 0
    bkt = _bucket(dist_c)
    bias_c = jnp.concatenate([_lookup(bkt, [tbl_ref[k, g * R + r] for k in range(N_BUCKETS)]) for r in range(R)],
                             axis=0).reshape(R, tq, LANES)
    s = _dot_nt(qs, kc_ref[0, 0].astype(BF16)).reshape(R, tq, LANES) * SCALE + bias_c
    s = jnp.where(valid_c[None], s, NEG_INF)
    m = jnp.max(s, axis=-1, keepdims=True)
    e = jnp.where(valid_c[None], jnp.exp(s - m), 0.0)
    l = jnp.sum(e, axis=-1, keepdims=True)
    p = e / jnp.where(l > 0.0, l, 1.0)
    o_c = _dot(p.reshape(R * tq, LANES).astype(BF16), vc_ref[0, 0].astype(BF16)).reshape(R, tq, HD)

    imp_c = p[0] + p[1] + p[2] + p[3]
    imp_s = _dot_split_lhs(imp_c, _sel_overlap_matrix(LANES, LANES, n_cb))
    score = _block_scores(imp_s, qpos, jj, n_sb)
    rank = jnp.zeros((tq, LANES), I32)
    for i in range(n_sb):
        col = score[:, i:i + 1]
        rank = rank + jnp.where(col > score, 1, jnp.where((col == score) & (jj > i), 1, 0))
    sel = jnp.where(rank < n_sel, 1.0, 0.0).astype(BF16)
    jb2 = _iota((LANES, LANES), 0)
    kk = _iota((LANES, LANES), 1)
    for kbi in range(T // LANES):
        expand = jnp.where(_div_pow2(kbi * LANES + kk, SEL_BLOCK) == jb2, 1.0, 0.0).astype(BF16)
        exp_ref[kbi] = _dot(sel, expand)

    def flash(k_ref, v_ref, lo, hi, maskfn):
        def body(kb, carry):
            m, l, acc = carry
            k0 = pl.multiple_of(kb * LANES, LANES)
            kblk = k_ref[0, pl.ds(k0, LANES), :].astype(BF16)
            vblk = v_ref[0, pl.ds(k0, LANES), :].astype(BF16)
            off = qi - kb
            bias = bt_ref[0, jnp.minimum(off, 2)].reshape(R, tq, LANES)
            s = _dot_nt(qs, kblk).reshape(R, tq, LANES) * SCALE + bias
            mask = maskfn(kb, off * LANES + ii - jj)[None]
            s = jnp.where(mask, s, NEG_INF)
            m_new = jnp.maximum(m, jnp.max(s, axis=-1, keepdims=True))
            alpha = jnp.exp(m - m_new)
            p = jnp.where(mask, jnp.exp(s - m_new), 0.0)
            l = alpha * l + jnp.sum(p, axis=-1, keepdims=True)
            pv = _dot(p.reshape(R * tq, LANES).astype(BF16), vblk).reshape(R, tq, HD)
            return m_new, l, alpha * acc + pv

        init = (jnp.full((R, tq, 1), NEG_INF, F32), jnp.zeros((R, tq, 1), F32), jnp.zeros((R, tq, HD), F32))
        m, l, acc = lax.fori_loop(lo, hi, body, init)
        return acc / jnp.where(l > 0.0, l, 1.0)

    o_s = flash(ks_ref, vs_ref, 0, qi + 1, lambda kb, d: (exp_ref[kb] > 0.5) & (d >= 0))
    o_w = flash(kw_ref, vw_ref, jnp.maximum(qi - WINDOW // LANES, 0), qi + 1,
                lambda kb, d: (d >= 0) & (d < WINDOW))

    brt = br_ref[0]

    def col(idx):
        return jnp.sum(jnp.where(jj == idx, brt, 0.0), axis=1, keepdims=True)

    for r in range(R):
        c0 = (g * R + r) * 3
        o = col(c0) * o_c[r] + col(c0 + 1) * o_s[r] + col(c0 + 2) * o_w[r]
        o_ref[0, :, r * HD:(r + 1) * HD] = o.astype(o_ref.dtype)


def nsa_prompt(table, qa, kc, vc, nsakv, win, small, btiles):
    B, T, _ = qa.shape
    assert T % LANES == 0 and T // CMP_STRIDE == LANES and kc.shape[2] == LANES
    n_cb = T // CMP_STRIDE - CMP_LEN // CMP_STRIDE + 1
    n_sb = -(-T // SEL_BLOCK)
    n_sel = min(SEL_TOPN, n_sb)
    tq = LANES
    return pl.pallas_call(
        functools.partial(_nsa_prompt_kernel, T=T, n_cb=n_cb, n_sb=n_sb, n_sel=n_sel),
        grid=(B, G, T // tq),
        in_specs=[pl.BlockSpec(memory_space=pltpu.SMEM),
                  pl.BlockSpec((1, tq, R * HD), lambda b, g, i: (b, i, g)),
                  pl.BlockSpec((1, 1, LANES, HD), lambda b, g, i: (b, g, 0, 0)),
                  pl.BlockSpec((1, 1, LANES, HD), lambda b, g, i: (b, g, 0, 0)),
                  pl.BlockSpec((1, T, HD), lambda b, g, i: (b, 0, 2 * G + g)),
                  pl.BlockSpec((1, T, HD), lambda b, g, i: (b, 0, 3 * G + g)),
                  pl.BlockSpec((1, T, HD), lambda b, g, i: (b, 0, g)),
                  pl.BlockSpec((1, T, HD), lambda b, g, i: (b, 0, G + g)),
                  pl.BlockSpec((1, tq, LANES), lambda b, g, i: (b, i, 0)),
                  pl.BlockSpec((1, 3, R * LANES, LANES), lambda b, g, i: (g, 0, 0, 0))],
        out_specs=pl.BlockSpec((1, tq, R * HD), lambda b, g, i: (b, i, g)),
        out_shape=jax.ShapeDtypeStruct((B, T, NSA_HEADS * HD), BF16),
        scratch_shapes=[pltpu.VMEM((T // LANES, tq, LANES), F32)],
        compiler_params=_cparams(("parallel", "parallel", "arbitrary")),
        name="nsa_prompt",
    )(table, qa, kc, vc, nsakv, nsakv, win, win, small, btiles)


def _dec_cmp_kernel(q_ref, kc_ref, vc_ref, tblt_ref, oc_ref, idx_ref, *, qpos, n_cb, n_sb, n_sel, nsbp):
    NC = kc_ref.shape[2]
    q8 = q_ref[0].astype(BF16)
    row = _iota((NSA_HEADS, NC), 0)
    j = _iota((NSA_HEADS, NC), 1)
    s = jnp.where(row < R, _dot_nt(q8, kc_ref[0, 0].astype(BF16)), _dot_nt(q8, kc_ref[0, 1].astype(BF16)))
    dist = qpos - (j * CMP_STRIDE + (CMP_LEN - 1))
    valid = dist >= 0
    tblt = tblt_ref[...]
    bias = _lookup(_bucket(dist), [tblt[:, k:k + 1] for k in range(N_BUCKETS)])
    s = jnp.where(valid, s * SCALE + bias, NEG_INF)
    m = jnp.max(s, axis=1, keepdims=True)
    e = jnp.where(valid, jnp.exp(s - m), 0.0)
    l = jnp.sum(e, axis=1, keepdims=True)
    p = e / jnp.where(l > 0.0, l, 1.0)
    pb = p.astype(BF16)
    row_o = _iota((NSA_HEADS, HD), 0)
    oc_ref[0] = jnp.where(row_o < R, _dot(pb, vc_ref[0, 0].astype(BF16)), _dot(pb, vc_ref[0, 1].astype(BF16)))

    imp0 = jnp.sum(jnp.where(row < R, p, 0.0), axis=0, keepdims=True)
    imp1 = jnp.sum(jnp.where(row >= R, p, 0.0), axis=0, keepdims=True)
    imp = jnp.where(row == 0, imp0, jnp.where(row == 1, imp1, 0.0))
    imp_s = _dot_split_lhs(imp, _sel_overlap_matrix(NC, nsbp, n_cb))
    jb = _iota((NSA_HEADS, nsbp), 1)
    score = _block_scores(imp_s, qpos, jb, n_sb)
    jbf = jb.astype(F32)
    lane_o = _iota((NSA_HEADS, LANES), 1)
    out = jnp.zeros((NSA_HEADS, LANES), I32)
    for n in range(n_sel):
        mx = jnp.max(score, axis=1, keepdims=True)
        am = jnp.min(jnp.where(score == mx, jbf, float(nsbp)), axis=1, keepdims=True)
        out = jnp.where(lane_o == n, am.astype(I32), out)
        score = jnp.where(jbf == am, -3.0, score)
    idx_ref[0] = out


def dec_cmp(q8, kc, vc, tblt, qpos):
    DB = q8.shape[0]
    NC = kc.shape[2]
    L = qpos + 1
    n_cb = L // CMP_STRIDE - CMP_LEN // CMP_STRIDE + 1
    assert n_cb == NC - 1
    n_sb = -(-L // SEL_BLOCK)
    n_sel = min(SEL_TOPN, n_sb)
    nsbp = -(-n_sb // LANES) * LANES
    oc, idx = pl.pallas_call(
        functools.partial(_dec_cmp_kernel, qpos=qpos, n_cb=n_cb, n_sb=n_sb, n_sel=n_sel, nsbp=nsbp),
        grid=(DB,),
        in_specs=[pl.BlockSpec((1, NSA_HEADS, HD), lambda b: (b, 0, 0)),
                  pl.BlockSpec((1, G, NC, HD), lambda b: (b, 0, 0, 0)),
                  pl.BlockSpec((1, G, NC, HD), lambda b: (b, 0, 0, 0)),
                  pl.BlockSpec((NSA_HEADS, N_BUCKETS), lambda b: (0, 0))],
        out_specs=[pl.BlockSpec((1, NSA_HEADS, HD), lambda b: (b, 0, 0)),
                   pl.BlockSpec((1, NSA_HEADS, LANES), lambda b: (b, 0, 0))],
        out_shape=[jax.ShapeDtypeStruct((DB, NSA_HEADS, HD), F32), jax.ShapeDtypeStruct((DB, NSA_HEADS, LANES), I32)],
        compiler_params=_cparams(("parallel",)),
        name="dec_cmp",
    )(q8, kc, vc, tblt)
    return oc, idx[:, :G, :n_sel].reshape(DB, G * n_sel), n_sel


def _softmax_with_new(s, valid, s_new, new_ok):
    s = jnp.where(valid, s, NEG_INF)
    s_new = jnp.where(new_ok, s_new, NEG_INF)
    m = jnp.maximum(jnp.max(s, axis=1, keepdims=True), s_new)
    e = jnp.where(valid, jnp.exp(s - m), 0.0)
    en = jnp.where(new_ok, jnp.exp(s_new - m), 0.0)
    l = jnp.sum(e, axis=1, keepdims=True) + en
    inv = 1.0 / jnp.where(l > 0.0, l, 1.0)
    return e, en, inv


def _dec_selwin_kernel(pt_ref, idx_ref, *refs, n_sel, qpos, jb_new, w_buf):
    blk_refs = refs[:2 * G * n_sel]
    (q_ref, new_ref, kw0_ref, kw1_ref, vw0_ref, vw1_ref, wnew_ref, br_ref, oc_ref, tblt_ref,
     o_ref) = refs[2 * G * n_sel:]
    b = pl.program_id(0)
    q8 = q_ref[0]
    row8 = _iota((NSA_HEADS, HD), 0)
    tblt = tblt_ref[...]
    tcols = [tblt[:, k:k + 1] for k in range(N_BUCKETS)]
    new8 = new_ref[0]
    wnew = wnew_ref[0]
    brrow = br_ref[0]
    oc = oc_ref[0]
    nk = n_sel * SEL_BLOCK
    lane = _iota((NSA_HEADS, nk), 1)
    outs = []
    for g in range(G):
        qg = jnp.where(_div_pow2(row8, R) == g, q8, 0.0)
        qb = qg.astype(BF16)
        qr = qb.astype(F32)
        kmat = jnp.concatenate([blk_refs[(g * n_sel + n) * 2][...] for n in range(n_sel)], axis=0).astype(BF16)
        vmat = jnp.concatenate([blk_refs[(g * n_sel + n) * 2 + 1][...] for n in range(n_sel)], axis=0).astype(BF16)
        idxv = jnp.zeros((NSA_HEADS, nk), I32)
        has_new = jnp.zeros((), jnp.bool_)
        for n in range(n_sel):
            sidx = idx_ref[b, g * n_sel + n]
            idxv = jnp.where(_div_pow2(lane, SEL_BLOCK) == n, sidx, idxv)
            has_new = has_new | (sidx == jb_new)
        dist = qpos - (idxv * SEL_BLOCK + (lane & (SEL_BLOCK - 1)))
        valid = (idxv < jb_new) & (dist >= 0)
        s = _dot_nt(qb, kmat) * SCALE + _lookup(_bucket(dist), tcols)
        kn = new8[2 * G + g:2 * G + g + 1, :].astype(BF16).astype(F32)
        vn = new8[3 * G + g:3 * G + g + 1, :].astype(BF16).astype(F32)
        s_new = jnp.sum(qr * kn, axis=1, keepdims=True) * SCALE + tcols[0]
        e, en, inv = _softmax_with_new(s, valid, s_new, has_new)
        o_s = (_dot(e.astype(BF16), vmat) + en * vn) * inv

        kw_ref = kw0_ref if g == 0 else kw1_ref
        vw_ref = vw0_ref if g == 0 else vw1_ref
        lw = _iota((NSA_HEADS, w_buf), 1)
        dist_w = w_buf - lw
        valid_w = (dist_w < WINDOW) & (qpos - dist_w >= 0)
        sw = _dot_nt(qb, kw_ref[...].astype(BF16)) * SCALE + _lookup(_bucket(dist_w), tcols)
        kwn = wnew[g:g + 1, :].astype(BF16).astype(F32)
        vwn = wnew[G + g:G + g + 1, :].astype(BF16).astype(F32)
        sw_new = jnp.sum(qr * kwn, axis=1, keepdims=True) * SCALE + tcols[0]
        ew, ewn, invw = _softmax_with_new(sw, valid_w, sw_new, jnp.ones((), jnp.bool_))
        o_w = (_dot(ew.astype(BF16), vw_ref[...].astype(BF16)) + ewn * vwn) * invw

        def bcol(i):
            return jnp.sum(jnp.where(_iota((NSA_HEADS, LANES), 1) == row8 * 3 + i, brrow, 0.0), axis=1, keepdims=True)

        outs.append(bcol(0) * oc + bcol(1) * o_s + bcol(2) * o_w)
    o_ref[0] = jnp.where(row8 < R, outs[0], outs[1])


def dec_selwin(q8, new8, wnew4, small_s, oc, tblt, cache_nsa_kv, state_win, page_table, idx, n_sel, layer, qpos):
    DB = q8.shape[0]
    n_pool, depth = cache_nsa_kv.shape[:2]
    page = cache_nsa_kv.shape[2]
    halves = page // SEL_BLOCK
    w_buf = state_win.shape[2]
    n_pages = page_table.shape[1]
    jb_new = qpos // SEL_BLOCK
    cache5 = cache_nsa_kv.reshape(n_pool, depth, halves, SEL_BLOCK, 4 * G * HD)
    state4 = state_win.reshape(depth, DB, w_buf, 2 * G * HD)
    in_specs = []
    args = []
    for g in range(G):
        for n in range(n_sel):
            for comp in (2, 3):
                def imap(b, pt, ix, g=g, n=n, comp=comp):
                    jb = jnp.minimum(ix[b, g * n_sel + n], jb_new - 1)
                    return (pt[b, jb // halves], layer, jb % halves, 0, comp * G + g)
                in_specs.append(pl.BlockSpec((None, None, None, SEL_BLOCK, HD), imap))
                args.append(cache5)
    row3 = lambda b, pt, ix: (b, 0, 0)
    in_specs += [pl.BlockSpec((1, NSA_HEADS, HD), row3), pl.BlockSpec((1, 4 * G, HD), row3)]
    for c in range(2):
        for g in range(G):
            in_specs.append(pl.BlockSpec((None, None, w_buf, HD), lambda b, pt, ix, c=c, g=g: (layer, b, 0, c * G + g)))
    in_specs += [pl.BlockSpec((1, 2 * G, HD), row3), pl.BlockSpec((1, 1, LANES), row3),
                 pl.BlockSpec((1, NSA_HEADS, HD), row3),
                 pl.BlockSpec((NSA_HEADS, N_BUCKETS), lambda b, pt, ix: (0, 0))]
    args += [q8, new8, state4, state4, state4, state4, wnew4, small_s.reshape(DB, 1, LANES), oc, tblt]
    return pl.pallas_call(
        functools.partial(_dec_selwin_kernel, n_sel=n_sel, qpos=qpos, jb_new=jb_new, w_buf=w_buf),
        grid_spec=pltpu.PrefetchScalarGridSpec(
            num_scalar_prefetch=2,
            grid=(DB,),
            in_specs=in_specs,
            out_specs=pl.BlockSpec((1, NSA_HEADS, HD), row3)),
        out_shape=jax.ShapeDtypeStruct((DB, NSA_HEADS, HD), F32),
        compiler_params=_cparams(("arbitrary",)),
        name="dec_selwin",
    )(page_table, idx, *args)


def _dec_fox_kernel(pt_ref, *refs, npg):
    kv_refs = refs[:npg]
    lf_refs = refs[npg:2 * npg]
    q_ref, new_ref, lfn_ref, o_ref, m_sc, l_sc, acc_sc, car_sc = refs[2 * npg:]
    p = pl.program_id(1)
    nh = FOX_HEADS
    q8 = q_ref[0]
    row = _iota((nh, nh * HD), 0)
    lane = _iota((nh, nh * HD), 1)
    qbd = jnp.where(_div_pow2(lane, HD) == row, jnp.concatenate([q8] * nh, axis=1), 0.0).astype(BF16)

    @pl.when(p == 0)
    def _():
        new = new_ref[0]
        kn = new[0:nh].astype(BF16).astype(F32)
        vn = new[nh:2 * nh]
        m_sc[...] = jnp.sum(q8.astype(BF16).astype(F32) * kn, axis=1, keepdims=True) * SCALE
        l_sc[...] = jnp.ones((nh, 1), F32)
        acc_sc[...] = jnp.concatenate([vn] * nh, axis=1)
        lane1 = _iota((nh, LANES), 1)
        row1 = _iota((nh, LANES), 0)
        car_sc[...] = jnp.sum(jnp.where(lane1 == LOGF_LANE0 + row1, lfn_ref[0], 0.0), axis=1, keepdims=True)

    uu = _iota((LANES, LANES), 0)
    ss = _iota((LANES, LANES), 1)
    later = jnp.where(uu > ss, 1.0, 0.0).astype(BF16)
    m = m_sc[...]
    l = l_sc[...]
    acc = acc_sc[...]
    car = car_sc[...]
    for i in range(npg):
        kmat = kv_refs[i][:, :nh * HD].astype(BF16)
        vmat = kv_refs[i][:, nh * HD:].astype(BF16)
        lft = lf_refs[i][...]
        decay = _dot_split_lhs(lft, later) + car
        s = _dot_nt(qbd, kmat) * SCALE + decay
        m_new = jnp.maximum(m, jnp.max(s, axis=1, keepdims=True))
        alpha = jnp.exp(m - m_new)
        pp = jnp.exp(s - m_new)
        l = alpha * l + jnp.sum(pp, axis=1, keepdims=True)
        acc = alpha * acc + _dot(pp.astype(BF16), vmat)
        m = m_new
        car = car + jnp.sum(lft, axis=1, keepdims=True)
    m_sc[...] = m
    l_sc[...] = l
    acc_sc[...] = acc
    car_sc[...] = car

    @pl.when(p == pl.num_programs(1) - 1)
    def _():
        out = acc / l
        rowo = _iota((nh, HD), 0)
        res = jnp.zeros((nh, HD), F32)
        for h in range(nh):
            res = jnp.where(rowo == h, out[:, h * HD:(h + 1) * HD], res)
        o_ref[0] = res


def dec_fox(q8, new16, small_s, cache_fox_kv, logf_t, page_table, layer):
    DB = q8.shape[0]
    n_pool, depth, page = cache_fox_kv.shape[:3]
    assert page == LANES
    n_pages = page_table.shape[1]
    npg = _pick(n_pages, (8, 4, 2, 1))
    width = 2 * FOX_HEADS * HD
    cache4 = cache_fox_kv.reshape(n_pool, depth, page, width)
    in_specs = []
    for i in range(npg):
        in_specs.append(pl.BlockSpec((None, None, page, width),
                                     lambda b, p, pt, i=i: (pt[b, n_pages - 1 - (p * npg + i)], layer, 0, 0)))
    for i in range(npg):
        in_specs.append(pl.BlockSpec((None, None, FOX_HEADS, page),
                                     lambda b, p, pt, i=i: (pt[b, n_pages - 1 - (p * npg + i)], layer, 0, 0)))
    row3 = lambda b, p, pt: (b, 0, 0)
    in_specs += [pl.BlockSpec((1, FOX_HEADS, HD), row3), pl.BlockSpec((1, 2 * FOX_HEADS, HD), row3),
                 pl.BlockSpec((1, 1, LANES), row3)]
    return pl.pallas_call(
        functools.partial(_dec_fox_kernel, npg=npg),
        grid_spec=pltpu.PrefetchScalarGridSpec(
            num_scalar_prefetch=1,
            grid=(DB, n_pages // npg),
            in_specs=in_specs,
            out_specs=pl.BlockSpec((1, FOX_HEADS, HD), row3),
            scratch_shapes=[pltpu.VMEM((FOX_HEADS, 1), F32), pltpu.VMEM((FOX_HEADS, 1), F32),
                            pltpu.VMEM((FOX_HEADS, FOX_HEADS * HD), F32), pltpu.VMEM((FOX_HEADS, 1), F32)]),
        out_shape=jax.ShapeDtypeStruct((DB, FOX_HEADS, HD), F32),
        compiler_params=_cparams(("parallel", "arbitrary")),
        name="dec_fox",
    )(page_table, *([cache4] * npg), *([logf_t] * npg), q8, new16, small_s.reshape(DB, 1, LANES))


def _layer_weights(l, D, w_in, b_forget, qk_gain_nsa, qk_gain_fox):
    nq = NSA_HEADS * HD
    nkv = 6 * G * HD
    nfox = 3 * FOX_HEADS * HD
    o_br = nq + nkv
    o_fox = o_br + BR_LANES
    o_f = o_fox + nfox
    o_mg = o_f + FOX_HEADS
    w = w_in[l]
    ones = jnp.ones((HD,), F32)
    gn = qk_gain_nsa[l]
    gf = qk_gain_fox[l]

    def seg(lo, hi):
        return w[:, lo:hi].astype(BF16)

    def gains(rows):
        return jnp.concatenate(rows).reshape(1, -1)

    zeros = lambda n: jnp.zeros((1, n), F32)
    small_w = jnp.concatenate([w[:, o_br:o_br + BR_LANES], w[:, o_f:o_f + FOX_HEADS],
                               jnp.zeros((D, LANES - BR_LANES - FOX_HEADS), F32)], axis=1).astype(BF16)
    small_b = jnp.concatenate([jnp.zeros((BR_LANES,), F32), b_forget[l],
                               jnp.zeros((LANES - BR_LANES - FOX_HEADS,), F32)]).reshape(1, LANES)
    kvh = G
    return {
        "qa": (seg(0, nq), gains([gn[0]] * NSA_HEADS), zeros(nq), ("norm",) * NSA_HEADS, nq),
        "nsakv": (seg(nq, nq + 4 * kvh * HD), gains([ones] * (2 * kvh) + [gn[2]] * kvh + [ones] * kvh),
                  zeros(4 * kvh * HD), ("raw",) * (2 * kvh) + ("norm",) * kvh + ("raw",) * kvh, 4 * kvh * HD),
        "win": (seg(nq + 4 * kvh * HD, o_br), gains([gn[3]] * kvh + [ones] * kvh), zeros(2 * kvh * HD),
                ("norm",) * kvh + ("raw",) * kvh, 2 * kvh * HD),
        "qf": (seg(o_fox, o_fox + FOX_HEADS * HD), gains([gf[0]] * FOX_HEADS), zeros(FOX_HEADS * HD),
               ("norm",) * FOX_HEADS, FOX_HEADS * HD),
        "foxkv": (seg(o_fox + FOX_HEADS * HD, o_f), gains([gf[1]] * FOX_HEADS + [ones] * FOX_HEADS),
                  zeros(2 * FOX_HEADS * HD), ("norm",) * FOX_HEADS + ("raw",) * FOX_HEADS, 2 * FOX_HEADS * HD),
        "gates": (seg(o_mg, o_mg + 2 * D), zeros(2 * D), zeros(2 * D),
                  ("sigmoid",) * (_pick(2 * D, (1024, 512, 256, 128)) // LANES), _pick(2 * D, (1024, 512, 256, 128))),
        "small": (small_w, zeros(LANES), small_b, ("small",), LANES),
    }


def _project_all(h, segs):
    return {name: proj(h, w, gn, bs, modes, tn) for name, (w, gn, bs, modes, tn) in segs.items()}


def _tail(x, o_nsa, o_fox, gates, pe, gains_l, gain_next, wbn, wbf, wo, wfi, wfo, wple, wpg):
    merged = merge(o_nsa, o_fox, gates, wbn, wbf)
    x1, h1 = out_proj(merged, wo, x, gains_l[1])
    act = ffn_in(h1, wfi)
    x2 = ffn_out(act, wfo, x1)
    return ple_gate(x2, gains_l[2], wpg, pe, wple, gain_next)


def kernel(x_prompt, x_sample, cache_nsa_kv, cache_fox_kv, cache_fox_logf, state_nsa_window, page_table, p_prompt, p_sample, rel_bias_table, norm_gains, w_in, b_forget, qk_gain_nsa, qk_gain_fox, cmp_pos, w_cmp1, b_cmp1, w_cmp2, w_branch_nsa, w_branch_fox, w_out, w_ffn_in, w_ffn_out, w_ple, w_ple_gate):
    B, T, D = x_prompt.shape
    DB, n_qs, _ = x_sample.shape
    assert n_qs == 1
    depth = w_in.shape[0]
    page = cache_nsa_kv.shape[2]
    n_pages = page_table.shape[1]
    past = n_pages * page
    w_buf = state_nsa_window.shape[2]
    kdim = CMP_STRIDE * HD

    xp = x_prompt.reshape(B * T, D)
    xs = x_sample.reshape(DB, D)
    btiles = bias_tiles(rel_bias_table)
    tblt = rel_bias_table.T
    logf_t = jnp.swapaxes(cache_fox_logf, 2, 3)
    pt_prompt = jnp.arange(B * (T // page), dtype=I32).reshape(B, T // page)
    hp = rms_bf16(xp, norm_gains[0, 0])
    hs = rms_bf16(xs, norm_gains[0, 0])

    outs = [[] for _ in range(8)]
    for l in range(depth):
        segs = _layer_weights(l, D, w_in, b_forget, qk_gain_nsa, qk_gain_fox)
        w1 = w_cmp1[l]
        w1cat = jnp.concatenate([w1[:, :kdim], w1[:, kdim:]], axis=2).astype(BF16)
        pos = cmp_pos[l].reshape(2, 2, 1, kdim)
        pos_ab = jnp.broadcast_to(pos, (2, 2, SUBLANES, kdim)).astype(BF16)
        cmp_args = (pos_ab, w1cat, b_cmp1[l].reshape(2, 1, HD), w_cmp2[l].astype(BF16),
                    qk_gain_nsa[l, 1].reshape(1, HD))
        gain_next = norm_gains[l + 1, 0] if l + 1 < depth else norm_gains[l, 0]
        tail_w = (norm_gains[l], gain_next, w_branch_nsa[l].astype(BF16), w_branch_fox[l].astype(BF16),
                  w_out[l].astype(BF16), w_ffn_in[l].astype(BF16), w_ffn_out[l].astype(BF16),
                  w_ple[l].astype(BF16), w_ple_gate[l].astype(BF16))

        P = _project_all(hp, segs)
        small3 = P["small"].reshape(B, T, LANES)
        c_tok, c_row = cumsum_logf(small3)
        o_fox_p = fox_prompt(P["qf"].reshape(B, T, -1), P["foxkv"].reshape(B, T, -1), c_tok, c_row)
        part = compress_partial(P["nsakv"].reshape(B * (T // page), 1, page, 4 * G * HD), pt_prompt, 0, w1cat)
        kc_p, vc_p = compress_finish(part, *cmp_args)
        o_nsa_p = nsa_prompt(rel_bias_table, P["qa"].reshape(B, T, -1), kc_p, vc_p,
                             P["nsakv"].reshape(B, T, -1), P["win"].reshape(B, T, -1), small3, btiles)
        xp, hp = _tail(xp, o_nsa_p.reshape(B * T, -1), o_fox_p.reshape(B * T, -1), P["gates"],
                       p_prompt[l].reshape(B * T, -1), *tail_w)

        S = _project_all(hs, segs)
        part_s = compress_partial(cache_nsa_kv.reshape(cache_nsa_kv.shape[0], depth, page, 4 * G * HD),
                                  page_table, l, w1cat)
        kc_s, vc_s = compress_finish(part_s, *cmp_args)
        q8 = S["qa"].reshape(DB, NSA_HEADS, HD)
        oc_s, idx, n_sel = dec_cmp(q8, kc_s, vc_s, tblt, past)
        o_nsa_s = dec_selwin(q8, S["nsakv"].reshape(DB, 4 * G, HD), S["win"].reshape(DB, 2 * G, HD), S["small"],
                             oc_s, tblt, cache_nsa_kv, state_nsa_window, page_table, idx, n_sel, l, past)
        o_fox_s = dec_fox(S["qf"].reshape(DB, FOX_HEADS, HD), S["foxkv"].reshape(DB, 2 * FOX_HEADS, HD),
                          S["small"], cache_fox_kv, logf_t, page_table, l)
        xs, hs = _tail(xs, o_nsa_s.reshape(DB, -1).astype(BF16), o_fox_s.reshape(DB, -1).astype(BF16), S["gates"],
                       p_sample[l].reshape(DB, -1), *tail_w)

        lf0, lf1 = LOGF_LANE0, LOGF_LANE0 + FOX_HEADS
        outs[0].append(P["nsakv"].reshape(B, T, 4, G, HD))
        outs[1].append(P["win"].reshape(B, T, 2, G, HD)[:, T - min(WINDOW, T):])
        outs[2].append(P["foxkv"].reshape(B, T, 2, FOX_HEADS, HD))
        outs[3].append(P["small"][:, lf0:lf1].reshape(B, T, FOX_HEADS))
        outs[4].append(S["nsakv"].reshape(DB, 1, 4, G, HD))
        win_all = jnp.concatenate([state_nsa_window[l], S["win"].reshape(DB, 1, 2, G, HD)], axis=1)
        outs[5].append(win_all[:, w_buf + 1 - min(WINDOW, w_buf + 1):])
        outs[6].append(S["foxkv"].reshape(DB, 1, 2, FOX_HEADS, HD))
        outs[7].append(S["small"][:, lf0:lf1].reshape(DB, 1, FOX_HEADS))

    return (xp.reshape(B, T, D), xs.reshape(DB, 1, D)) + tuple(jnp.stack(o) for o in outs)
```

```python
import functools
import math

import jax
import jax.numpy as jnp
from jax import lax
from jax.experimental import pallas as pl
from jax.experimental.pallas import tpu as pltpu

F32 = jnp.float32
BF16 = jnp.bfloat16
I32 = jnp.int32

HEAD_DIM = 128
NSA_HEADS = 8
NSA_KV_HEADS = 2
NSA_GROUP = NSA_HEADS // NSA_KV_HEADS
FOX_HEADS = 8
CMP_LEN = 32
CMP_STRIDE = 16
SEL_BLOCK = 64
SEL_TOPN = 16
N_LOCAL_BLOCKS = 2
WINDOW = 512
N_BUCKETS = 32
BUCKET_EXACT = N_BUCKETS // 2
MAX_DISTANCE = 128
EPS = 1e-6
SCALE = HEAD_DIM ** -0.5
NEG_INF = -1e30
FORCE_BONUS = 1e4

LANES = 128
SUBLANES = 8
VMEM_LIMIT_MB = 52

G = NSA_KV_HEADS
R = NSA_GROUP
HD = HEAD_DIM
BR_LANES = 3 * NSA_HEADS
LOGF_LANE0 = BR_LANES
CHUNKS_PER_PAGE = 128 // CMP_STRIDE


def _cparams(sem, vmem_mb=VMEM_LIMIT_MB):
    return pltpu.CompilerParams(dimension_semantics=sem, vmem_limit_bytes=vmem_mb * 1024 * 1024)


def _pick(n, prefs):
    for p in prefs:
        if n % p == 0:
            return p
    return n


def _iota(shape, dim):
    return lax.broadcasted_iota(I32, shape, dim)


def _dot(a, b):
    return jnp.dot(a, b, preferred_element_type=F32)


def _dot_nt(a, b):
    return lax.dot_general(a, b, (((1,), (1,)), ((), ())), preferred_element_type=F32)


def _split3(x):
    hi = x.astype(BF16)
    r1 = x - hi.astype(F32)
    mid = r1.astype(BF16)
    lo = (r1 - mid.astype(F32)).astype(BF16)
    return hi, mid, lo


def _dot_split_lhs(x, m01):
    hi, mid, lo = _split3(x)
    return _dot(hi, m01) + _dot(mid, m01) + _dot(lo, m01)


def _dot_split_rhs(m01, x):
    hi, mid, lo = _split3(x)
    return _dot(m01, hi) + _dot(m01, mid) + _dot(m01, lo)


def _rms(x, gain):
    return x * lax.rsqrt(jnp.mean(x * x, axis=-1, keepdims=True) + EPS) * gain


def _bucket(dist):
    n = jnp.maximum(dist, 0)
    rel = jnp.log(jnp.maximum(n, 1).astype(F32) / float(BUCKET_EXACT)) / math.log(MAX_DISTANCE / BUCKET_EXACT)
    large = jnp.minimum(BUCKET_EXACT + (rel * float(N_BUCKETS - BUCKET_EXACT)).astype(I32), N_BUCKETS - 1)
    return jnp.where(n < BUCKET_EXACT, n, large)


def _lookup(bucket, vals):
    out = vals[N_BUCKETS - 1]
    for k in range(N_BUCKETS - 1):
        out = jnp.where(bucket == k, vals[k], out)
    return out


def _rms_kernel(x_ref, g_ref, o_ref):
    o_ref[...] = _rms(x_ref[...], g_ref[...]).astype(o_ref.dtype)


def rms_bf16(x, gain):
    M, D = x.shape
    tm = _pick(M, (512, 256, 128, 8))
    return pl.pallas_call(
        _rms_kernel,
        grid=(M // tm,),
        in_specs=[pl.BlockSpec((tm, D), lambda i: (i, 0)), pl.BlockSpec((1, D), lambda i: (0, 0))],
        out_specs=pl.BlockSpec((tm, D), lambda i: (i, 0)),
        out_shape=jax.ShapeDtypeStruct((M, D), BF16),
        compiler_params=_cparams(("parallel",)),
        name="rms",
    )(x, gain.reshape(1, D))


def _proj_kernel(h_ref, w_ref, g_ref, b_ref, o_ref, *, modes):
    h = h_ref[...]
    nt = len(modes)
    t = 0
    while t < nt:
        wd = 2 if t + 1 < nt else 1
        acc = _dot(h, w_ref[:, t * LANES:(t + wd) * LANES])
        for u in range(wd):
            c0 = (t + u) * LANES
            v = acc[:, u * LANES:(u + 1) * LANES]
            mode = modes[t + u]
            if mode == "norm":
                v = _rms(v, g_ref[:, c0:c0 + LANES])
            elif mode == "sigmoid":
                v = jax.nn.sigmoid(v)
            elif mode == "small":
                lane = _iota(v.shape, 1)
                z = v + b_ref[:, c0:c0 + LANES]
                logsig = jnp.minimum(z, 0.0) - jnp.log(1.0 + jnp.exp(-jnp.abs(z)))
                v = jnp.where(lane < BR_LANES, jax.nn.sigmoid(v),
                              jnp.where(lane < LOGF_LANE0 + FOX_HEADS, logsig, 0.0))
            o_ref[:, c0:c0 + LANES] = v
        t += wd


def proj(h, w, gains, bias, modes, tn):
    M, K = h.shape
    N = w.shape[1]
    tm = _pick(M, (512, 256, 128, 8))
    assert len(modes) * LANES == tn and N % tn == 0
    return pl.pallas_call(
        functools.partial(_proj_kernel, modes=modes),
        grid=(M // tm, N // tn),
        in_specs=[pl.BlockSpec((tm, K), lambda i, j: (i, 0)),
                  pl.BlockSpec((K, tn), lambda i, j: (0, j)),
                  pl.BlockSpec((1, tn), lambda i, j: (0, j)),
                  pl.BlockSpec((1, tn), lambda i, j: (0, j))],
        out_specs=pl.BlockSpec((tm, tn), lambda i, j: (i, j)),
        out_shape=jax.ShapeDtypeStruct((M, N), F32),
        compiler_params=_cparams(("parallel", "arbitrary")),
        name="proj",
    )(h, w, gains, bias)


def _merge_kernel(on_ref, of_ref, ga_ref, gb_ref, wn_ref, wf_ref, o_ref):
    a = _dot(on_ref[...], wn_ref[...])
    b = _dot(of_ref[...], wf_ref[...])
    o_ref[...] = (ga_ref[...] * a + gb_ref[...] * b).astype(o_ref.dtype)


def merge(o_nsa, o_fox, gates, wbn, wbf):
    M, K = o_nsa.shape
    D = wbn.shape[1]
    tm = _pick(M, (512, 256, 128, 8))
    tn = _pick(D, (1024, 512, 256, 128))
    nj = D // tn
    return pl.pallas_call(
        _merge_kernel,
        grid=(M // tm, nj),
        in_specs=[pl.BlockSpec((tm, K), lambda i, j: (i, 0)),
                  pl.BlockSpec((tm, K), lambda i, j: (i, 0)),
                  pl.BlockSpec((tm, tn), lambda i, j: (i, j)),
                  pl.BlockSpec((tm, tn), lambda i, j: (i, j + nj)),
                  pl.BlockSpec((K, tn), lambda i, j: (0, j)),
                  pl.BlockSpec((K, tn), lambda i, j: (0, j))],
        out_specs=pl.BlockSpec((tm, tn), lambda i, j: (i, j)),
        out_shape=jax.ShapeDtypeStruct((M, D), BF16),
        compiler_params=_cparams(("parallel", "arbitrary")),
        name="merge",
    )(o_nsa, o_fox, gates, gates, wbn, wbf)


def _wo_kernel(m_ref, w_ref, x_ref, g_ref, x1_ref, h1_ref):
    x1 = x_ref[...] + _dot(m_ref[...], w_ref[...])
    x1_ref[...] = x1
    h1_ref[...] = _rms(x1, g_ref[...]).astype(h1_ref.dtype)


def out_proj(merged, wo, x, gain1):
    M, D = x.shape
    tm = _pick(M, (256, 128, 8))
    return pl.pallas_call(
        _wo_kernel,
        grid=(M // tm,),
        in_specs=[pl.BlockSpec((tm, D), lambda i: (i, 0)),
                  pl.BlockSpec((D, D), lambda i: (0, 0)),
                  pl.BlockSpec((tm, D), lambda i: (i, 0)),
                  pl.BlockSpec((1, D), lambda i: (0, 0))],
        out_specs=[pl.BlockSpec((tm, D), lambda i: (i, 0)), pl.BlockSpec((tm, D), lambda i: (i, 0))],
        out_shape=[jax.ShapeDtypeStruct((M, D), F32), jax.ShapeDtypeStruct((M, D), BF16)],
        compiler_params=_cparams(("parallel",)),
        name="out_proj",
    )(merged, wo, x, gain1.reshape(1, D))


def _ffn_in_kernel(h_ref, wg_ref, wu_ref, o_ref):
    h = h_ref[...]
    gt = _dot(h, wg_ref[...])
    up = _dot(h, wu_ref[...])
    o_ref[...] = (gt * jax.nn.sigmoid(gt) * up).astype(o_ref.dtype)


def ffn_in(h1, wfi):
    M, D = h1.shape
    F = wfi.shape[1] // 2
    tm = _pick(M, (1024, 512, 256, 128, 8))
    tn = _pick(F, (512, 256, 128))
    nj = F // tn
    return pl.pallas_call(
        _ffn_in_kernel,
        grid=(M // tm, nj),
        in_specs=[pl.BlockSpec((tm, D), lambda i, j: (i, 0)),
                  pl.BlockSpec((D, tn), lambda i, j: (0, j)),
                  pl.BlockSpec((D, tn), lambda i, j: (0, j + nj))],
        out_specs=pl.BlockSpec((tm, tn), lambda i, j: (i, j)),
        out_shape=jax.ShapeDtypeStruct((M, F), BF16),
        compiler_params=_cparams(("parallel", "arbitrary")),
        name="ffn_in",
    )(h1, wfi, wfi)


def _ffn_out_kernel(a_ref, w_ref, x_ref, o_ref):
    o_ref[...] = x_ref[...] + _dot(a_ref[...], w_ref[...])


def ffn_out(act, wfo, x1):
    M, F = act.shape
    D = wfo.shape[1]
    tm = _pick(M, (512, 256, 128, 8))
    tn = _pick(D, (512, 256, 128))
    return pl.pallas_call(
        _ffn_out_kernel,
        grid=(M // tm, D // tn),
        in_specs=[pl.BlockSpec((tm, F), lambda i, j: (i, 0)),
                  pl.BlockSpec((F, tn), lambda i, j: (0, j)),
                  pl.BlockSpec((tm, tn), lambda i, j: (i, j))],
        out_specs=pl.BlockSpec((tm, tn), lambda i, j: (i, j)),
        out_shape=jax.ShapeDtypeStruct((M, D), F32),
        compiler_params=_cparams(("parallel", "arbitrary")),
        name="ffn_out",
    )(act, wfo, x1)


def _ple_kernel(x_ref, g_ref, wpg_ref, pe_ref, wple_ref, gn_ref, o_ref, hn_ref):
    x = x_ref[...]
    h2 = _rms(x, g_ref[...]).astype(BF16)
    gate = jax.nn.sigmoid(_dot(h2, wpg_ref[...]))
    x3 = x + gate * _dot(pe_ref[...].astype(BF16), wple_ref[...])
    o_ref[...] = x3
    hn_ref[...] = _rms(x3, gn_ref[...]).astype(hn_ref.dtype)


def ple_gate(x2, gain2, wpg, pe, wple, gain_next):
    M, D = x2.shape
    Pd = pe.shape[1]
    tm = _pick(M, (256, 128, 8))
    return pl.pallas_call(
        _ple_kernel,
        grid=(M // tm,),
        in_specs=[pl.BlockSpec((tm, D), lambda i: (i, 0)),
                  pl.BlockSpec((1, D), lambda i: (0, 0)),
                  pl.BlockSpec((D, D), lambda i: (0, 0)),
                  pl.BlockSpec((tm, Pd), lambda i: (i, 0)),
                  pl.BlockSpec((Pd, D), lambda i: (0, 0)),
                  pl.BlockSpec((1, D), lambda i: (0, 0))],
        out_specs=[pl.BlockSpec((tm, D), lambda i: (i, 0)), pl.BlockSpec((tm, D), lambda i: (i, 0))],
        out_shape=[jax.ShapeDtypeStruct((M, D), F32), jax.ShapeDtypeStruct((M, D), BF16)],
        compiler_params=_cparams(("parallel",)),
        name="ple_gate",
    )(x2, gain2.reshape(1, D), wpg, pe, wple, gain_next.reshape(1, D))


def _bias_tiles_kernel(tbl_ref, o_ref):
    g = pl.program_id(0)
    ii = _iota((LANES, LANES), 0)
    jj = _iota((LANES, LANES), 1)
    for s in range(2):
        b = _bucket(s * LANES + ii - jj)
        for r in range(R):
            vals = [tbl_ref[k, g * R + r] for k in range(N_BUCKETS)]
            o_ref[0, s, r * LANES:(r + 1) * LANES, :] = _lookup(b, vals)
    for r in range(R):
        o_ref[0, 2, r * LANES:(r + 1) * LANES, :] = jnp.full((LANES, LANES), tbl_ref[N_BUCKETS - 1, g * R + r], F32)


def bias_tiles(table):
    return pl.pallas_call(
        _bias_tiles_kernel,
        grid=(G,),
        in_specs=[pl.BlockSpec(memory_space=pltpu.SMEM)],
        out_specs=pl.BlockSpec((1, 3, R * LANES, LANES), lambda g: (g, 0, 0, 0)),
        out_shape=jax.ShapeDtypeStruct((G, 3, R * LANES, LANES), F32),
        compiler_params=_cparams(("arbitrary",)),
        name="bias_tiles",
    )(table)


def _cumsum_kernel(x_ref, ct_ref, cr_ref):
    T = x_ref.shape[1]
    ii = _iota((LANES, LANES), 0)
    jj = _iota((LANES, LANES), 1)
    tri = jnp.where(jj <= ii, 1.0, 0.0).astype(BF16)
    carry = jnp.zeros((1, LANES), F32)
    for blk in range(T // LANES):
        sl = slice(blk * LANES, (blk + 1) * LANES)
        c = _dot_split_rhs(tri, x_ref[0, sl, :]) + carry
        ct_ref[0, sl, :] = c
        cr_ref[0, :, sl] = c.T[LOGF_LANE0:LOGF_LANE0 + FOX_HEADS, :]
        carry = c[LANES - 1:LANES, :]


def cumsum_logf(small):
    B, T, _ = small.shape
    return pl.pallas_call(
        _cumsum_kernel,
        grid=(B,),
        in_specs=[pl.BlockSpec((1, T, LANES), lambda b: (b, 0, 0))],
        out_specs=[pl.BlockSpec((1, T, LANES), lambda b: (b, 0, 0)),
                   pl.BlockSpec((1, FOX_HEADS, T), lambda b: (b, 0, 0))],
        out_shape=[jax.ShapeDtypeStruct((B, T, LANES), F32), jax.ShapeDtypeStruct((B, FOX_HEADS, T), F32)],
        compiler_params=_cparams(("parallel",)),
        name="cumsum_logf",
    )(small)


def _fox_prompt_kernel(q_ref, k_ref, v_ref, ct_ref, cr_ref, o_ref, *, tq):
    h = pl.program_id(1)
    qi = pl.program_id(2)
    q = q_ref[0].astype(BF16)
    lane = _iota((tq, LANES), 1)
    cq = jnp.sum(jnp.where(lane == LOGF_LANE0 + h, ct_ref[0], 0.0), axis=1, keepdims=True)
    ii = _iota((tq, tq), 0)
    jj = _iota((tq, tq), 1)

    def body(kb, carry):
        m, l, acc = carry
        k0 = pl.multiple_of(kb * tq, tq)
        kblk = k_ref[0, pl.ds(k0, tq), :].astype(BF16)
        vblk = v_ref[0, pl.ds(k0, tq), :].astype(BF16)
        ck = cr_ref[0, h, pl.ds(kb, 1), :]
        s = _dot_nt(q, kblk) * SCALE + (cq - ck)
        mask = (kb - qi) * tq + jj <= ii
        s = jnp.where(mask, s, NEG_INF)
        m_new = jnp.maximum(m, jnp.max(s, axis=1, keepdims=True))
        alpha = jnp.exp(m - m_new)
        p = jnp.where(mask, jnp.exp(s - m_new), 0.0)
        l = alpha * l + jnp.sum(p, axis=1, keepdims=True)
        acc = alpha * acc + _dot(p.astype(BF16), vblk)
        return m_new, l, acc

    init = (jnp.full((tq, 1), NEG_INF, F32), jnp.zeros((tq, 1), F32), jnp.zeros((tq, HD), F32))
    m, l, acc = lax.fori_loop(0, qi + 1, body, init)
    o_ref[0] = (acc / jnp.where(l > 0.0, l, 1.0)).astype(o_ref.dtype)


def fox_prompt(qf, foxkv, c_tok, c_row):
    B, T, _ = qf.shape
    tq = _pick(T, (256, 128))
    c_row4 = c_row.reshape(B, FOX_HEADS, T // tq, tq)
    return pl.pallas_call(
        functools.partial(_fox_prompt_kernel, tq=tq),
        grid=(B, FOX_HEADS, T // tq),
        in_specs=[pl.BlockSpec((1, tq, HD), lambda b, h, i: (b, i, h)),
                  pl.BlockSpec((1, T, HD), lambda b, h, i: (b, 0, h)),
                  pl.BlockSpec((1, T, HD), lambda b, h, i: (b, 0, FOX_HEADS + h)),
                  pl.BlockSpec((1, tq, LANES), lambda b, h, i: (b, i, 0)),
                  pl.BlockSpec((1, FOX_HEADS, T // tq, tq), lambda b, h, i: (b, 0, 0, 0))],
        out_specs=pl.BlockSpec((1, tq, HD), lambda b, h, i: (b, i, h)),
        out_shape=jax.ShapeDtypeStruct((B, T, FOX_HEADS * HD), BF16),
        compiler_params=_cparams(("parallel", "parallel", "arbitrary")),
        name="fox_prompt",
    )(qf, foxkv, foxkv, c_tok, c_row4)


def _cmp_p_kernel(pt_ref, *refs, npg, row_per_head):
    nref = npg if row_per_head else npg * 2 * G
    page_refs = refs[:nref]
    w_ref, o_ref, lhs_ref = refs[nref:]
    rph = 4 * G
    for i in range(npg):
        for c in range(2 * G):
            for t in range(CMP_STRIDE):
                if row_per_head:
                    rows = page_refs[i][pl.ds(t * rph + c, CHUNKS_PER_PAGE, stride=CMP_STRIDE * rph), :]
                else:
                    rows = page_refs[i * 2 * G + c][pl.ds(t, CHUNKS_PER_PAGE, stride=CMP_STRIDE), :]
                lhs_ref[c, i * CHUNKS_PER_PAGE:(i + 1) * CHUNKS_PER_PAGE, t * HD:(t + 1) * HD] = rows
    for kv in range(2):
        for g in range(G):
            o_ref[0, kv, g] = _dot(lhs_ref[kv * G + g].astype(BF16), w_ref[kv])


def compress_partial(pages, page_table, layer, w1cat, row_per_head):
    NB, n_pages = page_table.shape
    npg = _pick(n_pages, (16, 8, 4, 2, 1))
    rows = npg * CHUNKS_PER_PAGE
    kdim = CMP_STRIDE * HD
    in_specs = []
    for i in range(npg):
        if row_per_head:
            in_specs.append(pl.BlockSpec((None, None, 128 * 4 * G, HD),
                                         lambda b, p, pt, i=i: (pt[b, p * npg + i], layer, 0, 0)))
            continue
        for c in range(2 * G):
            in_specs.append(pl.BlockSpec((None, None, 128, HD),
                                         lambda b, p, pt, i=i, c=c: (pt[b, p * npg + i], layer, 0, c)))
    in_specs.append(pl.BlockSpec((2, kdim, 2 * HD), lambda b, p, pt: (0, 0, 0)))
    return pl.pallas_call(
        functools.partial(_cmp_p_kernel, npg=npg, row_per_head=row_per_head),
        grid_spec=pltpu.PrefetchScalarGridSpec(
            num_scalar_prefetch=1,
            grid=(NB, n_pages // npg),
            in_specs=in_specs,
            out_specs=pl.BlockSpec((1, 2, G, rows, 2 * HD), lambda b, p, pt: (b, 0, 0, p, 0)),
            scratch_shapes=[pltpu.VMEM((2 * G, rows, kdim), F32)]),
        out_shape=jax.ShapeDtypeStruct((NB, 2, G, n_pages * CHUNKS_PER_PAGE, 2 * HD), F32),
        compiler_params=_cparams(("parallel", "arbitrary")),
        name="compress_partial",
    )(page_table, *([pages] * len(in_specs[:-1])), w1cat)


def _gelu_tanh(x):
    return x * (0.5 * (1.0 + jnp.tanh(math.sqrt(2.0 / math.pi) * (x + 0.044715 * (x * x * x)))))


def _cmp_fin_kernel(p_ref, pos_ref, w1_ref, b1_ref, w2_ref, gk_ref, kc_ref, vc_ref):
    NC = p_ref.shape[3]
    row = _iota((NC, HD), 0)
    for kv in range(2):
        w1 = w1_ref[kv]
        posterm = (_dot(pos_ref[kv, 0], w1[:, :HD]) + _dot(pos_ref[kv, 1], w1[:, HD:]))[0:1, :] + b1_ref[kv]
        for g in range(G):
            P = p_ref[0, kv, g]
            pre = P[:, :HD] + pltpu.roll(P[:, HD:], NC - 1, 0) + posterm
            o = _dot(_gelu_tanh(pre).astype(BF16), w2_ref[kv])
            if kv == 0:
                o = _rms(o, gk_ref[...])
            o = jnp.where(row < NC - 1, o, 0.0)
            if kv == 0:
                kc_ref[0, g] = o
            else:
                vc_ref[0, g] = o


def compress_finish(part, pos_ab, w1cat, b1, w2, gain_kc):
    NB, _, _, NC, _ = part.shape
    kdim = CMP_STRIDE * HD
    full = lambda shape: pl.BlockSpec(shape, lambda b: (0,) * len(shape))
    return pl.pallas_call(
        _cmp_fin_kernel,
        grid=(NB,),
        in_specs=[pl.BlockSpec((1, 2, G, NC, 2 * HD), lambda b: (b, 0, 0, 0, 0)),
                  full((2, 2, SUBLANES, kdim)), full((2, kdim, 2 * HD)), full((2, 1, HD)),
                  full((2, HD, HD)), full((1, HD))],
        out_specs=[pl.BlockSpec((1, G, NC, HD), lambda b: (b, 0, 0, 0)),
                   pl.BlockSpec((1, G, NC, HD), lambda b: (b, 0, 0, 0))],
        out_shape=[jax.ShapeDtypeStruct((NB, G, NC, HD), F32), jax.ShapeDtypeStruct((NB, G, NC, HD), F32)],
        compiler_params=_cparams(("parallel",)),
        name="compress_finish",
    )(part, pos_ab, w1cat, b1, w2, gain_kc)


def _sel_overlap_matrix(nc, nsb, n_cb):
    ci = _iota((nc, nsb), 0)
    jb = _iota((nc, nsb), 1)
    ratio = SEL_BLOCK // CMP_STRIDE
    first = ratio * jb - (CMP_LEN // CMP_STRIDE) + 1
    n_ov = (SEL_BLOCK + CMP_LEN) // CMP_STRIDE - 1
    hit = (ci >= first) & (ci < first + n_ov) & (ci < n_cb)
    return jnp.where(hit, 1.0, 0.0).astype(BF16)


def _div_pow2(x, d):
    if isinstance(x, int):
        return x // d
    return lax.shift_right_arithmetic(x, jnp.full(x.shape, d.bit_length() - 1, I32))


def _block_scores(imp_s, qpos, jb, n_sb):
    back = _div_pow2(qpos, SEL_BLOCK) - jb
    forced = (jb == 0) | ((back >= 0) & (back < N_LOCAL_BLOCKS))
    score = jnp.where(back >= 0, imp_s + jnp.where(forced, FORCE_BONUS, 0.0), -1.0)
    return jnp.where(jb < n_sb, score, -2.0)


def _nsa_prompt_kernel(tbl_ref, q_ref, kc_ref, vc_ref, ks_ref, vs_ref, kw_ref, vw_ref, br_ref, bt_ref,
                       o_ref, exp_ref, *, T, n_cb, n_sb, n_sel):
    g = pl.program_id(1)
    qi = pl.program_id(2)
    tq = LANES
    qf = q_ref[0]
    qs = jnp.concatenate([qf[:, r * HD:(r + 1) * HD] for r in range(R)], axis=0).astype(BF16)
    ii = _iota((tq, LANES), 0)
    jj = _iota((tq, LANES), 1)
    qpos = qi * tq + ii

    dist_c = qpos - (jj * CMP_STRIDE + (CMP_LEN - 1))
    valid_c = dist_c >= 0
    bkt = _bucket(dist_c)
    bias_c = jnp.concatenate([_lookup(bkt, [tbl_ref[k, g * R + r] for k in range(N_BUCKETS)]) for r in range(R)],
                             axis=0).reshape(R, tq, LANES)
    s = _dot_nt(qs, kc_ref[0, 0].astype(BF16)).reshape(R, tq, LANES) * SCALE + bias_c
    s = jnp.where(valid_c[None], s, NEG_INF)
    m = jnp.max(s, axis=-1, keepdims=True)
    e = jnp.where(valid_c[None], jnp.exp(s - m), 0.0)
    l = jnp.sum(e, axis=-1, keepdims=True)
    p = e / jnp.where(l > 0.0, l, 1.0)
    o_c = _dot(p.reshape(R * tq, LANES).astype(BF16), vc_ref[0, 0].astype(BF16)).reshape(R, tq, HD)

    imp_c = p[0] + p[1] + p[2] + p[3]
    imp_s = _dot_split_lhs(imp_c, _sel_overlap_matrix(LANES, LANES, n_cb))
    score = _block_scores(imp_s, qpos, jj, n_sb)
    rank = jnp.zeros((tq, LANES), I32)
    for i in range(n_sb):
        col = score[:, i:i + 1]
        rank = rank + jnp.where(col > score, 1, jnp.where((col == score) & (jj > i), 1, 0))
    sel = jnp.where(rank < n_sel, 1.0, 0.0).astype(BF16)
    jb2 = _iota((LANES, LANES), 0)
    kk = _iota((LANES, LANES), 1)
    for kbi in range(T // LANES):
        expand = jnp.where(_div_pow2(kbi * LANES + kk, SEL_BLOCK) == jb2, 1.0, 0.0).astype(BF16)
        exp_ref[kbi] = _dot(sel, expand)

    def flash(k_ref, v_ref, lo, hi, maskfn):
        def body(kb, carry):
            m, l, acc = carry
            k0 = pl.multiple_of(kb * LANES, LANES)
            kblk = k_ref[0, pl.ds(k0, LANES), :].astype(BF16)
            vblk = v_ref[0, pl.ds(k0, LANES), :].astype(BF16)
            off = qi - kb
            bias = bt_ref[0, jnp.minimum(off, 2)].reshape(R, tq, LANES)
            s = _dot_nt(qs, kblk).reshape(R, tq, LANES) * SCALE + bias
            mask = maskfn(kb, off * LANES + ii - jj)[None]
            s = jnp.where(mask, s, NEG_INF)
            m_new = jnp.maximum(m, jnp.max(s, axis=-1, keepdims=True))
            alpha = jnp.exp(m - m_new)
            p = jnp.where(mask, jnp.exp(s - m_new), 0.0)
            l = alpha * l + jnp.sum(p, axis=-1, keepdims=True)
            pv = _dot(p.reshape(R * tq, LANES).astype(BF16), vblk).reshape(R, tq, HD)
            return m_new, l, alpha * acc + pv

        init = (jnp.full((R, tq, 1), NEG_INF, F32), jnp.zeros((R, tq, 1), F32), jnp.zeros((R, tq, HD), F32))
        m, l, acc = lax.fori_loop(lo, hi, body, init)
        return acc / jnp.where(l > 0.0, l, 1.0)

    o_s = flash(ks_ref, vs_ref, 0, qi + 1, lambda kb, d: (exp_ref[kb] > 0.5) & (d >= 0))
    o_w = flash(kw_ref, vw_ref, jnp.maximum(qi - WINDOW // LANES, 0), qi + 1,
                lambda kb, d: (d >= 0) & (d < WINDOW))

    brt = br_ref[0]

    def col(idx):
        return jnp.sum(jnp.where(jj == idx, brt, 0.0), axis=1, keepdims=True)

    for r in range(R):
        c0 = (g * R + r) * 3
        o = col(c0) * o_c[r] + col(c0 + 1) * o_s[r] + col(c0 + 2) * o_w[r]
        o_ref[0, :, r * HD:(r + 1) * HD] = o.astype(o_ref.dtype)


def nsa_prompt(table, qa, kc, vc, nsakv, win, small, btiles):
    B, T, _ = qa.shape
    assert T % LANES == 0 and T // CMP_STRIDE == LANES and kc.shape[2] == LANES
    n_cb = T // CMP_STRIDE - CMP_LEN // CMP_STRIDE + 1
    n_sb = -(-T // SEL_BLOCK)
    n_sel = min(SEL_TOPN, n_sb)
    tq = LANES
    return pl.pallas_call(
        functools.partial(_nsa_prompt_kernel, T=T, n_cb=n_cb, n_sb=n_sb, n_sel=n_sel),
        grid=(B, G, T // tq),
        in_specs=[pl.BlockSpec(memory_space=pltpu.SMEM),
                  pl.BlockSpec((1, tq, R * HD), lambda b, g, i: (b, i, g)),
                  pl.BlockSpec((1, 1, LANES, HD), lambda b, g, i: (b, g, 0, 0)),
                  pl.BlockSpec((1, 1, LANES, HD), lambda b, g, i: (b, g, 0, 0)),
                  pl.BlockSpec((1, T, HD), lambda b, g, i: (b, 0, 2 * G + g)),
                  pl.BlockSpec((1, T, HD), lambda b, g, i: (b, 0, 3 * G + g)),
                  pl.BlockSpec((1, T, HD), lambda b, g, i: (b, 0, g)),
                  pl.BlockSpec((1, T, HD), lambda b, g, i: (b, 0, G + g)),
                  pl.BlockSpec((1, tq, LANES), lambda b, g, i: (b, i, 0)),
                  pl.BlockSpec((1, 3, R * LANES, LANES), lambda b, g, i: (g, 0, 0, 0))],
        out_specs=pl.BlockSpec((1, tq, R * HD), lambda b, g, i: (b, i, g)),
        out_shape=jax.ShapeDtypeStruct((B, T, NSA_HEADS * HD), BF16),
        scratch_shapes=[pltpu.VMEM((T // LANES, tq, LANES), F32)],
        compiler_params=_cparams(("parallel", "parallel", "arbitrary")),
        name="nsa_prompt",
    )(table, qa, kc, vc, nsakv, nsakv, win, win, small, btiles)


def _dec_cmp_kernel(q_ref, kc_ref, vc_ref, tblt_ref, oc_ref, idx_ref, *, qpos, n_cb, n_sb, n_sel, nsbp):
    NC = kc_ref.shape[2]
    q8 = q_ref[0].astype(BF16)
    row = _iota((NSA_HEADS, NC), 0)
    j = _iota((NSA_HEADS, NC), 1)
    s = jnp.where(row < R, _dot_nt(q8, kc_ref[0, 0].astype(BF16)), _dot_nt(q8, kc_ref[0, 1].astype(BF16)))
    dist = qpos - (j * CMP_STRIDE + (CMP_LEN - 1))
    valid = dist >= 0
    tblt = tblt_ref[...]
    bias = _lookup(_bucket(dist), [tblt[:, k:k + 1] for k in range(N_BUCKETS)])
    s = jnp.where(valid, s * SCALE + bias, NEG_INF)
    m = jnp.max(s, axis=1, keepdims=True)
    e = jnp.where(valid, jnp.exp(s - m), 0.0)
    l = jnp.sum(e, axis=1, keepdims=True)
    p = e / jnp.where(l > 0.0, l, 1.0)
    pb = p.astype(BF16)
    row_o = _iota((NSA_HEADS, HD), 0)
    oc_ref[0] = jnp.where(row_o < R, _dot(pb, vc_ref[0, 0].astype(BF16)), _dot(pb, vc_ref[0, 1].astype(BF16)))

    imp0 = jnp.sum(jnp.where(row < R, p, 0.0), axis=0, keepdims=True)
    imp1 = jnp.sum(jnp.where(row >= R, p, 0.0), axis=0, keepdims=True)
    imp = jnp.where(row == 0, imp0, jnp.where(row == 1, imp1, 0.0))
    imp_s = _dot_split_lhs(imp, _sel_overlap_matrix(NC, nsbp, n_cb))
    jb = _iota((NSA_HEADS, nsbp), 1)
    score = _block_scores(imp_s, qpos, jb, n_sb)
    jbf = jb.astype(F32)
    lane_o = _iota((NSA_HEADS, LANES), 1)
    out = jnp.zeros((NSA_HEADS, LANES), I32)
    for n in range(n_sel):
        mx = jnp.max(score, axis=1, keepdims=True)
        am = jnp.min(jnp.where(score == mx, jbf, float(nsbp)), axis=1, keepdims=True)
        out = jnp.where(lane_o == n, am.astype(I32), out)
        score = jnp.where(jbf == am, -3.0, score)
    idx_ref[0] = out


def dec_cmp(q8, kc, vc, tblt, qpos):
    DB = q8.shape[0]
    NC = kc.shape[2]
    L = qpos + 1
    n_cb = L // CMP_STRIDE - CMP_LEN // CMP_STRIDE + 1
    assert n_cb == NC - 1
    n_sb = -(-L // SEL_BLOCK)
    n_sel = min(SEL_TOPN, n_sb)
    nsbp = -(-n_sb // LANES) * LANES
    oc, idx = pl.pallas_call(
        functools.partial(_dec_cmp_kernel, qpos=qpos, n_cb=n_cb, n_sb=n_sb, n_sel=n_sel, nsbp=nsbp),
        grid=(DB,),
        in_specs=[pl.BlockSpec((1, NSA_HEADS, HD), lambda b: (b, 0, 0)),
                  pl.BlockSpec((1, G, NC, HD), lambda b: (b, 0, 0, 0)),
                  pl.BlockSpec((1, G, NC, HD), lambda b: (b, 0, 0, 0)),
                  pl.BlockSpec((NSA_HEADS, N_BUCKETS), lambda b: (0, 0))],
        out_specs=[pl.BlockSpec((1, NSA_HEADS, HD), lambda b: (b, 0, 0)),
                   pl.BlockSpec((1, NSA_HEADS, LANES), lambda b: (b, 0, 0))],
        out_shape=[jax.ShapeDtypeStruct((DB, NSA_HEADS, HD), F32), jax.ShapeDtypeStruct((DB, NSA_HEADS, LANES), I32)],
        compiler_params=_cparams(("parallel",)),
        name="dec_cmp",
    )(q8, kc, vc, tblt)
    return oc, idx[:, :G, :n_sel].reshape(DB, G * n_sel), n_sel


def _softmax_with_new(s, valid, s_new, new_ok):
    s = jnp.where(valid, s, NEG_INF)
    s_new = jnp.where(new_ok, s_new, NEG_INF)
    m = jnp.maximum(jnp.max(s, axis=1, keepdims=True), s_new)
    e = jnp.where(valid, jnp.exp(s - m), 0.0)
    en = jnp.where(new_ok, jnp.exp(s_new - m), 0.0)
    l = jnp.sum(e, axis=1, keepdims=True) + en
    inv = 1.0 / jnp.where(l > 0.0, l, 1.0)
    return e, en, inv


def _dec_selwin_kernel(pt_ref, idx_ref, *refs, n_sel, qpos, jb_new, w_buf):
    blk_refs = refs[:G * n_sel]
    q_ref, new_ref, win_ref, wnew_ref, br_ref, oc_ref, tblt_ref, o_ref = refs[G * n_sel:]
    rph = 4 * G
    b = pl.program_id(0)
    q8 = q_ref[0]
    row8 = _iota((NSA_HEADS, HD), 0)
    tblt = tblt_ref[...]
    tcols = [tblt[:, k:k + 1] for k in range(N_BUCKETS)]
    new8 = new_ref[0]
    wnew = wnew_ref[0]
    brrow = br_ref[0]
    oc = oc_ref[0]
    nk = n_sel * SEL_BLOCK
    lane = _iota((NSA_HEADS, nk), 1)
    outs = []
    for g in range(G):
        qg = jnp.where(_div_pow2(row8, R) == g, q8, 0.0)
        qb = qg.astype(BF16)
        qr = qb.astype(F32)
        kmat = jnp.concatenate([blk_refs[g * n_sel + n][pl.ds(2 * G + g, SEL_BLOCK, stride=rph), :]
                                for n in range(n_sel)], axis=0).astype(BF16)
        vmat = jnp.concatenate([blk_refs[g * n_sel + n][pl.ds(3 * G + g, SEL_BLOCK, stride=rph), :]
                                for n in range(n_sel)], axis=0).astype(BF16)
        idxv = jnp.zeros((NSA_HEADS, nk), I32)
        has_new = jnp.zeros((), jnp.bool_)
        for n in range(n_sel):
            sidx = idx_ref[b, g * n_sel + n]
            idxv = jnp.where(_div_pow2(lane, SEL_BLOCK) == n, sidx, idxv)
            has_new = has_new | (sidx == jb_new)
        dist = qpos - (idxv * SEL_BLOCK + (lane & (SEL_BLOCK - 1)))
        valid = (idxv < jb_new) & (dist >= 0)
        s = _dot_nt(qb, kmat) * SCALE + _lookup(_bucket(dist), tcols)
        kn = new8[2 * G + g:2 * G + g + 1, :].astype(BF16).astype(F32)
        vn = new8[3 * G + g:3 * G + g + 1, :].astype(BF16).astype(F32)
        s_new = jnp.sum(qr * kn, axis=1, keepdims=True) * SCALE + tcols[0]
        e, en, inv = _softmax_with_new(s, valid, s_new, has_new)
        o_s = (_dot(e.astype(BF16), vmat) + en * vn) * inv

        kwin = win_ref[pl.ds(g, w_buf, stride=2 * G), :].astype(BF16)
        vwin = win_ref[pl.ds(G + g, w_buf, stride=2 * G), :].astype(BF16)
        lw = _iota((NSA_HEADS, w_buf), 1)
        dist_w = w_buf - lw
        valid_w = (dist_w < WINDOW) & (qpos - dist_w >= 0)
        sw = _dot_nt(qb, kwin) * SCALE + _lookup(_bucket(dist_w), tcols)
        kwn = wnew[g:g + 1, :].astype(BF16).astype(F32)
        vwn = wnew[G + g:G + g + 1, :].astype(BF16).astype(F32)
        sw_new = jnp.sum(qr * kwn, axis=1, keepdims=True) * SCALE + tcols[0]
        ew, ewn, invw = _softmax_with_new(sw, valid_w, sw_new, jnp.ones((), jnp.bool_))
        o_w = (_dot(ew.astype(BF16), vwin) + ewn * vwn) * invw

        def bcol(i):
            return jnp.sum(jnp.where(_iota((NSA_HEADS, LANES), 1) == row8 * 3 + i, brrow, 0.0), axis=1, keepdims=True)

        outs.append(bcol(0) * oc + bcol(1) * o_s + bcol(2) * o_w)
    o_ref[0] = jnp.where(row8 < R, outs[0], outs[1])


def dec_selwin(q8, new8, wnew4, small_s, oc, tblt, cache_nsa_kv, state_win, page_table, idx, n_sel, layer, qpos):
    DB = q8.shape[0]
    n_pool, depth = cache_nsa_kv.shape[:2]
    page = cache_nsa_kv.shape[2]
    halves = page // SEL_BLOCK
    w_buf = state_win.shape[2]
    n_pages = page_table.shape[1]
    jb_new = qpos // SEL_BLOCK
    rph = 4 * G
    cache_rows = cache_nsa_kv.reshape(n_pool, depth, page * rph, HD)
    state_rows = state_win.reshape(depth, DB, w_buf * 2 * G, HD)
    in_specs = []
    args = []
    for g in range(G):
        for n in range(n_sel):
            def imap(b, pt, ix, g=g, n=n):
                jb = jnp.clip(ix[b, g * n_sel + n], 0, jb_new - 1)
                return (pt[b, jb // halves], layer, jb % halves, 0)
            in_specs.append(pl.BlockSpec((None, None, SEL_BLOCK * rph, HD), imap))
            args.append(cache_rows)
    row3 = lambda b, pt, ix: (b, 0, 0)
    in_specs += [pl.BlockSpec((1, NSA_HEADS, HD), row3), pl.BlockSpec((1, 4 * G, HD), row3),
                 pl.BlockSpec((None, None, w_buf * 2 * G, HD), lambda b, pt, ix: (layer, b, 0, 0)),
                 pl.BlockSpec((1, 2 * G, HD), row3), pl.BlockSpec((1, 1, LANES), row3),
                 pl.BlockSpec((1, NSA_HEADS, HD), row3),
                 pl.BlockSpec((NSA_HEADS, N_BUCKETS), lambda b, pt, ix: (0, 0))]
    args += [q8, new8, state_rows, wnew4, small_s.reshape(DB, 1, LANES), oc, tblt]
    return pl.pallas_call(
        functools.partial(_dec_selwin_kernel, n_sel=n_sel, qpos=qpos, jb_new=jb_new, w_buf=w_buf),
        grid_spec=pltpu.PrefetchScalarGridSpec(
            num_scalar_prefetch=2,
            grid=(DB,),
            in_specs=in_specs,
            out_specs=pl.BlockSpec((1, NSA_HEADS, HD), row3)),
        out_shape=jax.ShapeDtypeStruct((DB, NSA_HEADS, HD), F32),
        compiler_params=_cparams(("arbitrary",)),
        name="dec_selwin",
    )(page_table, idx, *args)


def _dec_fox_kernel(pt_ref, *refs, npg):
    kv_refs = refs[:npg]
    lf_refs = refs[npg:2 * npg]
    q_ref, new_ref, lfn_ref, o_ref, m_sc, l_sc, acc_sc, car_sc = refs[2 * npg:]
    p = pl.program_id(1)
    nh = FOX_HEADS
    page = kv_refs[0].shape[0]
    q8 = q_ref[0]
    qs = q8 * SCALE

    @pl.when(p == 0)
    def _():
        new = new_ref[0]
        s_new = jnp.sum(qs * new[0:nh], axis=1, keepdims=True)
        m_sc[...] = jnp.broadcast_to(s_new, (nh, LANES))
        l_sc[...] = jnp.ones((nh, LANES), F32)
        acc_sc[...] = new[nh:2 * nh]
        lane1 = _iota((nh, LANES), 1)
        row1 = _iota((nh, LANES), 0)
        car_sc[...] = jnp.sum(jnp.where(lane1 == LOGF_LANE0 + row1, lfn_ref[0], 0.0), axis=1, keepdims=True)

    def tree(x, op):
        while x.shape[0] > 1:
            half = x.shape[0] // 2
            x = op(x[:half], x[half:])
        return x[0]

    uu = _iota((LANES, LANES), 0)
    ss = _iota((LANES, LANES), 1)
    later = jnp.where(uu > ss, 1.0, 0.0).astype(BF16)
    ones = jnp.ones((HD, LANES), BF16)
    tok3 = _iota((page, nh, LANES), 0)
    lane3 = _iota((page, nh, LANES), 2)
    m = m_sc[...]
    l = l_sc[...]
    acc = acc_sc[...]
    car = car_sc[...]
    for i in range(npg):
        lft = lf_refs[i][...]
        decay = _dot_split_lhs(lft, later) + car
        lhs = jnp.where(lane3 == tok3, decay[None], 0.0) + kv_refs[i][:, 0] * qs[None]
        lhs = lhs.reshape(page * nh, HD)
        hi = lhs.astype(BF16)
        lo = (lhs - hi.astype(F32)).astype(BF16)
        s = (_dot(hi, ones) + _dot(lo, ones)).reshape(page, nh, LANES)
        m_new = jnp.maximum(m, tree(s, jnp.maximum))
        alpha = jnp.exp(m - m_new)
        pp = jnp.exp(s - m_new[None])
        l = alpha * l + tree(pp, jnp.add)
        acc = alpha * acc + tree(pp * kv_refs[i][:, 1], jnp.add)
        m = m_new
        car = car + jnp.sum(lft, axis=1, keepdims=True)
    m_sc[...] = m
    l_sc[...] = l
    acc_sc[...] = acc
    car_sc[...] = car

    @pl.when(p == pl.num_programs(1) - 1)
    def _():
        o_ref[0] = acc / l


def dec_fox(q8, new16, small_s, cache_fox_kv, logf_t, page_table, layer):
    DB = q8.shape[0]
    n_pool, depth, page = cache_fox_kv.shape[:3]
    assert page == LANES
    n_pages = page_table.shape[1]
    npg = _pick(n_pages, (4, 2, 1))
    in_specs = []
    for i in range(npg):
        in_specs.append(pl.BlockSpec((None, None, page, 2, FOX_HEADS, HD),
                                     lambda b, p, pt, i=i: (pt[b, n_pages - 1 - (p * npg + i)], layer, 0, 0, 0, 0)))
    for i in range(npg):
        in_specs.append(pl.BlockSpec((None, None, FOX_HEADS, page),
                                     lambda b, p, pt, i=i: (pt[b, n_pages - 1 - (p * npg + i)], layer, 0, 0)))
    row3 = lambda b, p, pt: (b, 0, 0)
    in_specs += [pl.BlockSpec((1, FOX_HEADS, HD), row3), pl.BlockSpec((1, 2 * FOX_HEADS, HD), row3),
                 pl.BlockSpec((1, 1, LANES), row3)]
    return pl.pallas_call(
        functools.partial(_dec_fox_kernel, npg=npg),
        grid_spec=pltpu.PrefetchScalarGridSpec(
            num_scalar_prefetch=1,
            grid=(DB, n_pages // npg),
            in_specs=in_specs,
            out_specs=pl.BlockSpec((1, FOX_HEADS, HD), row3),
            scratch_shapes=[pltpu.VMEM((FOX_HEADS, LANES), F32), pltpu.VMEM((FOX_HEADS, LANES), F32),
                            pltpu.VMEM((FOX_HEADS, HD), F32), pltpu.VMEM((FOX_HEADS, 1), F32)]),
        out_shape=jax.ShapeDtypeStruct((DB, FOX_HEADS, HD), F32),
        compiler_params=_cparams(("parallel", "arbitrary")),
        name="dec_fox",
    )(page_table, *([cache_fox_kv] * npg), *([logf_t] * npg), q8, new16, small_s.reshape(DB, 1, LANES))


def _layer_weights(l, D, w_in, b_forget, qk_gain_nsa, qk_gain_fox):
    nq = NSA_HEADS * HD
    nkv = 6 * G * HD
    nfox = 3 * FOX_HEADS * HD
    o_br = nq + nkv
    o_fox = o_br + BR_LANES
    o_f = o_fox + nfox
    o_mg = o_f + FOX_HEADS
    w = w_in[l]
    ones = jnp.ones((HD,), F32)
    gn = qk_gain_nsa[l]
    gf = qk_gain_fox[l]

    def seg(lo, hi):
        return w[:, lo:hi].astype(BF16)

    def gains(rows):
        return jnp.concatenate(rows).reshape(1, -1)

    zeros = lambda n: jnp.zeros((1, n), F32)
    small_w = jnp.concatenate([w[:, o_br:o_br + BR_LANES], w[:, o_f:o_f + FOX_HEADS],
                               jnp.zeros((D, LANES - BR_LANES - FOX_HEADS), F32)], axis=1).astype(BF16)
    small_b = jnp.concatenate([jnp.zeros((BR_LANES,), F32), b_forget[l],
                               jnp.zeros((LANES - BR_LANES - FOX_HEADS,), F32)]).reshape(1, LANES)
    kvh = G
    return {
        "qa": (seg(0, nq), gains([gn[0]] * NSA_HEADS), zeros(nq), ("norm",) * NSA_HEADS, nq),
        "nsakv": (seg(nq, nq + 4 * kvh * HD), gains([ones] * (2 * kvh) + [gn[2]] * kvh + [ones] * kvh),
                  zeros(4 * kvh * HD), ("raw",) * (2 * kvh) + ("norm",) * kvh + ("raw",) * kvh, 4 * kvh * HD),
        "win": (seg(nq + 4 * kvh * HD, o_br), gains([gn[3]] * kvh + [ones] * kvh), zeros(2 * kvh * HD),
                ("norm",) * kvh + ("raw",) * kvh, 2 * kvh * HD),
        "qf": (seg(o_fox, o_fox + FOX_HEADS * HD), gains([gf[0]] * FOX_HEADS), zeros(FOX_HEADS * HD),
               ("norm",) * FOX_HEADS, FOX_HEADS * HD),
        "foxkv": (seg(o_fox + FOX_HEADS * HD, o_f), gains([gf[1]] * FOX_HEADS + [ones] * FOX_HEADS),
                  zeros(2 * FOX_HEADS * HD), ("norm",) * FOX_HEADS + ("raw",) * FOX_HEADS, 2 * FOX_HEADS * HD),
        "gates": (seg(o_mg, o_mg + 2 * D), zeros(2 * D), zeros(2 * D),
                  ("sigmoid",) * (_pick(2 * D, (1024, 512, 256, 128)) // LANES), _pick(2 * D, (1024, 512, 256, 128))),
        "small": (small_w, zeros(LANES), small_b, ("small",), LANES),
    }


def _project_all(h, segs):
    return {name: proj(h, w, gn, bs, modes, tn) for name, (w, gn, bs, modes, tn) in segs.items()}


def _tail(x, o_nsa, o_fox, gates, pe, gains_l, gain_next, wbn, wbf, wo, wfi, wfo, wple, wpg):
    merged = merge(o_nsa, o_fox, gates, wbn, wbf)
    x1, h1 = out_proj(merged, wo, x, gains_l[1])
    act = ffn_in(h1, wfi)
    x2 = ffn_out(act, wfo, x1)
    return ple_gate(x2, gains_l[2], wpg, pe, wple, gain_next)


def kernel(x_prompt, x_sample, cache_nsa_kv, cache_fox_kv, cache_fox_logf, state_nsa_window, page_table, p_prompt, p_sample, rel_bias_table, norm_gains, w_in, b_forget, qk_gain_nsa, qk_gain_fox, cmp_pos, w_cmp1, b_cmp1, w_cmp2, w_branch_nsa, w_branch_fox, w_out, w_ffn_in, w_ffn_out, w_ple, w_ple_gate):
    B, T, D = x_prompt.shape
    DB, n_qs, _ = x_sample.shape
    assert n_qs == 1
    depth = w_in.shape[0]
    page = cache_nsa_kv.shape[2]
    n_pages = page_table.shape[1]
    past = n_pages * page
    w_buf = state_nsa_window.shape[2]
    kdim = CMP_STRIDE * HD

    xp = x_prompt.reshape(B * T, D)
    xs = x_sample.reshape(DB, D)
    btiles = bias_tiles(rel_bias_table)
    tblt = rel_bias_table.T
    logf_t = jnp.swapaxes(cache_fox_logf, 2, 3)
    pt_prompt = jnp.arange(B * (T // page), dtype=I32).reshape(B, T // page)
    hp = rms_bf16(xp, norm_gains[0, 0])
    hs = rms_bf16(xs, norm_gains[0, 0])

    outs = [[] for _ in range(8)]
    for l in range(depth):
        segs = _layer_weights(l, D, w_in, b_forget, qk_gain_nsa, qk_gain_fox)
        w1 = w_cmp1[l]
        w1cat = jnp.concatenate([w1[:, :kdim], w1[:, kdim:]], axis=2).astype(BF16)
        pos = cmp_pos[l].reshape(2, 2, 1, kdim)
        pos_ab = jnp.broadcast_to(pos, (2, 2, SUBLANES, kdim)).astype(BF16)
        cmp_args = (pos_ab, w1cat, b_cmp1[l].reshape(2, 1, HD), w_cmp2[l].astype(BF16),
                    qk_gain_nsa[l, 1].reshape(1, HD))
        gain_next = norm_gains[l + 1, 0] if l + 1 < depth else norm_gains[l, 0]
        tail_w = (norm_gains[l], gain_next, w_branch_nsa[l].astype(BF16), w_branch_fox[l].astype(BF16),
                  w_out[l].astype(BF16), w_ffn_in[l].astype(BF16), w_ffn_out[l].astype(BF16),
                  w_ple[l].astype(BF16), w_ple_gate[l].astype(BF16))

        P = _project_all(hp, segs)
        small3 = P["small"].reshape(B, T, LANES)
        c_tok, c_row = cumsum_logf(small3)
        o_fox_p = fox_prompt(P["qf"].reshape(B, T, -1), P["foxkv"].reshape(B, T, -1), c_tok, c_row)
        part = compress_partial(P["nsakv"].reshape(B * (T // page), 1, page, 4 * G * HD), pt_prompt, 0, w1cat, False)
        kc_p, vc_p = compress_finish(part, *cmp_args)
        o_nsa_p = nsa_prompt(rel_bias_table, P["qa"].reshape(B, T, -1), kc_p, vc_p,
                             P["nsakv"].reshape(B, T, -1), P["win"].reshape(B, T, -1), small3, btiles)
        xp, hp = _tail(xp, o_nsa_p.reshape(B * T, -1), o_fox_p.reshape(B * T, -1), P["gates"],
                       p_prompt[l].reshape(B * T, -1), *tail_w)

        S = _project_all(hs, segs)
        part_s = compress_partial(cache_nsa_kv.reshape(cache_nsa_kv.shape[0], depth, page * 4 * G, HD),
                                  page_table, l, w1cat, True)
        kc_s, vc_s = compress_finish(part_s, *cmp_args)
        q8 = S["qa"].reshape(DB, NSA_HEADS, HD)
        oc_s, idx, n_sel = dec_cmp(q8, kc_s, vc_s, tblt, past)
        o_nsa_s = dec_selwin(q8, S["nsakv"].reshape(DB, 4 * G, HD), S["win"].reshape(DB, 2 * G, HD), S["small"],
                             oc_s, tblt, cache_nsa_kv, state_nsa_window, page_table, idx, n_sel, l, past)
        o_fox_s = dec_fox(S["qf"].reshape(DB, FOX_HEADS, HD), S["foxkv"].reshape(DB, 2 * FOX_HEADS, HD),
                          S["small"], cache_fox_kv, logf_t, page_table, l)
        xs, hs = _tail(xs, o_nsa_s.reshape(DB, -1).astype(BF16), o_fox_s.reshape(DB, -1).astype(BF16), S["gates"],
                       p_sample[l].reshape(DB, -1), *tail_w)

        lf0, lf1 = LOGF_LANE0, LOGF_LANE0 + FOX_HEADS
        outs[0].append(P["nsakv"].reshape(B, T, 4, G, HD))
        outs[1].append(P["win"].reshape(B, T, 2, G, HD)[:, T - min(WINDOW, T):])
        outs[2].append(P["foxkv"].reshape(B, T, 2, FOX_HEADS, HD))
        outs[3].append(P["small"][:, lf0:lf1].reshape(B, T, FOX_HEADS))
        outs[4].append(S["nsakv"].reshape(DB, 1, 4, G, HD))
        win_all = jnp.concatenate([state_nsa_window[l], S["win"].reshape(DB, 1, 2, G, HD)], axis=1)
        outs[5].append(win_all[:, w_buf + 1 - min(WINDOW, w_buf + 1):])
        outs[6].append(S["foxkv"].reshape(DB, 1, 2, FOX_HEADS, HD))
        outs[7].append(S["small"][:, lf0:lf1].reshape(DB, 1, FOX_HEADS))

    return (xp.reshape(B, T, D), xs.reshape(DB, 1, D)) + tuple(jnp.stack(o) for o in outs)
```

```python
import functools
import math

import jax
import jax.numpy as jnp
from jax import lax
from jax.experimental import pallas as pl
from jax.experimental.pallas import tpu as pltpu

F32 = jnp.float32
BF16 = jnp.bfloat16
I32 = jnp.int32

HEAD_DIM = 128
NSA_HEADS = 8
NSA_KV_HEADS = 2
NSA_GROUP = NSA_HEADS // NSA_KV_HEADS
FOX_HEADS = 8
CMP_LEN = 32
CMP_STRIDE = 16
SEL_BLOCK = 64
SEL_TOPN = 16
N_LOCAL_BLOCKS = 2
WINDOW = 512
N_BUCKETS = 32
BUCKET_EXACT = N_BUCKETS // 2
MAX_DISTANCE = 128
EPS = 1e-6
SCALE = HEAD_DIM ** -0.5
NEG_INF = -1e30
FORCE_BONUS = 1e4

LANES = 128
SUBLANES = 8
VMEM_LIMIT_MB = 52

G = NSA_KV_HEADS
R = NSA_GROUP
HD = HEAD_DIM
BR_LANES = 3 * NSA_HEADS
LOGF_LANE0 = BR_LANES
CHUNKS_PER_PAGE = 128 // CMP_STRIDE


def _cparams(sem, vmem_mb=VMEM_LIMIT_MB):
    return pltpu.CompilerParams(dimension_semantics=sem, vmem_limit_bytes=vmem_mb * 1024 * 1024)


def _pick(n, prefs):
    for p in prefs:
        if n % p == 0:
            return p
    return n


def _iota(shape, dim):
    return lax.broadcasted_iota(I32, shape, dim)


def _dot(a, b):
    return jnp.dot(a, b, preferred_element_type=F32)


def _dot_nt(a, b):
    return lax.dot_general(a, b, (((1,), (1,)), ((), ())), preferred_element_type=F32)


def _split3(x):
    hi = x.astype(BF16)
    r1 = x - hi.astype(F32)
    mid = r1.astype(BF16)
    lo = (r1 - mid.astype(F32)).astype(BF16)
    return hi, mid, lo


def _dot_split_lhs(x, m01):
    hi, mid, lo = _split3(x)
    return _dot(hi, m01) + _dot(mid, m01) + _dot(lo, m01)


def _dot_split_rhs(m01, x):
    hi, mid, lo = _split3(x)
    return _dot(m01, hi) + _dot(m01, mid) + _dot(m01, lo)


def _rms(x, gain):
    return x * lax.rsqrt(jnp.mean(x * x, axis=-1, keepdims=True) + EPS) * gain


def _bucket(dist):
    n = jnp.maximum(dist, 0)
    rel = jnp.log(jnp.maximum(n, 1).astype(F32) / float(BUCKET_EXACT)) / math.log(MAX_DISTANCE / BUCKET_EXACT)
    large = jnp.minimum(BUCKET_EXACT + (rel * float(N_BUCKETS - BUCKET_EXACT)).astype(I32), N_BUCKETS - 1)
    return jnp.where(n < BUCKET_EXACT, n, large)


def _lookup(bucket, vals):
    out = vals[N_BUCKETS - 1]
    for k in range(N_BUCKETS - 1):
        out = jnp.where(bucket == k, vals[k], out)
    return out


def _rms_kernel(x_ref, g_ref, o_ref):
    o_ref[...] = _rms(x_ref[...], g_ref[...]).astype(o_ref.dtype)


def rms_bf16(x, gain):
    M, D = x.shape
    tm = _pick(M, (512, 256, 128, 8))
    return pl.pallas_call(
        _rms_kernel,
        grid=(M // tm,),
        in_specs=[pl.BlockSpec((tm, D), lambda i: (i, 0)), pl.BlockSpec((1, D), lambda i: (0, 0))],
        out_specs=pl.BlockSpec((tm, D), lambda i: (i, 0)),
        out_shape=jax.ShapeDtypeStruct((M, D), BF16),
        compiler_params=_cparams(("parallel",)),
        name="rms",
    )(x, gain.reshape(1, D))


def _proj_kernel(h_ref, w_ref, g_ref, b_ref, o_ref, *, modes):
    h = h_ref[...]
    nt = len(modes)
    t = 0
    while t < nt:
        wd = 2 if t + 1 < nt else 1
        acc = _dot(h, w_ref[:, t * LANES:(t + wd) * LANES])
        for u in range(wd):
            c0 = (t + u) * LANES
            v = acc[:, u * LANES:(u + 1) * LANES]
            mode = modes[t + u]
            if mode == "norm":
                v = _rms(v, g_ref[:, c0:c0 + LANES])
            elif mode == "sigmoid":
                v = jax.nn.sigmoid(v)
            elif mode == "small":
                lane = _iota(v.shape, 1)
                z = v + b_ref[:, c0:c0 + LANES]
                logsig = jnp.minimum(z, 0.0) - jnp.log(1.0 + jnp.exp(-jnp.abs(z)))
                v = jnp.where(lane < BR_LANES, jax.nn.sigmoid(v),
                              jnp.where(lane < LOGF_LANE0 + FOX_HEADS, logsig, 0.0))
            o_ref[:, c0:c0 + LANES] = v
        t += wd


def proj(h, w, gains, bias, modes, tn):
    M, K = h.shape
    N = w.shape[1]
    tm = _pick(M, (512, 256, 128, 8))
    assert len(modes) * LANES == tn and N % tn == 0
    return pl.pallas_call(
        functools.partial(_proj_kernel, modes=modes),
        grid=(M // tm, N // tn),
        in_specs=[pl.BlockSpec((tm, K), lambda i, j: (i, 0)),
                  pl.BlockSpec((K, tn), lambda i, j: (0, j)),
                  pl.BlockSpec((1, tn), lambda i, j: (0, j)),
                  pl.BlockSpec((1, tn), lambda i, j: (0, j))],
        out_specs=pl.BlockSpec((tm, tn), lambda i, j: (i, j)),
        out_shape=jax.ShapeDtypeStruct((M, N), F32),
        compiler_params=_cparams(("parallel", "arbitrary")),
        name="proj",
    )(h, w, gains, bias)


def _merge_kernel(on_ref, of_ref, ga_ref, gb_ref, wn_ref, wf_ref, o_ref):
    a = _dot(on_ref[...], wn_ref[...])
    b = _dot(of_ref[...], wf_ref[...])
    o_ref[...] = (ga_ref[...] * a + gb_ref[...] * b).astype(o_ref.dtype)


def merge(o_nsa, o_fox, gates, wbn, wbf):
    M, K = o_nsa.shape
    D = wbn.shape[1]
    tm = _pick(M, (512, 256, 128, 8))
    tn = _pick(D, (1024, 512, 256, 128))
    nj = D // tn
    return pl.pallas_call(
        _merge_kernel,
        grid=(M // tm, nj),
        in_specs=[pl.BlockSpec((tm, K), lambda i, j: (i, 0)),
                  pl.BlockSpec((tm, K), lambda i, j: (i, 0)),
                  pl.BlockSpec((tm, tn), lambda i, j: (i, j)),
                  pl.BlockSpec((tm, tn), lambda i, j: (i, j + nj)),
                  pl.BlockSpec((K, tn), lambda i, j: (0, j)),
                  pl.BlockSpec((K, tn), lambda i, j: (0, j))],
        out_specs=pl.BlockSpec((tm, tn), lambda i, j: (i, j)),
        out_shape=jax.ShapeDtypeStruct((M, D), BF16),
        compiler_params=_cparams(("parallel", "arbitrary")),
        name="merge",
    )(o_nsa, o_fox, gates, gates, wbn, wbf)


def _wo_kernel(m_ref, w_ref, x_ref, g_ref, x1_ref, h1_ref):
    x1 = x_ref[...] + _dot(m_ref[...], w_ref[...])
    x1_ref[...] = x1
    h1_ref[...] = _rms(x1, g_ref[...]).astype(h1_ref.dtype)


def out_proj(merged, wo, x, gain1):
    M, D = x.shape
    tm = _pick(M, (256, 128, 8))
    return pl.pallas_call(
        _wo_kernel,
        grid=(M // tm,),
        in_specs=[pl.BlockSpec((tm, D), lambda i: (i, 0)),
                  pl.BlockSpec((D, D), lambda i: (0, 0)),
                  pl.BlockSpec((tm, D), lambda i: (i, 0)),
                  pl.BlockSpec((1, D), lambda i: (0, 0))],
        out_specs=[pl.BlockSpec((tm, D), lambda i: (i, 0)), pl.BlockSpec((tm, D), lambda i: (i, 0))],
        out_shape=[jax.ShapeDtypeStruct((M, D), F32), jax.ShapeDtypeStruct((M, D), BF16)],
        compiler_params=_cparams(("parallel",)),
        name="out_proj",
    )(merged, wo, x, gain1.reshape(1, D))


def _ffn_in_kernel(h_ref, wg_ref, wu_ref, o_ref):
    h = h_ref[...]
    gt = _dot(h, wg_ref[...])
    up = _dot(h, wu_ref[...])
    o_ref[...] = (gt * jax.nn.sigmoid(gt) * up).astype(o_ref.dtype)


def ffn_in(h1, wfi):
    M, D = h1.shape
    F = wfi.shape[1] // 2
    tm = _pick(M, (1024, 512, 256, 128, 8))
    tn = _pick(F, (512, 256, 128))
    nj = F // tn
    return pl.pallas_call(
        _ffn_in_kernel,
        grid=(M // tm, nj),
        in_specs=[pl.BlockSpec((tm, D), lambda i, j: (i, 0)),
                  pl.BlockSpec((D, tn), lambda i, j: (0, j)),
                  pl.BlockSpec((D, tn), lambda i, j: (0, j + nj))],
        out_specs=pl.BlockSpec((tm, tn), lambda i, j: (i, j)),
        out_shape=jax.ShapeDtypeStruct((M, F), BF16),
        compiler_params=_cparams(("parallel", "arbitrary")),
        name="ffn_in",
    )(h1, wfi, wfi)


def _ffn_out_kernel(a_ref, w_ref, x_ref, o_ref):
    o_ref[...] = x_ref[...] + _dot(a_ref[...], w_ref[...])


def ffn_out(act, wfo, x1):
    M, F = act.shape
    D = wfo.shape[1]
    tm = _pick(M, (512, 256, 128, 8))
    tn = _pick(D, (512, 256, 128))
    return pl.pallas_call(
        _ffn_out_kernel,
        grid=(M // tm, D // tn),
        in_specs=[pl.BlockSpec((tm, F), lambda i, j: (i, 0)),
                  pl.BlockSpec((F, tn), lambda i, j: (0, j)),
                  pl.BlockSpec((tm, tn), lambda i, j: (i, j))],
        out_specs=pl.BlockSpec((tm, tn), lambda i, j: (i, j)),
        out_shape=jax.ShapeDtypeStruct((M, D), F32),
        compiler_params=_cparams(("parallel", "arbitrary")),
        name="ffn_out",
    )(act, wfo, x1)


def _ple_kernel(x_ref, g_ref, wpg_ref, pe_ref, wple_ref, gn_ref, o_ref, hn_ref):
    x = x_ref[...]
    h2 = _rms(x, g_ref[...]).astype(BF16)
    gate = jax.nn.sigmoid(_dot(h2, wpg_ref[...]))
    x3 = x + gate * _dot(pe_ref[...].astype(BF16), wple_ref[...])
    o_ref[...] = x3
    hn_ref[...] = _rms(x3, gn_ref[...]).astype(hn_ref.dtype)


def ple_gate(x2, gain2, wpg, pe, wple, gain_next):
    M, D = x2.shape
    Pd = pe.shape[1]
    tm = _pick(M, (256, 128, 8))
    return pl.pallas_call(
        _ple_kernel,
        grid=(M // tm,),
        in_specs=[pl.BlockSpec((tm, D), lambda i: (i, 0)),
                  pl.BlockSpec((1, D), lambda i: (0, 0)),
                  pl.BlockSpec((D, D), lambda i: (0, 0)),
                  pl.BlockSpec((tm, Pd), lambda i: (i, 0)),
                  pl.BlockSpec((Pd, D), lambda i: (0, 0)),
                  pl.BlockSpec((1, D), lambda i: (0, 0))],
        out_specs=[pl.BlockSpec((tm, D), lambda i: (i, 0)), pl.BlockSpec((tm, D), lambda i: (i, 0))],
        out_shape=[jax.ShapeDtypeStruct((M, D), F32), jax.ShapeDtypeStruct((M, D), BF16)],
        compiler_params=_cparams(("parallel",)),
        name="ple_gate",
    )(x2, gain2.reshape(1, D), wpg, pe, wple, gain_next.reshape(1, D))


SEL_BIAS_KINDS = 3
WIN_BIAS_KINDS = WINDOW // LANES + 2


def _bias_sw_kernel(tbl_ref, bs_ref, bw_ref):
    g = pl.program_id(0)
    kk = _iota((LANES, LANES), 0)
    qq = _iota((LANES, LANES), 1)
    bw_ref[0, WIN_BIAS_KINDS - 1] = jnp.full((LANES, R * LANES), NEG_INF, F32)
    for off in range(WIN_BIAS_KINDS - 1):
        d = off * LANES + qq - kk
        bkt = _bucket(d)
        in_window = (d >= 0) & (d < WINDOW)
        for r in range(R):
            cols = slice(r * LANES, (r + 1) * LANES)
            b = _lookup(bkt, [tbl_ref[k, g * R + r] for k in range(N_BUCKETS)])
            bw_ref[0, off, :, cols] = jnp.where(in_window, b, NEG_INF)
            if off < SEL_BIAS_KINDS - 1:
                bs_ref[0, off, :, cols] = b
    for r in range(R):
        bs_ref[0, SEL_BIAS_KINDS - 1, :, r * LANES:(r + 1) * LANES] = jnp.full(
            (LANES, LANES), tbl_ref[N_BUCKETS - 1, g * R + r], F32)


def bias_sel_win(table):
    shapes = [(G, SEL_BIAS_KINDS, LANES, R * LANES), (G, WIN_BIAS_KINDS, LANES, R * LANES)]
    return pl.pallas_call(
        _bias_sw_kernel,
        grid=(G,),
        in_specs=[pl.BlockSpec(memory_space=pltpu.SMEM)],
        out_specs=[pl.BlockSpec((1,) + s[1:], lambda g: (g, 0, 0, 0)) for s in shapes],
        out_shape=[jax.ShapeDtypeStruct(s, F32) for s in shapes],
        compiler_params=_cparams(("arbitrary",)),
        name="bias_sel_win",
    )(table)


def _bias_cmp_kernel(tbl_ref, bc_ref):
    g = pl.program_id(0)
    qi = pl.program_id(1)
    cc = _iota((LANES, LANES), 0)
    qq = _iota((LANES, LANES), 1)
    d = qi * LANES + qq - (cc * CMP_STRIDE + (CMP_LEN - 1))
    bkt = _bucket(d)
    for r in range(R):
        b = _lookup(bkt, [tbl_ref[k, g * R + r] for k in range(N_BUCKETS)])
        bc_ref[0, 0, :, r * LANES:(r + 1) * LANES] = jnp.where(d >= 0, b, NEG_INF)


def bias_cmp(table, n_qblk):
    return pl.pallas_call(
        _bias_cmp_kernel,
        grid=(G, n_qblk),
        in_specs=[pl.BlockSpec(memory_space=pltpu.SMEM)],
        out_specs=pl.BlockSpec((1, 1, LANES, R * LANES), lambda g, i: (g, i, 0, 0)),
        out_shape=jax.ShapeDtypeStruct((G, n_qblk, LANES, R * LANES), F32),
        compiler_params=_cparams(("arbitrary", "arbitrary")),
        name="bias_cmp",
    )(table)


def _cumsum_kernel(x_ref, cr_ref, cp_ref):
    T = x_ref.shape[1]
    ii = _iota((LANES, LANES), 0)
    jj = _iota((LANES, LANES), 1)
    tri = jnp.where(jj <= ii, 1.0, 0.0).astype(BF16)
    carry = jnp.zeros((1, LANES), F32)
    for blk in range(T // LANES):
        sl = slice(blk * LANES, (blk + 1) * LANES)
        c = _dot_split_rhs(tri, x_ref[0, sl, :]) + carry
        cr_ref[0, :, sl] = c.T[LOGF_LANE0:LOGF_LANE0 + FOX_HEADS, :]
        for h in range(FOX_HEADS):
            cp_ref[0, h, sl, :] = jnp.broadcast_to(c[:, LOGF_LANE0 + h:LOGF_LANE0 + h + 1], (LANES, LANES))
        carry = c[LANES - 1:LANES, :]


def cumsum_logf(small):
    B, T, _ = small.shape
    return pl.pallas_call(
        _cumsum_kernel,
        grid=(B,),
        in_specs=[pl.BlockSpec((1, T, LANES), lambda b: (b, 0, 0))],
        out_specs=[pl.BlockSpec((1, FOX_HEADS, T), lambda b: (b, 0, 0)),
                   pl.BlockSpec((1, FOX_HEADS, T, LANES), lambda b: (b, 0, 0, 0))],
        out_shape=[jax.ShapeDtypeStruct((B, FOX_HEADS, T), F32),
                   jax.ShapeDtypeStruct((B, FOX_HEADS, T, LANES), F32)],
        compiler_params=_cparams(("parallel",)),
        name="cumsum_logf",
    )(small)


def key_blocks_transposed(x, n_heads, blk):
    B, T, _ = x.shape
    return x.reshape(B, T // blk, blk, n_heads, HD).transpose(0, 3, 1, 4, 2).astype(BF16)


def _fox_prompt_kernel(q_ref, k_ref, vt_ref, cq_ref, ck_ref, o_ref, *, tk):
    tq = LANES
    qi = pl.program_id(1)
    nh = FOX_HEADS
    qf = q_ref[0]
    qs = [(qf[:, h * HD:(h + 1) * HD] * SCALE).astype(BF16) for h in range(nh)]
    kk = _iota((tk, tq), 0)
    qq = _iota((tk, tq), 1)

    def step(kb, carry, diagonal):
        k0 = pl.multiple_of(kb * tk, tk)
        scores = [_dot_nt(k_ref[0, pl.ds(k0, tk), h * HD:(h + 1) * HD], qs[h]) for h in range(nh)]
        probs = []
        for h in range(nh):
            m, l, _ = carry[h]
            s = scores[h] + (cq_ref[0, h, pl.ds(qi, 1), :] - ck_ref[0, h, pl.ds(k0, tk), :])
            if diagonal:
                s = jnp.where(k0 + kk <= qi * tq + qq, s, NEG_INF)
            m_new = jnp.maximum(m, jnp.max(s, axis=0, keepdims=True))
            alpha = jnp.exp(m - m_new)
            p = jnp.exp(s - m_new)
            probs.append((m_new, alpha, alpha * l + jnp.sum(p, axis=0, keepdims=True), p.astype(BF16)))
        out = []
        for h in range(nh):
            m_new, alpha, l, p = probs[h]
            out.append((m_new, l, alpha * carry[h][2] + _dot(vt_ref[0, h, kb], p)))
        return tuple(out)

    n_full = lax.div(qi * tq, tk)
    n_all = lax.div(qi * tq + tq + tk - 1, tk)
    one = (jnp.full((1, tq), NEG_INF, F32), jnp.zeros((1, tq), F32), jnp.zeros((HD, tq), F32))
    carry = lax.fori_loop(0, n_full, lambda kb, c: step(kb, c, False), (one,) * nh)
    res = lax.fori_loop(n_full, n_all, lambda kb, c: step(kb, c, True), carry)
    for h in range(nh):
        _, l, acc = res[h]
        o_ref[0, :, h * HD:(h + 1) * HD] = (acc / l).T.astype(o_ref.dtype)


def fox_prompt(qf, k_bf, v_t, c_row, c_rep):
    B, T, W = qf.shape
    tq = LANES
    tk = v_t.shape[-1]
    assert T % tk == 0 and tk % tq == 0
    c_row4 = c_row.reshape(B, FOX_HEADS, T // tq, tq)
    return pl.pallas_call(
        functools.partial(_fox_prompt_kernel, tk=tk),
        grid=(B, T // tq),
        in_specs=[pl.BlockSpec((1, tq, W), lambda b, i: (b, i, 0)),
                  pl.BlockSpec((1, T, W), lambda b, i: (b, 0, 0)),
                  pl.BlockSpec((1, FOX_HEADS, T // tk, HD, tk), lambda b, i: (b, 0, 0, 0, 0)),
                  pl.BlockSpec((1, FOX_HEADS, T // tq, tq), lambda b, i: (b, 0, 0, 0)),
                  pl.BlockSpec((1, FOX_HEADS, T, LANES), lambda b, i: (b, 0, 0, 0))],
        out_specs=pl.BlockSpec((1, tq, W), lambda b, i: (b, i, 0)),
        out_shape=jax.ShapeDtypeStruct((B, T, W), BF16),
        compiler_params=_cparams(("parallel", "arbitrary")),
        name="fox_prompt",
    )(qf, k_bf, v_t, c_row4, c_rep)


def _cmp_p_kernel(pt_ref, *refs, npg, row_per_head):
    nref = npg if row_per_head else npg * 2 * G
    page_refs = refs[:nref]
    w_ref, o_ref, lhs_ref = refs[nref:]
    rph = 4 * G
    for i in range(npg):
        for c in range(2 * G):
            for t in range(CMP_STRIDE):
                if row_per_head:
                    rows = page_refs[i][pl.ds(t * rph + c, CHUNKS_PER_PAGE, stride=CMP_STRIDE * rph), :]
                else:
                    rows = page_refs[i * 2 * G + c][pl.ds(t, CHUNKS_PER_PAGE, stride=CMP_STRIDE), :]
                lhs_ref[c, i * CHUNKS_PER_PAGE:(i + 1) * CHUNKS_PER_PAGE, t * HD:(t + 1) * HD] = rows
    for kv in range(2):
        for g in range(G):
            o_ref[0, kv, g] = _dot(lhs_ref[kv * G + g].astype(BF16), w_ref[kv])


def compress_partial(pages, page_table, layer, w1cat, row_per_head):
    NB, n_pages = page_table.shape
    npg = _pick(n_pages, (16, 8, 4, 2, 1))
    rows = npg * CHUNKS_PER_PAGE
    kdim = CMP_STRIDE * HD
    in_specs = []
    for i in range(npg):
        if row_per_head:
            in_specs.append(pl.BlockSpec((None, None, 128 * 4 * G, HD),
                                         lambda b, p, pt, i=i: (pt[b, p * npg + i], layer, 0, 0)))
            continue
        for c in range(2 * G):
            in_specs.append(pl.BlockSpec((None, None, 128, HD),
                                         lambda b, p, pt, i=i, c=c: (pt[b, p * npg + i], layer, 0, c)))
    in_specs.append(pl.BlockSpec((2, kdim, 2 * HD), lambda b, p, pt: (0, 0, 0)))
    return pl.pallas_call(
        functools.partial(_cmp_p_kernel, npg=npg, row_per_head=row_per_head),
        grid_spec=pltpu.PrefetchScalarGridSpec(
            num_scalar_prefetch=1,
            grid=(NB, n_pages // npg),
            in_specs=in_specs,
            out_specs=pl.BlockSpec((1, 2, G, rows, 2 * HD), lambda b, p, pt: (b, 0, 0, p, 0)),
            scratch_shapes=[pltpu.VMEM((2 * G, rows, kdim), F32)]),
        out_shape=jax.ShapeDtypeStruct((NB, 2, G, n_pages * CHUNKS_PER_PAGE, 2 * HD), F32),
        compiler_params=_cparams(("parallel", "arbitrary")),
        name="compress_partial",
    )(page_table, *([pages] * len(in_specs[:-1])), w1cat)


def _gelu_tanh(x):
    return x * (0.5 * (1.0 + jnp.tanh(math.sqrt(2.0 / math.pi) * (x + 0.044715 * (x * x * x)))))


def _cmp_fin_kernel(p_ref, pos_ref, w1_ref, b1_ref, w2_ref, gk_ref, kc_ref, vc_ref):
    NC = p_ref.shape[3]
    row = _iota((NC, HD), 0)
    for kv in range(2):
        w1 = w1_ref[kv]
        posterm = (_dot(pos_ref[kv, 0], w1[:, :HD]) + _dot(pos_ref[kv, 1], w1[:, HD:]))[0:1, :] + b1_ref[kv]
        for g in range(G):
            P = p_ref[0, kv, g]
            pre = P[:, :HD] + pltpu.roll(P[:, HD:], NC - 1, 0) + posterm
            o = _dot(_gelu_tanh(pre).astype(BF16), w2_ref[kv])
            if kv == 0:
                o = _rms(o, gk_ref[...])
            o = jnp.where(row < NC - 1, o, 0.0)
            if kv == 0:
                kc_ref[0, g] = o
            else:
                vc_ref[0, g] = o


def compress_finish(part, pos_ab, w1cat, b1, w2, gain_kc):
    NB, _, _, NC, _ = part.shape
    kdim = CMP_STRIDE * HD
    full = lambda shape: pl.BlockSpec(shape, lambda b: (0,) * len(shape))
    return pl.pallas_call(
        _cmp_fin_kernel,
        grid=(NB,),
        in_specs=[pl.BlockSpec((1, 2, G, NC, 2 * HD), lambda b: (b, 0, 0, 0, 0)),
                  full((2, 2, SUBLANES, kdim)), full((2, kdim, 2 * HD)), full((2, 1, HD)),
                  full((2, HD, HD)), full((1, HD))],
        out_specs=[pl.BlockSpec((1, G, NC, HD), lambda b: (b, 0, 0, 0)),
                   pl.BlockSpec((1, G, NC, HD), lambda b: (b, 0, 0, 0))],
        out_shape=[jax.ShapeDtypeStruct((NB, G, NC, HD), F32), jax.ShapeDtypeStruct((NB, G, NC, HD), F32)],
        compiler_params=_cparams(("parallel",)),
        name="compress_finish",
    )(part, pos_ab, w1cat, b1, w2, gain_kc)


def _sel_overlap_matrix(nc, nsb, n_cb):
    ci = _iota((nc, nsb), 0)
    jb = _iota((nc, nsb), 1)
    ratio = SEL_BLOCK // CMP_STRIDE
    first = ratio * jb - (CMP_LEN // CMP_STRIDE) + 1
    n_ov = (SEL_BLOCK + CMP_LEN) // CMP_STRIDE - 1
    hit = (ci >= first) & (ci < first + n_ov) & (ci < n_cb)
    return jnp.where(hit, 1.0, 0.0).astype(BF16)


def _sel_overlap_matrix_t(nsb, nc, n_cb):
    jb = _iota((nsb, nc), 0)
    ci = _iota((nsb, nc), 1)
    first = (SEL_BLOCK // CMP_STRIDE) * jb - (CMP_LEN // CMP_STRIDE) + 1
    n_ov = (SEL_BLOCK + CMP_LEN) // CMP_STRIDE - 1
    hit = (ci >= first) & (ci < first + n_ov) & (ci < n_cb)
    return jnp.where(hit, 1.0, 0.0).astype(BF16)


def _div_pow2(x, d):
    if isinstance(x, int):
        return x // d
    return lax.shift_right_arithmetic(x, jnp.full(x.shape, d.bit_length() - 1, I32))


def _block_scores(imp_s, qpos, jb, n_sb):
    back = _div_pow2(qpos, SEL_BLOCK) - jb
    forced = (jb == 0) | ((back >= 0) & (back < N_LOCAL_BLOCKS))
    score = jnp.where(back >= 0, imp_s + jnp.where(forced, FORCE_BONUS, 0.0), -1.0)
    return jnp.where(jb < n_sb, score, -2.0)


def _nsa_prompt_kernel(q_ref, kc_ref, vct_ref, ks_ref, vst_ref, kw_ref, vwt_ref, br_ref, bc_ref, bs_ref, bw_ref,
                       exp_ref, o_ref, msk_ref, oc_ref, brt_ref, *, n_cb, n_sb, n_sel):
    g = pl.program_id(1)
    qi = pl.program_id(2)
    tq = LANES
    qf = q_ref[0]
    qt = jnp.concatenate([(qf[:, r * HD:(r + 1) * HD] * SCALE).T for r in range(R)], axis=1).astype(BF16)
    qpos = qi * tq + _iota((1, tq), 1)
    heads = lambda x: jnp.concatenate([x] * R, axis=1)

    has_c = heads(qpos >= CMP_LEN - 1)
    s = _dot(kc_ref[0, 0].astype(BF16), qt) + bc_ref[0, 0]
    e = jnp.exp(s - jnp.max(s, axis=0, keepdims=True))
    p = e * jnp.where(has_c, 1.0 / jnp.sum(e, axis=0, keepdims=True), 0.0)
    oc_ref[...] = _dot(vct_ref[0, 0].astype(BF16), p.astype(BF16))
    imp_c = p[:, 0:tq]
    for r in range(1, R):
        imp_c = imp_c + p[:, r * tq:(r + 1) * tq]

    nsb8 = -(-n_sb // SUBLANES) * SUBLANES
    imp_s = _dot_split_rhs(_sel_overlap_matrix_t(LANES, LANES, n_cb), imp_c)[0:nsb8]
    jb = _iota((nsb8, tq), 0)
    score = _block_scores(imp_s, qpos, jb, n_sb)
    rank = jnp.zeros((nsb8, tq), I32)
    for i in range(n_sb):
        row = score[i:i + 1, :]
        rank = rank + jnp.where(row > score, 1, jnp.where((row == score) & (jb > i), 1, 0))
    sel = jnp.where(rank < n_sel, 1.0, 0.0)
    if nsb8 < LANES:
        sel = jnp.concatenate([sel, jnp.zeros((LANES - nsb8, tq), F32)], axis=0)
    sel = sel.astype(BF16)

    n_keys = msk_ref.shape[0]
    chosen = _dot(exp_ref[...], sel)
    key_pos = _iota((n_keys, tq), 0)
    msk_ref[...] = jnp.where((chosen > 0.5) & (qi * tq + _iota((n_keys, tq), 1) >= key_pos), 0.0, NEG_INF)

    def flash(k_ref, vt_ref, b_ref, kind_of, first_blk, masked):
        def body(j, carry):
            m, l, acc = carry
            k0 = pl.multiple_of(j * (2 * LANES), 2 * LANES)
            off = qi - 2 * j
            bias = jnp.concatenate([b_ref[0, kind_of(off)], b_ref[0, kind_of(off - 1)]], axis=0)
            s = _dot(k_ref[0, pl.ds(k0, 2 * LANES), :], qt) + bias
            if masked:
                s = s + heads(msk_ref[pl.ds(k0, 2 * LANES), :])
            m_new = jnp.maximum(m, jnp.max(s, axis=0, keepdims=True))
            alpha = jnp.exp(m - m_new)
            p = jnp.exp(s - m_new)
            l = alpha * l + jnp.sum(p, axis=0, keepdims=True)
            vblk = jnp.concatenate([vt_ref[0, 0, 2 * j], vt_ref[0, 0, 2 * j + 1]], axis=1)
            acc = alpha * acc + _dot(vblk, p.astype(BF16))
            return m_new, l, acc

        init = (jnp.full((1, R * tq), NEG_INF, F32), jnp.zeros((1, R * tq), F32), jnp.zeros((HD, R * tq), F32))
        m, l, acc = lax.fori_loop(first_blk // 2, qi // 2 + 1, body, init)
        return acc * (1.0 / l)

    n_win = WINDOW // LANES
    o_s = flash(ks_ref, vst_ref, bs_ref, lambda off: jnp.clip(off, 0, SEL_BIAS_KINDS - 1), 0, True)
    o_w = flash(kw_ref, vwt_ref, bw_ref, lambda off: jnp.where((off < 0) | (off > n_win), n_win + 1, off),
                jnp.maximum(qi - n_win, 0), False)
    brt_ref[...] = br_ref[0].T
    gate = lambda i: jnp.concatenate([brt_ref[pl.ds((g * R + r) * 3 + i, 1), :] for r in range(R)], axis=1)
    o = gate(0) * oc_ref[...] + gate(1) * o_s + gate(2) * o_w
    for r in range(R):
        o_ref[0, :, r * HD:(r + 1) * HD] = o[:, r * tq:(r + 1) * tq].T.astype(o_ref.dtype)


def nsa_prompt(qa, kc, vc_t, ks_bf, vs_t, kw_bf, vw_t, small, b_cmp, b_sel, b_win):
    B, T, _ = qa.shape
    assert T % LANES == 0 and T // CMP_STRIDE == LANES and kc.shape[2] == LANES
    n_cb = T // CMP_STRIDE - CMP_LEN // CMP_STRIDE + 1
    n_sb = -(-T // SEL_BLOCK)
    n_sel = min(SEL_TOPN, n_sb)
    tq = LANES
    nkb = T // LANES
    assert nkb % 2 == 0 and n_sb <= LANES
    grp = lambda b, g, i: (b, g, 0, 0)
    expand = (jnp.arange(T, dtype=I32)[:, None] // SEL_BLOCK == jnp.arange(LANES, dtype=I32)[None, :]).astype(BF16)
    return pl.pallas_call(
        functools.partial(_nsa_prompt_kernel, n_cb=n_cb, n_sb=n_sb, n_sel=n_sel),
        grid=(B, G, T // tq),
        in_specs=[pl.BlockSpec((1, tq, R * HD), lambda b, g, i: (b, i, g)),
                  pl.BlockSpec((1, 1, LANES, HD), grp),
                  pl.BlockSpec((1, 1, HD, LANES), grp),
                  pl.BlockSpec((1, T, HD), lambda b, g, i: (b, 0, g)),
                  pl.BlockSpec((1, 1, nkb, HD, LANES), lambda b, g, i: (b, g, 0, 0, 0)),
                  pl.BlockSpec((1, T, HD), lambda b, g, i: (b, 0, g)),
                  pl.BlockSpec((1, 1, nkb, HD, LANES), lambda b, g, i: (b, g, 0, 0, 0)),
                  pl.BlockSpec((1, tq, LANES), lambda b, g, i: (b, i, 0)),
                  pl.BlockSpec((1, 1, LANES, R * LANES), lambda b, g, i: (g, i, 0, 0)),
                  pl.BlockSpec((1, SEL_BIAS_KINDS, LANES, R * LANES), lambda b, g, i: (g, 0, 0, 0)),
                  pl.BlockSpec((1, WIN_BIAS_KINDS, LANES, R * LANES), lambda b, g, i: (g, 0, 0, 0)),
                  pl.BlockSpec((T, LANES), lambda b, g, i: (0, 0))],
        out_specs=pl.BlockSpec((1, tq, R * HD), lambda b, g, i: (b, i, g)),
        out_shape=jax.ShapeDtypeStruct((B, T, NSA_HEADS * HD), BF16),
        scratch_shapes=[pltpu.VMEM((T, tq), F32), pltpu.VMEM((HD, R * tq), F32),
                        pltpu.VMEM((LANES, tq), F32)],
        compiler_params=_cparams(("parallel", "parallel", "arbitrary")),
        name="nsa_prompt",
    )(qa, kc, vc_t, ks_bf, vs_t, kw_bf, vw_t, small, b_cmp, b_sel, b_win, expand)


def _dec_cmp_kernel(q_ref, kc_ref, vc_ref, tblt_ref, oc_ref, idx_ref, *, qpos, n_cb, n_sb, n_sel, nsbp):
    NC = kc_ref.shape[2]
    q8 = q_ref[0].astype(BF16)
    row = _iota((NSA_HEADS, NC), 0)
    j = _iota((NSA_HEADS, NC), 1)
    s = jnp.where(row < R, _dot_nt(q8, kc_ref[0, 0].astype(BF16)), _dot_nt(q8, kc_ref[0, 1].astype(BF16)))
    dist = qpos - (j * CMP_STRIDE + (CMP_LEN - 1))
    valid = dist >= 0
    tblt = tblt_ref[...]
    bias = _lookup(_bucket(dist), [tblt[:, k:k + 1] for k in range(N_BUCKETS)])
    s = jnp.where(valid, s * SCALE + bias, NEG_INF)
    m = jnp.max(s, axis=1, keepdims=True)
    e = jnp.where(valid, jnp.exp(s - m), 0.0)
    l = jnp.sum(e, axis=1, keepdims=True)
    p = e / jnp.where(l > 0.0, l, 1.0)
    pb = p.astype(BF16)
    row_o = _iota((NSA_HEADS, HD), 0)
    oc_ref[0] = jnp.where(row_o < R, _dot(pb, vc_ref[0, 0].astype(BF16)), _dot(pb, vc_ref[0, 1].astype(BF16)))

    imp0 = jnp.sum(jnp.where(row < R, p, 0.0), axis=0, keepdims=True)
    imp1 = jnp.sum(jnp.where(row >= R, p, 0.0), axis=0, keepdims=True)
    imp = jnp.where(row == 0, imp0, jnp.where(row == 1, imp1, 0.0))
    imp_s = _dot_split_lhs(imp, _sel_overlap_matrix(NC, nsbp, n_cb))
    jb = _iota((NSA_HEADS, nsbp), 1)
    score = _block_scores(imp_s, qpos, jb, n_sb)
    jbf = jb.astype(F32)
    lane_o = _iota((NSA_HEADS, LANES), 1)
    out = jnp.zeros((NSA_HEADS, LANES), I32)
    for n in range(n_sel):
        mx = jnp.max(score, axis=1, keepdims=True)
        am = jnp.min(jnp.where(score == mx, jbf, float(nsbp)), axis=1, keepdims=True)
        out = jnp.where(lane_o == n, am.astype(I32), out)
        score = jnp.where(jbf == am, -3.0, score)
    idx_ref[0] = out


def dec_cmp(q8, kc, vc, tblt, qpos):
    DB = q8.shape[0]
    NC = kc.shape[2]
    L = qpos + 1
    n_cb = L // CMP_STRIDE - CMP_LEN // CMP_STRIDE + 1
    assert n_cb == NC - 1
    n_sb = -(-L // SEL_BLOCK)
    n_sel = min(SEL_TOPN, n_sb)
    nsbp = -(-n_sb // LANES) * LANES
    oc, idx = pl.pallas_call(
        functools.partial(_dec_cmp_kernel, qpos=qpos, n_cb=n_cb, n_sb=n_sb, n_sel=n_sel, nsbp=nsbp),
        grid=(DB,),
        in_specs=[pl.BlockSpec((1, NSA_HEADS, HD), lambda b: (b, 0, 0)),
                  pl.BlockSpec((1, G, NC, HD), lambda b: (b, 0, 0, 0)),
                  pl.BlockSpec((1, G, NC, HD), lambda b: (b, 0, 0, 0)),
                  pl.BlockSpec((NSA_HEADS, N_BUCKETS), lambda b: (0, 0))],
        out_specs=[pl.BlockSpec((1, NSA_HEADS, HD), lambda b: (b, 0, 0)),
                   pl.BlockSpec((1, NSA_HEADS, LANES), lambda b: (b, 0, 0))],
        out_shape=[jax.ShapeDtypeStruct((DB, NSA_HEADS, HD), F32), jax.ShapeDtypeStruct((DB, NSA_HEADS, LANES), I32)],
        compiler_params=_cparams(("parallel",)),
        name="dec_cmp",
    )(q8, kc, vc, tblt)
    return oc, idx[:, :G, :n_sel].reshape(DB, G * n_sel), n_sel


def _softmax_with_new(s, valid, s_new, new_ok):
    s = jnp.where(valid, s, NEG_INF)
    s_new = jnp.where(new_ok, s_new, NEG_INF)
    m = jnp.maximum(jnp.max(s, axis=1, keepdims=True), s_new)
    e = jnp.where(valid, jnp.exp(s - m), 0.0)
    en = jnp.where(new_ok, jnp.exp(s_new - m), 0.0)
    l = jnp.sum(e, axis=1, keepdims=True) + en
    inv = 1.0 / jnp.where(l > 0.0, l, 1.0)
    return e, en, inv


def _dec_selwin_kernel(pt_ref, idx_ref, *refs, n_sel, qpos, jb_new, w_buf):
    blk_refs = refs[:G * n_sel]
    q_ref, new_ref, win_ref, wnew_ref, br_ref, oc_ref, tblt_ref, o_ref = refs[G * n_sel:]
    rph = 4 * G
    b = pl.program_id(0)
    q8 = q_ref[0]
    row8 = _iota((NSA_HEADS, HD), 0)
    tblt = tblt_ref[...]
    tcols = [tblt[:, k:k + 1] for k in range(N_BUCKETS)]
    new8 = new_ref[0]
    wnew = wnew_ref[0]
    brrow = br_ref[0]
    oc = oc_ref[0]
    nk = n_sel * SEL_BLOCK
    lane = _iota((NSA_HEADS, nk), 1)
    outs = []
    for g in range(G):
        qg = jnp.where(_div_pow2(row8, R) == g, q8, 0.0)
        qb = qg.astype(BF16)
        qr = qb.astype(F32)
        kmat = jnp.concatenate([blk_refs[g * n_sel + n][pl.ds(2 * G + g, SEL_BLOCK, stride=rph), :]
                                for n in range(n_sel)], axis=0).astype(BF16)
        vmat = jnp.concatenate([blk_refs[g * n_sel + n][pl.ds(3 * G + g, SEL_BLOCK, stride=rph), :]
                                for n in range(n_sel)], axis=0).astype(BF16)
        idxv = jnp.zeros((NSA_HEADS, nk), I32)
        has_new = jnp.zeros((), jnp.bool_)
        for n in range(n_sel):
            sidx = idx_ref[b, g * n_sel + n]
            idxv = jnp.where(_div_pow2(lane, SEL_BLOCK) == n, sidx, idxv)
            has_new = has_new | (sidx == jb_new)
        dist = qpos - (idxv * SEL_BLOCK + (lane & (SEL_BLOCK - 1)))
        valid = (idxv < jb_new) & (dist >= 0)
        s = _dot_nt(qb, kmat) * SCALE + _lookup(_bucket(dist), tcols)
        kn = new8[2 * G + g:2 * G + g + 1, :].astype(BF16).astype(F32)
        vn = new8[3 * G + g:3 * G + g + 1, :].astype(BF16).astype(F32)
        s_new = jnp.sum(qr * kn, axis=1, keepdims=True) * SCALE + tcols[0]
        e, en, inv = _softmax_with_new(s, valid, s_new, has_new)
        o_s = (_dot(e.astype(BF16), vmat) + en * vn) * inv

        kwin = win_ref[pl.ds(g, w_buf, stride=2 * G), :].astype(BF16)
        vwin = win_ref[pl.ds(G + g, w_buf, stride=2 * G), :].astype(BF16)
        lw = _iota((NSA_HEADS, w_buf), 1)
        dist_w = w_buf - lw
        valid_w = (dist_w < WINDOW) & (qpos - dist_w >= 0)
        sw = _dot_nt(qb, kwin) * SCALE + _lookup(_bucket(dist_w), tcols)
        kwn = wnew[g:g + 1, :].astype(BF16).astype(F32)
        vwn = wnew[G + g:G + g + 1, :].astype(BF16).astype(F32)
        sw_new = jnp.sum(qr * kwn, axis=1, keepdims=True) * SCALE + tcols[0]
        ew, ewn, invw = _softmax_with_new(sw, valid_w, sw_new, jnp.ones((), jnp.bool_))
        o_w = (_dot(ew.astype(BF16), vwin) + ewn * vwn) * invw

        def bcol(i):
            return jnp.sum(jnp.where(_iota((NSA_HEADS, LANES), 1) == row8 * 3 + i, brrow, 0.0), axis=1, keepdims=True)

        outs.append(bcol(0) * oc + bcol(1) * o_s + bcol(2) * o_w)
    o_ref[0] = jnp.where(row8 < R, outs[0], outs[1])


def dec_selwin(q8, new8, wnew4, small_s, oc, tblt, cache_nsa_kv, state_win, page_table, idx, n_sel, layer, qpos):
    DB = q8.shape[0]
    n_pool, depth = cache_nsa_kv.shape[:2]
    page = cache_nsa_kv.shape[2]
    halves = page // SEL_BLOCK
    w_buf = state_win.shape[2]
    n_pages = page_table.shape[1]
    jb_new = qpos // SEL_BLOCK
    rph = 4 * G
    cache_rows = cache_nsa_kv.reshape(n_pool, depth, page * rph, HD)
    state_rows = state_win.reshape(depth, DB, w_buf * 2 * G, HD)
    in_specs = []
    args = []
    for g in range(G):
        for n in range(n_sel):
            def imap(b, pt, ix, g=g, n=n):
                jb = jnp.clip(ix[b, g * n_sel + n], 0, jb_new - 1)
                return (pt[b, jb // halves], layer, jb % halves, 0)
            in_specs.append(pl.BlockSpec((None, None, SEL_BLOCK * rph, HD), imap))
            args.append(cache_rows)
    row3 = lambda b, pt, ix: (b, 0, 0)
    in_specs += [pl.BlockSpec((1, NSA_HEADS, HD), row3), pl.BlockSpec((1, 4 * G, HD), row3),
                 pl.BlockSpec((None, None, w_buf * 2 * G, HD), lambda b, pt, ix: (layer, b, 0, 0)),
                 pl.BlockSpec((1, 2 * G, HD), row3), pl.BlockSpec((1, 1, LANES), row3),
                 pl.BlockSpec((1, NSA_HEADS, HD), row3),
                 pl.BlockSpec((NSA_HEADS, N_BUCKETS), lambda b, pt, ix: (0, 0))]
    args += [q8, new8, state_rows, wnew4, small_s.reshape(DB, 1, LANES), oc, tblt]
    return pl.pallas_call(
        functools.partial(_dec_selwin_kernel, n_sel=n_sel, qpos=qpos, jb_new=jb_new, w_buf=w_buf),
        grid_spec=pltpu.PrefetchScalarGridSpec(
            num_scalar_prefetch=2,
            grid=(DB,),
            in_specs=in_specs,
            out_specs=pl.BlockSpec((1, NSA_HEADS, HD), row3)),
        out_shape=jax.ShapeDtypeStruct((DB, NSA_HEADS, HD), F32),
        compiler_params=_cparams(("arbitrary",)),
        name="dec_selwin",
    )(page_table, idx, *args)


def _dec_fox_kernel(pt_ref, *refs, npg):
    kv_refs = refs[:npg]
    lf_refs = refs[npg:2 * npg]
    q_ref, new_ref, lfn_ref, o_ref, m_sc, l_sc, acc_sc, car_sc = refs[2 * npg:]
    p = pl.program_id(1)
    nh = FOX_HEADS
    page = kv_refs[0].shape[0]
    q8 = q_ref[0]
    qs = q8 * SCALE

    @pl.when(p == 0)
    def _():
        new = new_ref[0]
        s_new = jnp.sum(qs * new[0:nh], axis=1, keepdims=True)
        m_sc[...] = jnp.broadcast_to(s_new, (nh, LANES))
        l_sc[...] = jnp.ones((nh, LANES), F32)
        acc_sc[...] = new[nh:2 * nh]
        lane1 = _iota((nh, LANES), 1)
        row1 = _iota((nh, LANES), 0)
        car_sc[...] = jnp.sum(jnp.where(lane1 == LOGF_LANE0 + row1, lfn_ref[0], 0.0), axis=1, keepdims=True)

    def tree(x, op):
        while x.shape[0] > 1:
            half = x.shape[0] // 2
            x = op(x[:half], x[half:])
        return x[0]

    uu = _iota((LANES, LANES), 0)
    ss = _iota((LANES, LANES), 1)
    later = jnp.where(uu > ss, 1.0, 0.0).astype(BF16)
    ones = jnp.ones((HD, LANES), BF16)
    tok3 = _iota((page, nh, LANES), 0)
    lane3 = _iota((page, nh, LANES), 2)
    m = m_sc[...]
    l = l_sc[...]
    acc = acc_sc[...]
    car = car_sc[...]
    for i in range(npg):
        lft = lf_refs[i][...]
        decay = _dot_split_lhs(lft, later) + car
        lhs = jnp.where(lane3 == tok3, decay[None], 0.0) + kv_refs[i][:, 0] * qs[None]
        lhs = lhs.reshape(page * nh, HD)
        hi = lhs.astype(BF16)
        lo = (lhs - hi.astype(F32)).astype(BF16)
        s = (_dot(hi, ones) + _dot(lo, ones)).reshape(page, nh, LANES)
        m_new = jnp.maximum(m, tree(s, jnp.maximum))
        alpha = jnp.exp(m - m_new)
        pp = jnp.exp(s - m_new[None])
        l = alpha * l + tree(pp, jnp.add)
        acc = alpha * acc + tree(pp * kv_refs[i][:, 1], jnp.add)
        m = m_new
        car = car + jnp.sum(lft, axis=1, keepdims=True)
    m_sc[...] = m
    l_sc[...] = l
    acc_sc[...] = acc
    car_sc[...] = car

    @pl.when(p == pl.num_programs(1) - 1)
    def _():
        o_ref[0] = acc / l


def dec_fox(q8, new16, small_s, cache_fox_kv, logf_t, page_table, layer):
    DB = q8.shape[0]
    n_pool, depth, page = cache_fox_kv.shape[:3]
    assert page == LANES
    n_pages = page_table.shape[1]
    npg = _pick(n_pages, (4, 2, 1))
    in_specs = []
    for i in range(npg):
        in_specs.append(pl.BlockSpec((None, None, page, 2, FOX_HEADS, HD),
                                     lambda b, p, pt, i=i: (pt[b, n_pages - 1 - (p * npg + i)], layer, 0, 0, 0, 0)))
    for i in range(npg):
        in_specs.append(pl.BlockSpec((None, None, FOX_HEADS, page),
                                     lambda b, p, pt, i=i: (pt[b, n_pages - 1 - (p * npg + i)], layer, 0, 0)))
    row3 = lambda b, p, pt: (b, 0, 0)
    in_specs += [pl.BlockSpec((1, FOX_HEADS, HD), row3), pl.BlockSpec((1, 2 * FOX_HEADS, HD), row3),
                 pl.BlockSpec((1, 1, LANES), row3)]
    return pl.pallas_call(
        functools.partial(_dec_fox_kernel, npg=npg),
        grid_spec=pltpu.PrefetchScalarGridSpec(
            num_scalar_prefetch=1,
            grid=(DB, n_pages // npg),
            in_specs=in_specs,
            out_specs=pl.BlockSpec((1, FOX_HEADS, HD), row3),
            scratch_shapes=[pltpu.VMEM((FOX_HEADS, LANES), F32), pltpu.VMEM((FOX_HEADS, LANES), F32),
                            pltpu.VMEM((FOX_HEADS, HD), F32), pltpu.VMEM((FOX_HEADS, 1), F32)]),
        out_shape=jax.ShapeDtypeStruct((DB, FOX_HEADS, HD), F32),
        compiler_params=_cparams(("parallel", "arbitrary")),
        name="dec_fox",
    )(page_table, *([cache_fox_kv] * npg), *([logf_t] * npg), q8, new16, small_s.reshape(DB, 1, LANES))


def _layer_weights(l, D, w_in, b_forget, qk_gain_nsa, qk_gain_fox):
    nq = NSA_HEADS * HD
    nkv = 6 * G * HD
    nfox = 3 * FOX_HEADS * HD
    o_br = nq + nkv
    o_fox = o_br + BR_LANES
    o_f = o_fox + nfox
    o_mg = o_f + FOX_HEADS
    w = w_in[l]
    ones = jnp.ones((HD,), F32)
    gn = qk_gain_nsa[l]
    gf = qk_gain_fox[l]

    def seg(lo, hi):
        return w[:, lo:hi].astype(BF16)

    def gains(rows):
        return jnp.concatenate(rows).reshape(1, -1)

    zeros = lambda n: jnp.zeros((1, n), F32)
    small_w = jnp.concatenate([w[:, o_br:o_br + BR_LANES], w[:, o_f:o_f + FOX_HEADS],
                               jnp.zeros((D, LANES - BR_LANES - FOX_HEADS), F32)], axis=1).astype(BF16)
    small_b = jnp.concatenate([jnp.zeros((BR_LANES,), F32), b_forget[l],
                               jnp.zeros((LANES - BR_LANES - FOX_HEADS,), F32)]).reshape(1, LANES)
    kvh = G
    return {
        "qa": (seg(0, nq), gains([gn[0]] * NSA_HEADS), zeros(nq), ("norm",) * NSA_HEADS, nq),
        "nsakv": (seg(nq, nq + 4 * kvh * HD), gains([ones] * (2 * kvh) + [gn[2]] * kvh + [ones] * kvh),
                  zeros(4 * kvh * HD), ("raw",) * (2 * kvh) + ("norm",) * kvh + ("raw",) * kvh, 4 * kvh * HD),
        "win": (seg(nq + 4 * kvh * HD, o_br), gains([gn[3]] * kvh + [ones] * kvh), zeros(2 * kvh * HD),
                ("norm",) * kvh + ("raw",) * kvh, 2 * kvh * HD),
        "qf": (seg(o_fox, o_fox + FOX_HEADS * HD), gains([gf[0]] * FOX_HEADS), zeros(FOX_HEADS * HD),
               ("norm",) * FOX_HEADS, FOX_HEADS * HD),
        "foxkv": (seg(o_fox + FOX_HEADS * HD, o_f), gains([gf[1]] * FOX_HEADS + [ones] * FOX_HEADS),
                  zeros(2 * FOX_HEADS * HD), ("norm",) * FOX_HEADS + ("raw",) * FOX_HEADS, 2 * FOX_HEADS * HD),
        "gates": (seg(o_mg, o_mg + 2 * D), zeros(2 * D), zeros(2 * D),
                  ("sigmoid",) * (_pick(2 * D, (1024, 512, 256, 128)) // LANES), _pick(2 * D, (1024, 512, 256, 128))),
        "small": (small_w, zeros(LANES), small_b, ("small",), LANES),
    }


def _project_all(h, segs):
    return {name: proj(h, w, gn, bs, modes, tn) for name, (w, gn, bs, modes, tn) in segs.items()}


def _tail(x, o_nsa, o_fox, gates, pe, gains_l, gain_next, wbn, wbf, wo, wfi, wfo, wple, wpg):
    merged = merge(o_nsa, o_fox, gates, wbn, wbf)
    x1, h1 = out_proj(merged, wo, x, gains_l[1])
    act = ffn_in(h1, wfi)
    x2 = ffn_out(act, wfo, x1)
    return ple_gate(x2, gains_l[2], wpg, pe, wple, gain_next)


def kernel(x_prompt, x_sample, cache_nsa_kv, cache_fox_kv, cache_fox_logf, state_nsa_window, page_table, p_prompt, p_sample, rel_bias_table, norm_gains, w_in, b_forget, qk_gain_nsa, qk_gain_fox, cmp_pos, w_cmp1, b_cmp1, w_cmp2, w_branch_nsa, w_branch_fox, w_out, w_ffn_in, w_ffn_out, w_ple, w_ple_gate):
    B, T, D = x_prompt.shape
    DB, n_qs, _ = x_sample.shape
    assert n_qs == 1
    depth = w_in.shape[0]
    page = cache_nsa_kv.shape[2]
    n_pages = page_table.shape[1]
    past = n_pages * page
    w_buf = state_nsa_window.shape[2]
    kdim = CMP_STRIDE * HD

    xp = x_prompt.reshape(B * T, D)
    xs = x_sample.reshape(DB, D)
    b_sel, b_win = bias_sel_win(rel_bias_table)
    b_cmp = bias_cmp(rel_bias_table, T // LANES)
    fox_tk = _pick(T, (256, 128))
    tblt = rel_bias_table.T
    logf_t = jnp.swapaxes(cache_fox_logf, 2, 3)
    pt_prompt = jnp.arange(B * (T // page), dtype=I32).reshape(B, T // page)
    hp = rms_bf16(xp, norm_gains[0, 0])
    hs = rms_bf16(xs, norm_gains[0, 0])

    outs = [[] for _ in range(8)]
    for l in range(depth):
        segs = _layer_weights(l, D, w_in, b_forget, qk_gain_nsa, qk_gain_fox)
        w1 = w_cmp1[l]
        w1cat = jnp.concatenate([w1[:, :kdim], w1[:, kdim:]], axis=2).astype(BF16)
        pos = cmp_pos[l].reshape(2, 2, 1, kdim)
        pos_ab = jnp.broadcast_to(pos, (2, 2, SUBLANES, kdim)).astype(BF16)
        cmp_args = (pos_ab, w1cat, b_cmp1[l].reshape(2, 1, HD), w_cmp2[l].astype(BF16),
                    qk_gain_nsa[l, 1].reshape(1, HD))
        gain_next = norm_gains[l + 1, 0] if l + 1 < depth else norm_gains[l, 0]
        tail_w = (norm_gains[l], gain_next, w_branch_nsa[l].astype(BF16), w_branch_fox[l].astype(BF16),
                  w_out[l].astype(BF16), w_ffn_in[l].astype(BF16), w_ffn_out[l].astype(BF16),
                  w_ple[l].astype(BF16), w_ple_gate[l].astype(BF16))

        P = _project_all(hp, segs)
        small3 = P["small"].reshape(B, T, LANES)
        c_row, c_rep = cumsum_logf(small3)
        foxkv3 = P["foxkv"].reshape(B, T, -1)
        nfk = FOX_HEADS * HD
        o_fox_p = fox_prompt(P["qf"].reshape(B, T, -1), foxkv3[:, :, :nfk].astype(BF16),
                             key_blocks_transposed(foxkv3[:, :, nfk:], FOX_HEADS, fox_tk), c_row, c_rep)
        part = compress_partial(P["nsakv"].reshape(B * (T // page), 1, page, 4 * G * HD), pt_prompt, 0, w1cat, False)
        kc_p, vc_p = compress_finish(part, *cmp_args)
        nsakv3 = P["nsakv"].reshape(B, T, -1)
        win3 = P["win"].reshape(B, T, -1)
        ngk = G * HD
        o_nsa_p = nsa_prompt(P["qa"].reshape(B, T, -1), kc_p, jnp.swapaxes(vc_p, 2, 3),
                             nsakv3[:, :, 2 * ngk:3 * ngk].astype(BF16),
                             key_blocks_transposed(nsakv3[:, :, 3 * ngk:], G, LANES),
                             win3[:, :, :ngk].astype(BF16), key_blocks_transposed(win3[:, :, ngk:], G, LANES),
                             small3, b_cmp, b_sel, b_win)
        xp, hp = _tail(xp, o_nsa_p.reshape(B * T, -1), o_fox_p.reshape(B * T, -1), P["gates"],
                       p_prompt[l].reshape(B * T, -1), *tail_w)

        S = _project_all(hs, segs)
        part_s = compress_partial(cache_nsa_kv.reshape(cache_nsa_kv.shape[0], depth, page * 4 * G, HD),
                                  page_table, l, w1cat, True)
        kc_s, vc_s = compress_finish(part_s, *cmp_args)
        q8 = S["qa"].reshape(DB, NSA_HEADS, HD)
        oc_s, idx, n_sel = dec_cmp(q8, kc_s, vc_s, tblt, past)
        o_nsa_s = dec_selwin(q8, S["nsakv"].reshape(DB, 4 * G, HD), S["win"].reshape(DB, 2 * G, HD), S["small"],
                             oc_s, tblt, cache_nsa_kv, state_nsa_window, page_table, idx, n_sel, l, past)
        o_fox_s = dec_fox(S["qf"].reshape(DB, FOX_HEADS, HD), S["foxkv"].reshape(DB, 2 * FOX_HEADS, HD),
                          S["small"], cache_fox_kv, logf_t, page_table, l)
        xs, hs = _tail(xs, o_nsa_s.reshape(DB, -1).astype(BF16), o_fox_s.reshape(DB, -1).astype(BF16), S["gates"],
                       p_sample[l].reshape(DB, -1), *tail_w)

        lf0, lf1 = LOGF_LANE0, LOGF_LANE0 + FOX_HEADS
        outs[0].append(P["nsakv"].reshape(B, T, 4, G, HD))
        outs[1].append(P["win"].reshape(B, T, 2, G, HD)[:, T - min(WINDOW, T):])
        outs[2].append(P["foxkv"].reshape(B, T, 2, FOX_HEADS, HD))
        outs[3].append(P["small"][:, lf0:lf1].reshape(B, T, FOX_HEADS))
        outs[4].append(S["nsakv"].reshape(DB, 1, 4, G, HD))
        win_all = jnp.concatenate([state_nsa_window[l], S["win"].reshape(DB, 1, 2, G, HD)], axis=1)
        outs[5].append(win_all[:, w_buf + 1 - min(WINDOW, w_buf + 1):])
        outs[6].append(S["foxkv"].reshape(DB, 1, 2, FOX_HEADS, HD))
        outs[7].append(S["small"][:, lf0:lf1].reshape(DB, 1, FOX_HEADS))

    return (xp.reshape(B, T, D), xs.reshape(DB, 1, D)) + tuple(jnp.stack(o) for o in outs)
```

```python
import functools
import math

import jax
import jax.numpy as jnp
from jax import lax
from jax.experimental import pallas as pl
from jax.experimental.pallas import tpu as pltpu

F32 = jnp.float32
BF16 = jnp.bfloat16
I32 = jnp.int32

HEAD_DIM = 128
NSA_HEADS = 8
NSA_KV_HEADS = 2
NSA_GROUP = NSA_HEADS // NSA_KV_HEADS
FOX_HEADS = 8
CMP_LEN = 32
CMP_STRIDE = 16
SEL_BLOCK = 64
SEL_TOPN = 16
N_LOCAL_BLOCKS = 2
WINDOW = 512
N_BUCKETS = 32
BUCKET_EXACT = N_BUCKETS // 2
MAX_DISTANCE = 128
EPS = 1e-6
SCALE = HEAD_DIM ** -0.5
NEG_INF = -1e30
FORCE_BONUS = 1e4

LANES = 128
SUBLANES = 8
VMEM_LIMIT_MB = 52

G = NSA_KV_HEADS
R = NSA_GROUP
HD = HEAD_DIM
BR_LANES = 3 * NSA_HEADS
LOGF_LANE0 = BR_LANES
CHUNKS_PER_PAGE = 128 // CMP_STRIDE


def _cparams(sem, vmem_mb=VMEM_LIMIT_MB):
    return pltpu.CompilerParams(dimension_semantics=sem, vmem_limit_bytes=vmem_mb * 1024 * 1024)


def _pick(n, prefs):
    for p in prefs:
        if n % p == 0:
            return p
    return n


def _iota(shape, dim):
    return lax.broadcasted_iota(I32, shape, dim)


def _dot(a, b):
    return jnp.dot(a, b, preferred_element_type=F32)


def _dot_nt(a, b):
    return lax.dot_general(a, b, (((1,), (1,)), ((), ())), preferred_element_type=F32)


def _split3(x):
    hi = x.astype(BF16)
    r1 = x - hi.astype(F32)
    mid = r1.astype(BF16)
    lo = (r1 - mid.astype(F32)).astype(BF16)
    return hi, mid, lo


def _dot_split_lhs(x, m01):
    hi, mid, lo = _split3(x)
    return _dot(hi, m01) + _dot(mid, m01) + _dot(lo, m01)


def _dot_split_rhs(m01, x):
    hi, mid, lo = _split3(x)
    return _dot(m01, hi) + _dot(m01, mid) + _dot(m01, lo)


def _rms(x, gain):
    return x * lax.rsqrt(jnp.mean(x * x, axis=-1, keepdims=True) + EPS) * gain


def _bucket(dist):
    n = jnp.maximum(dist, 0)
    rel = jnp.log(jnp.maximum(n, 1).astype(F32) / float(BUCKET_EXACT)) / math.log(MAX_DISTANCE / BUCKET_EXACT)
    large = jnp.minimum(BUCKET_EXACT + (rel * float(N_BUCKETS - BUCKET_EXACT)).astype(I32), N_BUCKETS - 1)
    return jnp.where(n < BUCKET_EXACT, n, large)


def _lookup(bucket, vals):
    out = vals[N_BUCKETS - 1]
    for k in range(N_BUCKETS - 1):
        out = jnp.where(bucket == k, vals[k], out)
    return out


def _rms_kernel(x_ref, g_ref, o_ref):
    o_ref[...] = _rms(x_ref[...], g_ref[...]).astype(o_ref.dtype)


def rms_bf16(x, gain):
    M, D = x.shape
    tm = _pick(M, (512, 256, 128, 8))
    return pl.pallas_call(
        _rms_kernel,
        grid=(M // tm,),
        in_specs=[pl.BlockSpec((tm, D), lambda i: (i, 0)), pl.BlockSpec((1, D), lambda i: (0, 0))],
        out_specs=pl.BlockSpec((tm, D), lambda i: (i, 0)),
        out_shape=jax.ShapeDtypeStruct((M, D), BF16),
        compiler_params=_cparams(("parallel",)),
        name="rms",
    )(x, gain.reshape(1, D))


def _proj_kernel(h_ref, w_ref, g_ref, b_ref, o_ref, *, modes):
    h = h_ref[...]
    nt = len(modes)
    t = 0
    while t < nt:
        wd = 2 if t + 1 < nt else 1
        acc = _dot(h, w_ref[:, t * LANES:(t + wd) * LANES])
        for u in range(wd):
            c0 = (t + u) * LANES
            v = acc[:, u * LANES:(u + 1) * LANES]
            mode = modes[t + u]
            if mode == "norm":
                v = _rms(v, g_ref[:, c0:c0 + LANES])
            elif mode == "sigmoid":
                v = jax.nn.sigmoid(v)
            elif mode == "small":
                lane = _iota(v.shape, 1)
                z = v + b_ref[:, c0:c0 + LANES]
                logsig = jnp.minimum(z, 0.0) - jnp.log(1.0 + jnp.exp(-jnp.abs(z)))
                v = jnp.where(lane < BR_LANES, jax.nn.sigmoid(v),
                              jnp.where(lane < LOGF_LANE0 + FOX_HEADS, logsig, 0.0))
            o_ref[:, c0:c0 + LANES] = v
        t += wd


def proj(h, w, col0, N, gains, bias, modes, tn):
    M, K = h.shape
    tm = _pick(M, (512, 256, 128, 8))
    assert len(modes) * LANES == tn and N % tn == 0 and col0 % tn == 0
    jb0 = col0 // tn
    return pl.pallas_call(
        functools.partial(_proj_kernel, modes=modes),
        grid=(M // tm, N // tn),
        in_specs=[pl.BlockSpec((tm, K), lambda i, j: (i, 0)),
                  pl.BlockSpec((K, tn), lambda i, j: (0, jb0 + j)),
                  pl.BlockSpec((1, tn), lambda i, j: (0, j)),
                  pl.BlockSpec((1, tn), lambda i, j: (0, j))],
        out_specs=pl.BlockSpec((tm, tn), lambda i, j: (i, j)),
        out_shape=jax.ShapeDtypeStruct((M, N), F32),
        compiler_params=_cparams(("parallel", "arbitrary")),
        name="proj",
    )(h, w, gains, bias)


def _merge_kernel(on_ref, of_ref, ga_ref, gb_ref, wn_ref, wf_ref, o_ref):
    a = _dot(on_ref[...], wn_ref[...])
    b = _dot(of_ref[...], wf_ref[...])
    o_ref[...] = (ga_ref[...] * a + gb_ref[...] * b).astype(o_ref.dtype)


def merge(o_nsa, o_fox, gates, wbn, wbf):
    M, K = o_nsa.shape
    D = wbn.shape[1]
    tm = _pick(M, (512, 256, 128, 8))
    tn = _pick(D, (1024, 512, 256, 128))
    nj = D // tn
    return pl.pallas_call(
        _merge_kernel,
        grid=(M // tm, nj),
        in_specs=[pl.BlockSpec((tm, K), lambda i, j: (i, 0)),
                  pl.BlockSpec((tm, K), lambda i, j: (i, 0)),
                  pl.BlockSpec((tm, tn), lambda i, j: (i, j)),
                  pl.BlockSpec((tm, tn), lambda i, j: (i, j + nj)),
                  pl.BlockSpec((K, tn), lambda i, j: (0, j)),
                  pl.BlockSpec((K, tn), lambda i, j: (0, j))],
        out_specs=pl.BlockSpec((tm, tn), lambda i, j: (i, j)),
        out_shape=jax.ShapeDtypeStruct((M, D), BF16),
        compiler_params=_cparams(("parallel", "arbitrary")),
        name="merge",
    )(o_nsa, o_fox, gates, gates, wbn, wbf)


def _wo_kernel(m_ref, w_ref, x_ref, g_ref, x1_ref, h1_ref):
    x1 = x_ref[...] + _dot(m_ref[...], w_ref[...])
    x1_ref[...] = x1
    h1_ref[...] = _rms(x1, g_ref[...]).astype(h1_ref.dtype)


def out_proj(merged, wo, x, gain1):
    M, D = x.shape
    tm = _pick(M, (256, 128, 8))
    return pl.pallas_call(
        _wo_kernel,
        grid=(M // tm,),
        in_specs=[pl.BlockSpec((tm, D), lambda i: (i, 0)),
                  pl.BlockSpec((D, D), lambda i: (0, 0)),
                  pl.BlockSpec((tm, D), lambda i: (i, 0)),
                  pl.BlockSpec((1, D), lambda i: (0, 0))],
        out_specs=[pl.BlockSpec((tm, D), lambda i: (i, 0)), pl.BlockSpec((tm, D), lambda i: (i, 0))],
        out_shape=[jax.ShapeDtypeStruct((M, D), F32), jax.ShapeDtypeStruct((M, D), BF16)],
        compiler_params=_cparams(("parallel",)),
        name="out_proj",
    )(merged, wo, x, gain1.reshape(1, D))


def _ffn_in_kernel(h_ref, wg_ref, wu_ref, o_ref):
    h = h_ref[...]
    gt = _dot(h, wg_ref[...])
    up = _dot(h, wu_ref[...])
    o_ref[...] = (gt * jax.nn.sigmoid(gt) * up).astype(o_ref.dtype)


def ffn_in(h1, wfi):
    M, D = h1.shape
    F = wfi.shape[1] // 2
    tm = _pick(M, (1024, 512, 256, 128, 8))
    tn = _pick(F, (512, 256, 128))
    nj = F // tn
    return pl.pallas_call(
        _ffn_in_kernel,
        grid=(M // tm, nj),
        in_specs=[pl.BlockSpec((tm, D), lambda i, j: (i, 0)),
                  pl.BlockSpec((D, tn), lambda i, j: (0, j)),
                  pl.BlockSpec((D, tn), lambda i, j: (0, j + nj))],
        out_specs=pl.BlockSpec((tm, tn), lambda i, j: (i, j)),
        out_shape=jax.ShapeDtypeStruct((M, F), BF16),
        compiler_params=_cparams(("parallel", "arbitrary")),
        name="ffn_in",
    )(h1, wfi, wfi)


def _ffn_out_kernel(a_ref, w_ref, x_ref, o_ref):
    o_ref[...] = x_ref[...] + _dot(a_ref[...], w_ref[...])


def ffn_out(act, wfo, x1):
    M, F = act.shape
    D = wfo.shape[1]
    tm = _pick(M, (512, 256, 128, 8))
    tn = _pick(D, (512, 256, 128))
    return pl.pallas_call(
        _ffn_out_kernel,
        grid=(M // tm, D // tn),
        in_specs=[pl.BlockSpec((tm, F), lambda i, j: (i, 0)),
                  pl.BlockSpec((F, tn), lambda i, j: (0, j)),
                  pl.BlockSpec((tm, tn), lambda i, j: (i, j))],
        out_specs=pl.BlockSpec((tm, tn), lambda i, j: (i, j)),
        out_shape=jax.ShapeDtypeStruct((M, D), F32),
        compiler_params=_cparams(("parallel", "arbitrary")),
        name="ffn_out",
    )(act, wfo, x1)


def _ple_kernel(x_ref, g_ref, wpg_ref, pe_ref, wple_ref, gn_ref, o_ref, hn_ref):
    x = x_ref[...]
    h2 = _rms(x, g_ref[...]).astype(BF16)
    gate = jax.nn.sigmoid(_dot(h2, wpg_ref[...]))
    x3 = x + gate * _dot(pe_ref[...].astype(BF16), wple_ref[...])
    o_ref[...] = x3
    hn_ref[...] = _rms(x3, gn_ref[...]).astype(hn_ref.dtype)


def ple_gate(x2, gain2, wpg, pe, wple, gain_next):
    M, D = x2.shape
    Pd = pe.shape[1]
    tm = _pick(M, (256, 128, 8))
    return pl.pallas_call(
        _ple_kernel,
        grid=(M // tm,),
        in_specs=[pl.BlockSpec((tm, D), lambda i: (i, 0)),
                  pl.BlockSpec((1, D), lambda i: (0, 0)),
                  pl.BlockSpec((D, D), lambda i: (0, 0)),
                  pl.BlockSpec((tm, Pd), lambda i: (i, 0)),
                  pl.BlockSpec((Pd, D), lambda i: (0, 0)),
                  pl.BlockSpec((1, D), lambda i: (0, 0))],
        out_specs=[pl.BlockSpec((tm, D), lambda i: (i, 0)), pl.BlockSpec((tm, D), lambda i: (i, 0))],
        out_shape=[jax.ShapeDtypeStruct((M, D), F32), jax.ShapeDtypeStruct((M, D), BF16)],
        compiler_params=_cparams(("parallel",)),
        name="ple_gate",
    )(x2, gain2.reshape(1, D), wpg, pe, wple, gain_next.reshape(1, D))


SEL_BIAS_KINDS = 3
WIN_BIAS_KINDS = WINDOW // LANES + 2


def _bias_sw_kernel(tbl_ref, bs_ref, bw_ref):
    g = pl.program_id(0)
    kk = _iota((LANES, LANES), 0)
    qq = _iota((LANES, LANES), 1)
    bw_ref[0, WIN_BIAS_KINDS - 1] = jnp.full((LANES, R * LANES), NEG_INF, F32)
    for off in range(WIN_BIAS_KINDS - 1):
        d = off * LANES + qq - kk
        bkt = _bucket(d)
        in_window = (d >= 0) & (d < WINDOW)
        for r in range(R):
            cols = slice(r * LANES, (r + 1) * LANES)
            b = _lookup(bkt, [tbl_ref[k, g * R + r] for k in range(N_BUCKETS)])
            bw_ref[0, off, :, cols] = jnp.where(in_window, b, NEG_INF)
            if off < SEL_BIAS_KINDS - 1:
                bs_ref[0, off, :, cols] = b
    for r in range(R):
        bs_ref[0, SEL_BIAS_KINDS - 1, :, r * LANES:(r + 1) * LANES] = jnp.full(
            (LANES, LANES), tbl_ref[N_BUCKETS - 1, g * R + r], F32)


def bias_sel_win(table):
    shapes = [(G, SEL_BIAS_KINDS, LANES, R * LANES), (G, WIN_BIAS_KINDS, LANES, R * LANES)]
    return pl.pallas_call(
        _bias_sw_kernel,
        grid=(G,),
        in_specs=[pl.BlockSpec(memory_space=pltpu.SMEM)],
        out_specs=[pl.BlockSpec((1,) + s[1:], lambda g: (g, 0, 0, 0)) for s in shapes],
        out_shape=[jax.ShapeDtypeStruct(s, F32) for s in shapes],
        compiler_params=_cparams(("arbitrary",)),
        name="bias_sel_win",
    )(table)


def _bias_cmp_kernel(tbl_ref, bc_ref):
    g = pl.program_id(0)
    qi = pl.program_id(1)
    cc = _iota((LANES, LANES), 0)
    qq = _iota((LANES, LANES), 1)
    d = qi * LANES + qq - (cc * CMP_STRIDE + (CMP_LEN - 1))
    bkt = _bucket(d)
    for r in range(R):
        b = _lookup(bkt, [tbl_ref[k, g * R + r] for k in range(N_BUCKETS)])
        bc_ref[0, 0, :, r * LANES:(r + 1) * LANES] = jnp.where(d >= 0, b, NEG_INF)


def bias_cmp(table, n_qblk):
    return pl.pallas_call(
        _bias_cmp_kernel,
        grid=(G, n_qblk),
        in_specs=[pl.BlockSpec(memory_space=pltpu.SMEM)],
        out_specs=pl.BlockSpec((1, 1, LANES, R * LANES), lambda g, i: (g, i, 0, 0)),
        out_shape=jax.ShapeDtypeStruct((G, n_qblk, LANES, R * LANES), F32),
        compiler_params=_cparams(("arbitrary", "arbitrary")),
        name="bias_cmp",
    )(table)


def _cumsum_kernel(x_ref, cr_ref, cp_ref):
    T = x_ref.shape[1]
    ii = _iota((LANES, LANES), 0)
    jj = _iota((LANES, LANES), 1)
    tri = jnp.where(jj <= ii, 1.0, 0.0).astype(BF16)
    carry = jnp.zeros((1, LANES), F32)
    for blk in range(T // LANES):
        sl = slice(blk * LANES, (blk + 1) * LANES)
        c = _dot_split_rhs(tri, x_ref[0, sl, :]) + carry
        cr_ref[0, :, sl] = c.T[LOGF_LANE0:LOGF_LANE0 + FOX_HEADS, :]
        for h in range(FOX_HEADS):
            cp_ref[0, h, sl, :] = jnp.broadcast_to(c[:, LOGF_LANE0 + h:LOGF_LANE0 + h + 1], (LANES, LANES))
        carry = c[LANES - 1:LANES, :]


def cumsum_logf(small):
    B, T, _ = small.shape
    return pl.pallas_call(
        _cumsum_kernel,
        grid=(B,),
        in_specs=[pl.BlockSpec((1, T, LANES), lambda b: (b, 0, 0))],
        out_specs=[pl.BlockSpec((1, FOX_HEADS, T), lambda b: (b, 0, 0)),
                   pl.BlockSpec((1, FOX_HEADS, T, LANES), lambda b: (b, 0, 0, 0))],
        out_shape=[jax.ShapeDtypeStruct((B, FOX_HEADS, T), F32),
                   jax.ShapeDtypeStruct((B, FOX_HEADS, T, LANES), F32)],
        compiler_params=_cparams(("parallel",)),
        name="cumsum_logf",
    )(small)


def key_blocks_transposed(x, n_heads, blk):
    B, T, _ = x.shape
    return x.reshape(B, T // blk, blk, n_heads, HD).transpose(0, 3, 1, 4, 2).astype(BF16)


def _fox_prompt_kernel(q_ref, k_ref, vt_ref, cq_ref, ck_ref, o_ref, *, tk):
    tq = LANES
    qi = pl.program_id(1)
    nh = FOX_HEADS
    qf = q_ref[0]
    qs = [(qf[:, h * HD:(h + 1) * HD] * SCALE).astype(BF16) for h in range(nh)]
    kk = _iota((tk, tq), 0)
    qq = _iota((tk, tq), 1)

    def step(kb, carry, diagonal):
        k0 = pl.multiple_of(kb * tk, tk)
        scores = [_dot_nt(k_ref[0, pl.ds(k0, tk), h * HD:(h + 1) * HD], qs[h]) for h in range(nh)]
        probs = []
        for h in range(nh):
            m, l, _ = carry[h]
            s = scores[h] + (cq_ref[0, h, pl.ds(qi, 1), :] - ck_ref[0, h, pl.ds(k0, tk), :])
            if diagonal:
                s = jnp.where(k0 + kk <= qi * tq + qq, s, NEG_INF)
            m_new = jnp.maximum(m, jnp.max(s, axis=0, keepdims=True))
            alpha = jnp.exp(m - m_new)
            p = jnp.exp(s - m_new)
            probs.append((m_new, alpha, alpha * l + jnp.sum(p, axis=0, keepdims=True), p.astype(BF16)))
        out = []
        for h in range(nh):
            m_new, alpha, l, p = probs[h]
            out.append((m_new, l, alpha * carry[h][2] + _dot(vt_ref[0, h, kb], p)))
        return tuple(out)

    n_full = lax.div(qi * tq, tk)
    n_all = lax.div(qi * tq + tq + tk - 1, tk)
    one = (jnp.full((1, tq), NEG_INF, F32), jnp.zeros((1, tq), F32), jnp.zeros((HD, tq), F32))
    carry = lax.fori_loop(0, n_full, lambda kb, c: step(kb, c, False), (one,) * nh)
    res = lax.fori_loop(n_full, n_all, lambda kb, c: step(kb, c, True), carry)
    for h in range(nh):
        _, l, acc = res[h]
        o_ref[0, :, h * HD:(h + 1) * HD] = (acc / l).T.astype(o_ref.dtype)


def fox_prompt(qf, k_bf, v_t, c_row, c_rep):
    B, T, W = qf.shape
    tq = LANES
    tk = v_t.shape[-1]
    assert T % tk == 0 and tk % tq == 0
    c_row4 = c_row.reshape(B, FOX_HEADS, T // tq, tq)
    return pl.pallas_call(
        functools.partial(_fox_prompt_kernel, tk=tk),
        grid=(B, T // tq),
        in_specs=[pl.BlockSpec((1, tq, W), lambda b, i: (b, i, 0)),
                  pl.BlockSpec((1, T, W), lambda b, i: (b, 0, 0)),
                  pl.BlockSpec((1, FOX_HEADS, T // tk, HD, tk), lambda b, i: (b, 0, 0, 0, 0)),
                  pl.BlockSpec((1, FOX_HEADS, T // tq, tq), lambda b, i: (b, 0, 0, 0)),
                  pl.BlockSpec((1, FOX_HEADS, T, LANES), lambda b, i: (b, 0, 0, 0))],
        out_specs=pl.BlockSpec((1, tq, W), lambda b, i: (b, i, 0)),
        out_shape=jax.ShapeDtypeStruct((B, T, W), BF16),
        compiler_params=_cparams(("parallel", "arbitrary")),
        name="fox_prompt",
    )(qf, k_bf, v_t, c_row4, c_rep)


def _cmp_p_kernel(pt_ref, *refs, npg, row_per_head):
    nref = npg if row_per_head else npg * 2 * G
    page_refs = refs[:nref]
    w_ref, o_ref, lhs_ref = refs[nref:]
    rph = 4 * G
    for i in range(npg):
        for c in range(2 * G):
            for t in range(CMP_STRIDE):
                if row_per_head:
                    rows = page_refs[i][pl.ds(t * rph + c, CHUNKS_PER_PAGE, stride=CMP_STRIDE * rph), :]
                else:
                    rows = page_refs[i * 2 * G + c][pl.ds(t, CHUNKS_PER_PAGE, stride=CMP_STRIDE), :]
                lhs_ref[c, i * CHUNKS_PER_PAGE:(i + 1) * CHUNKS_PER_PAGE, t * HD:(t + 1) * HD] = rows
    for kv in range(2):
        for g in range(G):
            o_ref[0, kv, g] = _dot(lhs_ref[kv * G + g].astype(BF16), w_ref[kv])


def compress_partial(pages, page_table, layer, w1cat, row_per_head):
    NB, n_pages = page_table.shape
    npg = _pick(n_pages, (16, 8, 4, 2, 1))
    rows = npg * CHUNKS_PER_PAGE
    kdim = CMP_STRIDE * HD
    in_specs = []
    for i in range(npg):
        if row_per_head:
            in_specs.append(pl.BlockSpec((None, None, 128 * 4 * G, HD),
                                         lambda b, p, pt, i=i: (pt[b, p * npg + i], layer, 0, 0)))
            continue
        for c in range(2 * G):
            in_specs.append(pl.BlockSpec((None, None, 128, HD),
                                         lambda b, p, pt, i=i, c=c: (pt[b, p * npg + i], layer, 0, c)))
    in_specs.append(pl.BlockSpec((2, kdim, 2 * HD), lambda b, p, pt: (0, 0, 0)))
    return pl.pallas_call(
        functools.partial(_cmp_p_kernel, npg=npg, row_per_head=row_per_head),
        grid_spec=pltpu.PrefetchScalarGridSpec(
            num_scalar_prefetch=1,
            grid=(NB, n_pages // npg),
            in_specs=in_specs,
            out_specs=pl.BlockSpec((1, 2, G, rows, 2 * HD), lambda b, p, pt: (b, 0, 0, p, 0)),
            scratch_shapes=[pltpu.VMEM((2 * G, rows, kdim), F32)]),
        out_shape=jax.ShapeDtypeStruct((NB, 2, G, n_pages * CHUNKS_PER_PAGE, 2 * HD), F32),
        compiler_params=_cparams(("parallel", "arbitrary")),
        name="compress_partial",
    )(page_table, *([pages] * len(in_specs[:-1])), w1cat)


def _gelu_tanh(x):
    return x * (0.5 * (1.0 + jnp.tanh(math.sqrt(2.0 / math.pi) * (x + 0.044715 * (x * x * x)))))


def _cmp_fin_kernel(p_ref, pos_ref, w1_ref, b1_ref, w2_ref, gk_ref, kc_ref, vc_ref):
    NC = p_ref.shape[3]
    row = _iota((NC, HD), 0)
    for kv in range(2):
        w1 = w1_ref[kv]
        posterm = (_dot(pos_ref[kv, 0], w1[:, :HD]) + _dot(pos_ref[kv, 1], w1[:, HD:]))[0:1, :] + b1_ref[kv]
        for g in range(G):
            P = p_ref[0, kv, g]
            pre = P[:, :HD] + pltpu.roll(P[:, HD:], NC - 1, 0) + posterm
            o = _dot(_gelu_tanh(pre).astype(BF16), w2_ref[kv])
            if kv == 0:
                o = _rms(o, gk_ref[...])
            o = jnp.where(row < NC - 1, o, 0.0)
            if kv == 0:
                kc_ref[0, g] = o
            else:
                vc_ref[0, g] = o


def compress_finish(part, pos_ab, w1cat, b1, w2, gain_kc):
    NB, _, _, NC, _ = part.shape
    kdim = CMP_STRIDE * HD
    full = lambda shape: pl.BlockSpec(shape, lambda b: (0,) * len(shape))
    return pl.pallas_call(
        _cmp_fin_kernel,
        grid=(NB,),
        in_specs=[pl.BlockSpec((1, 2, G, NC, 2 * HD), lambda b: (b, 0, 0, 0, 0)),
                  full((2, 2, SUBLANES, kdim)), full((2, kdim, 2 * HD)), full((2, 1, HD)),
                  full((2, HD, HD)), full((1, HD))],
        out_specs=[pl.BlockSpec((1, G, NC, HD), lambda b: (b, 0, 0, 0)),
                   pl.BlockSpec((1, G, NC, HD), lambda b: (b, 0, 0, 0))],
        out_shape=[jax.ShapeDtypeStruct((NB, G, NC, HD), F32), jax.ShapeDtypeStruct((NB, G, NC, HD), F32)],
        compiler_params=_cparams(("parallel",)),
        name="compress_finish",
    )(part, pos_ab, w1cat, b1, w2, gain_kc)


def _sel_overlap_matrix(nc, nsb, n_cb):
    ci = _iota((nc, nsb), 0)
    jb = _iota((nc, nsb), 1)
    ratio = SEL_BLOCK // CMP_STRIDE
    first = ratio * jb - (CMP_LEN // CMP_STRIDE) + 1
    n_ov = (SEL_BLOCK + CMP_LEN) // CMP_STRIDE - 1
    hit = (ci >= first) & (ci < first + n_ov) & (ci < n_cb)
    return jnp.where(hit, 1.0, 0.0).astype(BF16)


def _sel_overlap_matrix_t(nsb, nc, n_cb):
    jb = _iota((nsb, nc), 0)
    ci = _iota((nsb, nc), 1)
    first = (SEL_BLOCK // CMP_STRIDE) * jb - (CMP_LEN // CMP_STRIDE) + 1
    n_ov = (SEL_BLOCK + CMP_LEN) // CMP_STRIDE - 1
    hit = (ci >= first) & (ci < first + n_ov) & (ci < n_cb)
    return jnp.where(hit, 1.0, 0.0).astype(BF16)


def _div_pow2(x, d):
    if isinstance(x, int):
        return x // d
    return lax.shift_right_arithmetic(x, jnp.full(x.shape, d.bit_length() - 1, I32))


def _block_scores(imp_s, qpos, jb, n_sb):
    back = _div_pow2(qpos, SEL_BLOCK) - jb
    forced = (jb == 0) | ((back >= 0) & (back < N_LOCAL_BLOCKS))
    score = jnp.where(back >= 0, imp_s + jnp.where(forced, FORCE_BONUS, 0.0), -1.0)
    return jnp.where(jb < n_sb, score, -2.0)


def _nsa_prompt_kernel(q_ref, kc_ref, vct_ref, ks_ref, vst_ref, kw_ref, vwt_ref, br_ref, bc_ref, bs_ref, bw_ref,
                       exp_ref, o_ref, msk_ref, oc_ref, brt_ref, *, n_cb, n_sb, n_sel):
    g = pl.program_id(1)
    qi = pl.program_id(2)
    tq = LANES
    qf = q_ref[0]
    qt = jnp.concatenate([(qf[:, r * HD:(r + 1) * HD] * SCALE).T for r in range(R)], axis=1).astype(BF16)
    qpos = qi * tq + _iota((1, tq), 1)
    heads = lambda x: jnp.concatenate([x] * R, axis=1)

    has_c = heads(qpos >= CMP_LEN - 1)
    s = _dot(kc_ref[0, 0].astype(BF16), qt) + bc_ref[0, 0]
    e = jnp.exp(s - jnp.max(s, axis=0, keepdims=True))
    p = e * jnp.where(has_c, 1.0 / jnp.sum(e, axis=0, keepdims=True), 0.0)
    oc_ref[...] = _dot(vct_ref[0, 0].astype(BF16), p.astype(BF16))
    imp_c = p[:, 0:tq]
    for r in range(1, R):
        imp_c = imp_c + p[:, r * tq:(r + 1) * tq]

    nsb8 = -(-n_sb // SUBLANES) * SUBLANES
    imp_s = _dot_split_rhs(_sel_overlap_matrix_t(LANES, LANES, n_cb), imp_c)[0:nsb8]
    jb = _iota((nsb8, tq), 0)
    score = _block_scores(imp_s, qpos, jb, n_sb)
    rank = jnp.zeros((nsb8, tq), I32)
    for i in range(n_sb):
        row = score[i:i + 1, :]
        rank = rank + jnp.where(row > score, 1, jnp.where((row == score) & (jb > i), 1, 0))
    sel = jnp.where(rank < n_sel, 1.0, 0.0)
    if nsb8 < LANES:
        sel = jnp.concatenate([sel, jnp.zeros((LANES - nsb8, tq), F32)], axis=0)
    sel = sel.astype(BF16)

    n_keys = msk_ref.shape[0]
    chosen = _dot(exp_ref[...], sel)
    key_pos = _iota((n_keys, tq), 0)
    msk_ref[...] = jnp.where((chosen > 0.5) & (qi * tq + _iota((n_keys, tq), 1) >= key_pos), 0.0, NEG_INF)

    def branch(k_ref, vt_ref, b_ref, kind_of, masked):
        def body(j, carry):
            m, l, acc = carry
            k0 = pl.multiple_of(j * (2 * LANES), 2 * LANES)
            off = qi - 2 * j
            bias = jnp.concatenate([b_ref[0, kind_of(off)], b_ref[0, kind_of(off - 1)]], axis=0)
            s = _dot(k_ref[0, pl.ds(k0, 2 * LANES), :], qt) + bias
            if masked:
                s = s + heads(msk_ref[pl.ds(k0, 2 * LANES), :])
            m_new = jnp.maximum(m, jnp.max(s, axis=0, keepdims=True))
            alpha = jnp.exp(m - m_new)
            p = jnp.exp(s - m_new)
            l = alpha * l + jnp.sum(p, axis=0, keepdims=True)
            vblk = jnp.concatenate([vt_ref[0, 0, 2 * j], vt_ref[0, 0, 2 * j + 1]], axis=1)
            acc = alpha * acc + _dot(vblk, p.astype(BF16))
            return m_new, l, acc

        return body

    n_win = WINDOW // LANES
    sel_trip = branch(ks_ref, vst_ref, bs_ref, lambda off: jnp.clip(off, 0, SEL_BIAS_KINDS - 1), True)
    win_trip = branch(kw_ref, vwt_ref, bw_ref, lambda off: jnp.where((off < 0) | (off > n_win), n_win + 1, off),
                      False)
    init = (jnp.full((1, R * tq), NEG_INF, F32), jnp.zeros((1, R * tq), F32), jnp.zeros((HD, R * tq), F32))
    first_win = jnp.maximum(qi - n_win, 0) // 2
    sel_c = lax.fori_loop(0, first_win, sel_trip, init)
    sel_c, win_c = lax.fori_loop(first_win, qi // 2 + 1,
                                 lambda j, c: (sel_trip(j, c[0]), win_trip(j, c[1])), (sel_c, init))
    o_s = sel_c[2] * (1.0 / sel_c[1])
    o_w = win_c[2] * (1.0 / win_c[1])
    brt_ref[...] = br_ref[0].T
    gate = lambda i: jnp.concatenate([brt_ref[pl.ds((g * R + r) * 3 + i, 1), :] for r in range(R)], axis=1)
    o = gate(0) * oc_ref[...] + gate(1) * o_s + gate(2) * o_w
    for r in range(R):
        o_ref[0, :, r * HD:(r + 1) * HD] = o[:, r * tq:(r + 1) * tq].T.astype(o_ref.dtype)


def nsa_prompt(qa, kc, vc_t, ks_bf, vs_t, kw_bf, vw_t, small, b_cmp, b_sel, b_win):
    B, T, _ = qa.shape
    assert T % LANES == 0 and T // CMP_STRIDE == LANES and kc.shape[2] == LANES
    n_cb = T // CMP_STRIDE - CMP_LEN // CMP_STRIDE + 1
    n_sb = -(-T // SEL_BLOCK)
    n_sel = min(SEL_TOPN, n_sb)
    tq = LANES
    nkb = T // LANES
    assert nkb % 2 == 0 and n_sb <= LANES
    grp = lambda b, g, i: (b, g, 0, 0)
    expand = (jnp.arange(T, dtype=I32)[:, None] // SEL_BLOCK == jnp.arange(LANES, dtype=I32)[None, :]).astype(BF16)
    return pl.pallas_call(
        functools.partial(_nsa_prompt_kernel, n_cb=n_cb, n_sb=n_sb, n_sel=n_sel),
        grid=(B, G, T // tq),
        in_specs=[pl.BlockSpec((1, tq, R * HD), lambda b, g, i: (b, i, g)),
                  pl.BlockSpec((1, 1, LANES, HD), grp),
                  pl.BlockSpec((1, 1, HD, LANES), grp),
                  pl.BlockSpec((1, T, HD), lambda b, g, i: (b, 0, g)),
                  pl.BlockSpec((1, 1, nkb, HD, LANES), lambda b, g, i: (b, g, 0, 0, 0)),
                  pl.BlockSpec((1, T, HD), lambda b, g, i: (b, 0, g)),
                  pl.BlockSpec((1, 1, nkb, HD, LANES), lambda b, g, i: (b, g, 0, 0, 0)),
                  pl.BlockSpec((1, tq, LANES), lambda b, g, i: (b, i, 0)),
                  pl.BlockSpec((1, 1, LANES, R * LANES), lambda b, g, i: (g, i, 0, 0)),
                  pl.BlockSpec((1, SEL_BIAS_KINDS, LANES, R * LANES), lambda b, g, i: (g, 0, 0, 0)),
                  pl.BlockSpec((1, WIN_BIAS_KINDS, LANES, R * LANES), lambda b, g, i: (g, 0, 0, 0)),
                  pl.BlockSpec((T, LANES), lambda b, g, i: (0, 0))],
        out_specs=pl.BlockSpec((1, tq, R * HD), lambda b, g, i: (b, i, g)),
        out_shape=jax.ShapeDtypeStruct((B, T, NSA_HEADS * HD), BF16),
        scratch_shapes=[pltpu.VMEM((T, tq), F32), pltpu.VMEM((HD, R * tq), F32),
                        pltpu.VMEM((LANES, tq), F32)],
        compiler_params=_cparams(("parallel", "parallel", "arbitrary")),
        name="nsa_prompt",
    )(qa, kc, vc_t, ks_bf, vs_t, kw_bf, vw_t, small, b_cmp, b_sel, b_win, expand)


def _dec_cmp_kernel(q_ref, kc_ref, vc_ref, tblt_ref, oc_ref, idx_ref, *, qpos, n_cb, n_sb, n_sel, nsbp):
    NC = kc_ref.shape[2]
    q8 = q_ref[0].astype(BF16)
    row = _iota((NSA_HEADS, NC), 0)
    j = _iota((NSA_HEADS, NC), 1)
    s = jnp.where(row < R, _dot_nt(q8, kc_ref[0, 0].astype(BF16)), _dot_nt(q8, kc_ref[0, 1].astype(BF16)))
    dist = qpos - (j * CMP_STRIDE + (CMP_LEN - 1))
    valid = dist >= 0
    tblt = tblt_ref[...]
    bias = _lookup(_bucket(dist), [tblt[:, k:k + 1] for k in range(N_BUCKETS)])
    s = jnp.where(valid, s * SCALE + bias, NEG_INF)
    m = jnp.max(s, axis=1, keepdims=True)
    e = jnp.where(valid, jnp.exp(s - m), 0.0)
    l = jnp.sum(e, axis=1, keepdims=True)
    p = e / jnp.where(l > 0.0, l, 1.0)
    pb = p.astype(BF16)
    row_o = _iota((NSA_HEADS, HD), 0)
    oc_ref[0] = jnp.where(row_o < R, _dot(pb, vc_ref[0, 0].astype(BF16)), _dot(pb, vc_ref[0, 1].astype(BF16)))

    imp0 = jnp.sum(jnp.where(row < R, p, 0.0), axis=0, keepdims=True)
    imp1 = jnp.sum(jnp.where(row >= R, p, 0.0), axis=0, keepdims=True)
    imp = jnp.where(row == 0, imp0, jnp.where(row == 1, imp1, 0.0))
    imp_s = _dot_split_lhs(imp, _sel_overlap_matrix(NC, nsbp, n_cb))
    jb = _iota((NSA_HEADS, nsbp), 1)
    score = _block_scores(imp_s, qpos, jb, n_sb)
    jbf = jb.astype(F32)
    lane_o = _iota((NSA_HEADS, LANES), 1)
    out = jnp.zeros((NSA_HEADS, LANES), I32)
    for n in range(n_sel):
        mx = jnp.max(score, axis=1, keepdims=True)
        am = jnp.min(jnp.where(score == mx, jbf, float(nsbp)), axis=1, keepdims=True)
        out = jnp.where(lane_o == n, am.astype(I32), out)
        score = jnp.where(jbf == am, -3.0, score)
    idx_ref[0] = out


def dec_cmp(q8, kc, vc, tblt, qpos):
    DB = q8.shape[0]
    NC = kc.shape[2]
    L = qpos + 1
    n_cb = L // CMP_STRIDE - CMP_LEN // CMP_STRIDE + 1
    assert n_cb == NC - 1
    n_sb = -(-L // SEL_BLOCK)
    n_sel = min(SEL_TOPN, n_sb)
    nsbp = -(-n_sb // LANES) * LANES
    oc, idx = pl.pallas_call(
        functools.partial(_dec_cmp_kernel, qpos=qpos, n_cb=n_cb, n_sb=n_sb, n_sel=n_sel, nsbp=nsbp),
        grid=(DB,),
        in_specs=[pl.BlockSpec((1, NSA_HEADS, HD), lambda b: (b, 0, 0)),
                  pl.BlockSpec((1, G, NC, HD), lambda b: (b, 0, 0, 0)),
                  pl.BlockSpec((1, G, NC, HD), lambda b: (b, 0, 0, 0)),
                  pl.BlockSpec((NSA_HEADS, N_BUCKETS), lambda b: (0, 0))],
        out_specs=[pl.BlockSpec((1, NSA_HEADS, HD), lambda b: (b, 0, 0)),
                   pl.BlockSpec((1, NSA_HEADS, LANES), lambda b: (b, 0, 0))],
        out_shape=[jax.ShapeDtypeStruct((DB, NSA_HEADS, HD), F32), jax.ShapeDtypeStruct((DB, NSA_HEADS, LANES), I32)],
        compiler_params=_cparams(("parallel",)),
        name="dec_cmp",
    )(q8, kc, vc, tblt)
    return oc, idx[:, :G, :n_sel].reshape(DB, G * n_sel), n_sel


def _softmax_with_new(s, valid, s_new, new_ok):
    s = jnp.where(valid, s, NEG_INF)
    s_new = jnp.where(new_ok, s_new, NEG_INF)
    m = jnp.maximum(jnp.max(s, axis=1, keepdims=True), s_new)
    e = jnp.where(valid, jnp.exp(s - m), 0.0)
    en = jnp.where(new_ok, jnp.exp(s_new - m), 0.0)
    l = jnp.sum(e, axis=1, keepdims=True) + en
    inv = 1.0 / jnp.where(l > 0.0, l, 1.0)
    return e, en, inv


def _dec_selwin_kernel(pt_ref, idx_ref, *refs, n_sel, qpos, jb_new, w_buf):
    blk_refs = refs[:G * n_sel]
    q_ref, new_ref, win_ref, wnew_ref, br_ref, oc_ref, tblt_ref, o_ref = refs[G * n_sel:]
    rph = 4 * G
    b = pl.program_id(0)
    q8 = q_ref[0]
    row8 = _iota((NSA_HEADS, HD), 0)
    tblt = tblt_ref[...]
    tcols = [tblt[:, k:k + 1] for k in range(N_BUCKETS)]
    new8 = new_ref[0]
    wnew = wnew_ref[0]
    brrow = br_ref[0]
    oc = oc_ref[0]
    nk = n_sel * SEL_BLOCK
    lane = _iota((NSA_HEADS, nk), 1)
    outs = []
    for g in range(G):
        qg = jnp.where(_div_pow2(row8, R) == g, q8, 0.0)
        qb = qg.astype(BF16)
        qr = qb.astype(F32)
        kmat = jnp.concatenate([blk_refs[g * n_sel + n][pl.ds(2 * G + g, SEL_BLOCK, stride=rph), :]
                                for n in range(n_sel)], axis=0).astype(BF16)
        vmat = jnp.concatenate([blk_refs[g * n_sel + n][pl.ds(3 * G + g, SEL_BLOCK, stride=rph), :]
                                for n in range(n_sel)], axis=0).astype(BF16)
        idxv = jnp.zeros((NSA_HEADS, nk), I32)
        has_new = jnp.zeros((), jnp.bool_)
        for n in range(n_sel):
            sidx = idx_ref[b, g * n_sel + n]
            idxv = jnp.where(_div_pow2(lane, SEL_BLOCK) == n, sidx, idxv)
            has_new = has_new | (sidx == jb_new)
        dist = qpos - (idxv * SEL_BLOCK + (lane & (SEL_BLOCK - 1)))
        valid = (idxv < jb_new) & (dist >= 0)
        s = _dot_nt(qb, kmat) * SCALE + _lookup(_bucket(dist), tcols)
        kn = new8[2 * G + g:2 * G + g + 1, :].astype(BF16).astype(F32)
        vn = new8[3 * G + g:3 * G + g + 1, :].astype(BF16).astype(F32)
        s_new = jnp.sum(qr * kn, axis=1, keepdims=True) * SCALE + tcols[0]
        e, en, inv = _softmax_with_new(s, valid, s_new, has_new)
        o_s = (_dot(e.astype(BF16), vmat) + en * vn) * inv

        kwin = win_ref[pl.ds(g, w_buf, stride=2 * G), :].astype(BF16)
        vwin = win_ref[pl.ds(G + g, w_buf, stride=2 * G), :].astype(BF16)
        lw = _iota((NSA_HEADS, w_buf), 1)
        dist_w = w_buf - lw
        valid_w = (dist_w < WINDOW) & (qpos - dist_w >= 0)
        sw = _dot_nt(qb, kwin) * SCALE + _lookup(_bucket(dist_w), tcols)
        kwn = wnew[g:g + 1, :].astype(BF16).astype(F32)
        vwn = wnew[G + g:G + g + 1, :].astype(BF16).astype(F32)
        sw_new = jnp.sum(qr * kwn, axis=1, keepdims=True) * SCALE + tcols[0]
        ew, ewn, invw = _softmax_with_new(sw, valid_w, sw_new, jnp.ones((), jnp.bool_))
        o_w = (_dot(ew.astype(BF16), vwin) + ewn * vwn) * invw

        def bcol(i):
            return jnp.sum(jnp.where(_iota((NSA_HEADS, LANES), 1) == row8 * 3 + i, brrow, 0.0), axis=1, keepdims=True)

        outs.append(bcol(0) * oc + bcol(1) * o_s + bcol(2) * o_w)
    o_ref[0] = jnp.where(row8 < R, outs[0], outs[1])


def dec_selwin(q8, new8, wnew4, small_s, oc, tblt, cache_nsa_kv, state_win, page_table, idx, n_sel, layer, qpos):
    DB = q8.shape[0]
    n_pool, depth = cache_nsa_kv.shape[:2]
    page = cache_nsa_kv.shape[2]
    halves = page // SEL_BLOCK
    w_buf = state_win.shape[2]
    n_pages = page_table.shape[1]
    jb_new = qpos // SEL_BLOCK
    rph = 4 * G
    cache_rows = cache_nsa_kv.reshape(n_pool, depth, page * rph, HD)
    state_rows = state_win.reshape(depth, DB, w_buf * 2 * G, HD)
    in_specs = []
    args = []
    for g in range(G):
        for n in range(n_sel):
            def imap(b, pt, ix, g=g, n=n):
                jb = jnp.clip(ix[b, g * n_sel + n], 0, jb_new - 1)
                return (pt[b, jb // halves], layer, jb % halves, 0)
            in_specs.append(pl.BlockSpec((None, None, SEL_BLOCK * rph, HD), imap))
            args.append(cache_rows)
    row3 = lambda b, pt, ix: (b, 0, 0)
    in_specs += [pl.BlockSpec((1, NSA_HEADS, HD), row3), pl.BlockSpec((1, 4 * G, HD), row3),
                 pl.BlockSpec((None, None, w_buf * 2 * G, HD), lambda b, pt, ix: (layer, b, 0, 0)),
                 pl.BlockSpec((1, 2 * G, HD), row3), pl.BlockSpec((1, 1, LANES), row3),
                 pl.BlockSpec((1, NSA_HEADS, HD), row3),
                 pl.BlockSpec((NSA_HEADS, N_BUCKETS), lambda b, pt, ix: (0, 0))]
    args += [q8, new8, state_rows, wnew4, small_s.reshape(DB, 1, LANES), oc, tblt]
    return pl.pallas_call(
        functools.partial(_dec_selwin_kernel, n_sel=n_sel, qpos=qpos, jb_new=jb_new, w_buf=w_buf),
        grid_spec=pltpu.PrefetchScalarGridSpec(
            num_scalar_prefetch=2,
            grid=(DB,),
            in_specs=in_specs,
            out_specs=pl.BlockSpec((1, NSA_HEADS, HD), row3)),
        out_shape=jax.ShapeDtypeStruct((DB, NSA_HEADS, HD), F32),
        compiler_params=_cparams(("arbitrary",)),
        name="dec_selwin",
    )(page_table, idx, *args)


def _dec_fox_kernel(pt_ref, *refs, npg):
    kv_refs = refs[:npg]
    lf_refs = refs[npg:2 * npg]
    q_ref, new_ref, lfn_ref, o_ref, m_sc, l_sc, acc_sc, car_sc = refs[2 * npg:]
    p = pl.program_id(1)
    nh = FOX_HEADS
    page = kv_refs[0].shape[0]
    q8 = q_ref[0]
    qs = q8 * SCALE

    @pl.when(p == 0)
    def _():
        new = new_ref[0]
        s_new = jnp.sum(qs * new[0:nh], axis=1, keepdims=True)
        m_sc[...] = jnp.broadcast_to(s_new, (nh, LANES))
        l_sc[...] = jnp.ones((nh, LANES), F32)
        acc_sc[...] = new[nh:2 * nh]
        lane1 = _iota((nh, LANES), 1)
        row1 = _iota((nh, LANES), 0)
        car_sc[...] = jnp.sum(jnp.where(lane1 == LOGF_LANE0 + row1, lfn_ref[0], 0.0), axis=1, keepdims=True)

    def tree(x, op):
        while x.shape[0] > 1:
            half = x.shape[0] // 2
            x = op(x[:half], x[half:])
        return x[0]

    uu = _iota((LANES, LANES), 0)
    ss = _iota((LANES, LANES), 1)
    later = jnp.where(uu > ss, 1.0, 0.0).astype(BF16)
    ones = jnp.ones((HD, LANES), BF16)
    tok3 = _iota((page, nh, LANES), 0)
    lane3 = _iota((page, nh, LANES), 2)
    m = m_sc[...]
    l = l_sc[...]
    acc = acc_sc[...]
    car = car_sc[...]
    for i in range(npg):
        lft = lf_refs[i][...]
        decay = _dot_split_lhs(lft, later) + car
        lhs = jnp.where(lane3 == tok3, decay[None], 0.0) + kv_refs[i][:, 0] * qs[None]
        s = _dot(lhs.reshape(page * nh, HD).astype(BF16), ones).reshape(page, nh, LANES)
        m_new = jnp.maximum(m, tree(s, jnp.maximum))
        alpha = jnp.exp(m - m_new)
        pp = jnp.exp(s - m_new[None])
        l = alpha * l + tree(pp, jnp.add)
        acc = alpha * acc + tree(pp * kv_refs[i][:, 1], jnp.add)
        m = m_new
        car = car + jnp.sum(lft, axis=1, keepdims=True)
    m_sc[...] = m
    l_sc[...] = l
    acc_sc[...] = acc
    car_sc[...] = car

    @pl.when(p == pl.num_programs(1) - 1)
    def _():
        o_ref[0] = acc / l


def dec_fox(q8, new16, small_s, cache_fox_kv, logf_t, page_table, layer):
    DB = q8.shape[0]
    n_pool, depth, page = cache_fox_kv.shape[:3]
    assert page == LANES
    n_pages = page_table.shape[1]
    npg = _pick(n_pages, (8, 4, 2, 1))
    in_specs = []
    for i in range(npg):
        in_specs.append(pl.BlockSpec((None, None, page, 2, FOX_HEADS, HD),
                                     lambda b, p, pt, i=i: (pt[b, n_pages - 1 - (p * npg + i)], layer, 0, 0, 0, 0)))
    for i in range(npg):
        in_specs.append(pl.BlockSpec((None, None, FOX_HEADS, page),
                                     lambda b, p, pt, i=i: (pt[b, n_pages - 1 - (p * npg + i)], layer, 0, 0)))
    row3 = lambda b, p, pt: (b, 0, 0)
    in_specs += [pl.BlockSpec((1, FOX_HEADS, HD), row3), pl.BlockSpec((1, 2 * FOX_HEADS, HD), row3),
                 pl.BlockSpec((1, 1, LANES), row3)]
    return pl.pallas_call(
        functools.partial(_dec_fox_kernel, npg=npg),
        grid_spec=pltpu.PrefetchScalarGridSpec(
            num_scalar_prefetch=1,
            grid=(DB, n_pages // npg),
            in_specs=in_specs,
            out_specs=pl.BlockSpec((1, FOX_HEADS, HD), row3),
            scratch_shapes=[pltpu.VMEM((FOX_HEADS, LANES), F32), pltpu.VMEM((FOX_HEADS, LANES), F32),
                            pltpu.VMEM((FOX_HEADS, HD), F32), pltpu.VMEM((FOX_HEADS, 1), F32)]),
        out_shape=jax.ShapeDtypeStruct((DB, FOX_HEADS, HD), F32),
        compiler_params=_cparams(("parallel", "arbitrary")),
        name="dec_fox",
    )(page_table, *([cache_fox_kv] * npg), *([logf_t] * npg), q8, new16, small_s.reshape(DB, 1, LANES))


def _layer_weights(l, D, w_in, b_forget, qk_gain_nsa, qk_gain_fox):
    nq = NSA_HEADS * HD
    nkv = 6 * G * HD
    nfox = 3 * FOX_HEADS * HD
    o_br = nq + nkv
    o_fox = o_br + BR_LANES
    o_f = o_fox + nfox
    o_mg = o_f + FOX_HEADS
    w = w_in[l]
    ones = jnp.ones((HD,), F32)
    gn = qk_gain_nsa[l]
    gf = qk_gain_fox[l]

    def gains(rows):
        return jnp.concatenate(rows).reshape(1, -1)

    zeros = lambda n: jnp.zeros((1, n), F32)
    small_b = jnp.concatenate([jnp.zeros((BR_LANES,), F32), b_forget[l],
                               jnp.zeros((LANES - BR_LANES - FOX_HEADS,), F32)]).reshape(1, LANES)
    kvh = G
    tg = _pick(2 * D, (1024, 512, 256, 128))
    table = [
        ("foxkv", [(o_fox + FOX_HEADS * HD, o_f)], gains([gf[1]] * FOX_HEADS + [ones] * FOX_HEADS),
         zeros(2 * FOX_HEADS * HD), ("norm",) * FOX_HEADS + ("raw",) * FOX_HEADS, 2 * FOX_HEADS * HD),
        ("qa", [(0, nq)], gains([gn[0]] * NSA_HEADS), zeros(nq), ("norm",) * NSA_HEADS, nq),
        ("nsakv", [(nq, nq + 4 * kvh * HD)], gains([ones] * (2 * kvh) + [gn[2]] * kvh + [ones] * kvh),
         zeros(4 * kvh * HD), ("raw",) * (2 * kvh) + ("norm",) * kvh + ("raw",) * kvh, 4 * kvh * HD),
        ("qf", [(o_fox, o_fox + FOX_HEADS * HD)], gains([gf[0]] * FOX_HEADS), zeros(FOX_HEADS * HD),
         ("norm",) * FOX_HEADS, FOX_HEADS * HD),
        ("gates", [(o_mg, o_mg + 2 * D)], zeros(2 * D), zeros(2 * D), ("sigmoid",) * (tg // LANES), tg),
        ("win", [(nq + 4 * kvh * HD, o_br)], gains([gn[3]] * kvh + [ones] * kvh), zeros(2 * kvh * HD),
         ("norm",) * kvh + ("raw",) * kvh, 2 * kvh * HD),
        ("small", [(o_br, o_br + BR_LANES), (o_f, o_f + FOX_HEADS)], zeros(LANES), small_b, ("small",), LANES),
    ]
    pieces, segs, col = [], {}, 0
    for name, spans, gn_row, bias_row, modes, tn in table:
        width = gn_row.shape[1]
        got = 0
        for lo, hi in spans:
            pieces.append(w[:, lo:hi])
            got += hi - lo
        if got < width:
            pieces.append(jnp.zeros((D, width - got), F32))
        assert col % tn == 0
        segs[name] = (col, width, gn_row, bias_row, modes, tn)
        col += width
    return jnp.concatenate(pieces, axis=1).astype(BF16), segs


def _project_all(h, w_perm, segs):
    return {name: proj(h, w_perm, col0, n, gn, bs, modes, tn) for name, (col0, n, gn, bs, modes, tn) in segs.items()}


def _tail(x, o_nsa, o_fox, gates, pe, gains_l, gain_next, wbn, wbf, wo, wfi, wfo, wple, wpg):
    merged = merge(o_nsa, o_fox, gates, wbn, wbf)
    x1, h1 = out_proj(merged, wo, x, gains_l[1])
    act = ffn_in(h1, wfi)
    x2 = ffn_out(act, wfo, x1)
    return ple_gate(x2, gains_l[2], wpg, pe, wple, gain_next)


def kernel(x_prompt, x_sample, cache_nsa_kv, cache_fox_kv, cache_fox_logf, state_nsa_window, page_table, p_prompt, p_sample, rel_bias_table, norm_gains, w_in, b_forget, qk_gain_nsa, qk_gain_fox, cmp_pos, w_cmp1, b_cmp1, w_cmp2, w_branch_nsa, w_branch_fox, w_out, w_ffn_in, w_ffn_out, w_ple, w_ple_gate):
    B, T, D = x_prompt.shape
    DB, n_qs, _ = x_sample.shape
    assert n_qs == 1
    depth = w_in.shape[0]
    page = cache_nsa_kv.shape[2]
    n_pages = page_table.shape[1]
    past = n_pages * page
    w_buf = state_nsa_window.shape[2]
    kdim = CMP_STRIDE * HD

    xp = x_prompt.reshape(B * T, D)
    xs = x_sample.reshape(DB, D)
    b_sel, b_win = bias_sel_win(rel_bias_table)
    b_cmp = bias_cmp(rel_bias_table, T // LANES)
    fox_tk = _pick(T, (256, 128))
    tblt = rel_bias_table.T
    logf_t = jnp.swapaxes(cache_fox_logf, 2, 3)
    pt_prompt = jnp.arange(B * (T // page), dtype=I32).reshape(B, T // page)
    hp = rms_bf16(xp, norm_gains[0, 0])
    hs = rms_bf16(xs, norm_gains[0, 0])

    outs = [[] for _ in range(8)]
    for l in range(depth):
        w_perm, segs = _layer_weights(l, D, w_in, b_forget, qk_gain_nsa, qk_gain_fox)
        w1 = w_cmp1[l]
        w1cat = jnp.concatenate([w1[:, :kdim], w1[:, kdim:]], axis=2).astype(BF16)
        pos = cmp_pos[l].reshape(2, 2, 1, kdim)
        pos_ab = jnp.broadcast_to(pos, (2, 2, SUBLANES, kdim)).astype(BF16)
        cmp_args = (pos_ab, w1cat, b_cmp1[l].reshape(2, 1, HD), w_cmp2[l].astype(BF16),
                    qk_gain_nsa[l, 1].reshape(1, HD))
        gain_next = norm_gains[l + 1, 0] if l + 1 < depth else norm_gains[l, 0]
        tail_w = (norm_gains[l], gain_next, w_branch_nsa[l].astype(BF16), w_branch_fox[l].astype(BF16),
                  w_out[l].astype(BF16), w_ffn_in[l].astype(BF16), w_ffn_out[l].astype(BF16),
                  w_ple[l].astype(BF16), w_ple_gate[l].astype(BF16))

        P = _project_all(hp, w_perm, segs)
        small3 = P["small"].reshape(B, T, LANES)
        c_row, c_rep = cumsum_logf(small3)
        foxkv3 = P["foxkv"].reshape(B, T, -1)
        nfk = FOX_HEADS * HD
        o_fox_p = fox_prompt(P["qf"].reshape(B, T, -1), foxkv3[:, :, :nfk].astype(BF16),
                             key_blocks_transposed(foxkv3[:, :, nfk:], FOX_HEADS, fox_tk), c_row, c_rep)
        part = compress_partial(P["nsakv"].reshape(B * (T // page), 1, page, 4 * G * HD), pt_prompt, 0, w1cat, False)
        kc_p, vc_p = compress_finish(part, *cmp_args)
        nsakv3 = P["nsakv"].reshape(B, T, -1)
        win3 = P["win"].reshape(B, T, -1)
        ngk = G * HD
        o_nsa_p = nsa_prompt(P["qa"].reshape(B, T, -1), kc_p, jnp.swapaxes(vc_p, 2, 3),
                             nsakv3[:, :, 2 * ngk:3 * ngk].astype(BF16),
                             key_blocks_transposed(nsakv3[:, :, 3 * ngk:], G, LANES),
                             win3[:, :, :ngk].astype(BF16), key_blocks_transposed(win3[:, :, ngk:], G, LANES),
                             small3, b_cmp, b_sel, b_win)
        xp, hp = _tail(xp, o_nsa_p.reshape(B * T, -1), o_fox_p.reshape(B * T, -1), P["gates"],
                       p_prompt[l].reshape(B * T, -1), *tail_w)

        S = _project_all(hs, w_perm, segs)
        part_s = compress_partial(cache_nsa_kv.reshape(cache_nsa_kv.shape[0], depth, page * 4 * G, HD),
                                  page_table, l, w1cat, True)
        kc_s, vc_s = compress_finish(part_s, *cmp_args)
        q8 = S["qa"].reshape(DB, NSA_HEADS, HD)
        oc_s, idx, n_sel = dec_cmp(q8, kc_s, vc_s, tblt, past)
        o_nsa_s = dec_selwin(q8, S["nsakv"].reshape(DB, 4 * G, HD), S["win"].reshape(DB, 2 * G, HD), S["small"],
                             oc_s, tblt, cache_nsa_kv, state_nsa_window, page_table, idx, n_sel, l, past)
        o_fox_s = dec_fox(S["qf"].reshape(DB, FOX_HEADS, HD), S["foxkv"].reshape(DB, 2 * FOX_HEADS, HD),
                          S["small"], cache_fox_kv, logf_t, page_table, l)
        xs, hs = _tail(xs, o_nsa_s.reshape(DB, -1).astype(BF16), o_fox_s.reshape(DB, -1).astype(BF16), S["gates"],
                       p_sample[l].reshape(DB, -1), *tail_w)

        lf0, lf1 = LOGF_LANE0, LOGF_LANE0 + FOX_HEADS
        outs[0].append(P["nsakv"].reshape(B, T, 4, G, HD))
        outs[1].append(P["win"].reshape(B, T, 2, G, HD)[:, T - min(WINDOW, T):])
        outs[2].append(P["foxkv"].reshape(B, T, 2, FOX_HEADS, HD))
        outs[3].append(P["small"][:, lf0:lf1].reshape(B, T, FOX_HEADS))
        outs[4].append(S["nsakv"].reshape(DB, 1, 4, G, HD))
        win_all = jnp.concatenate([state_nsa_window[l], S["win"].reshape(DB, 1, 2, G, HD)], axis=1)
        outs[5].append(win_all[:, w_buf + 1 - min(WINDOW, w_buf + 1):])
        outs[6].append(S["foxkv"].reshape(DB, 1, 2, FOX_HEADS, HD))
        outs[7].append(S["small"][:, lf0:lf1].reshape(DB, 1, FOX_HEADS))

    return (xp.reshape(B, T, D), xs.reshape(DB, 1, D)) + tuple(jnp.stack(o) for o in outs)
```

```python
import functools
import math

import jax
import jax.numpy as jnp
from jax import lax
from jax.experimental import pallas as pl
from jax.experimental.pallas import tpu as pltpu

F32 = jnp.float32
BF16 = jnp.bfloat16
I32 = jnp.int32

HEAD_DIM = 128
NSA_HEADS = 8
NSA_KV_HEADS = 2
NSA_GROUP = NSA_HEADS // NSA_KV_HEADS
FOX_HEADS = 8
CMP_LEN = 32
CMP_STRIDE = 16
SEL_BLOCK = 64
SEL_TOPN = 16
N_LOCAL_BLOCKS = 2
WINDOW = 512
N_BUCKETS = 32
BUCKET_EXACT = N_BUCKETS // 2
MAX_DISTANCE = 128
EPS = 1e-6
SCALE = HEAD_DIM ** -0.5
NEG_INF = -1e30
FORCE_BONUS = 1e4

LANES = 128
SUBLANES = 8
VMEM_LIMIT_MB = 52

G = NSA_KV_HEADS
R = NSA_GROUP
HD = HEAD_DIM
BR_LANES = 3 * NSA_HEADS
LOGF_LANE0 = BR_LANES
CHUNKS_PER_PAGE = 128 // CMP_STRIDE


def _cparams(sem, vmem_mb=VMEM_LIMIT_MB):
    return pltpu.CompilerParams(dimension_semantics=sem, vmem_limit_bytes=vmem_mb * 1024 * 1024)


def _pick(n, prefs):
    for p in prefs:
        if n % p == 0:
            return p
    return n


def _iota(shape, dim):
    return lax.broadcasted_iota(I32, shape, dim)


def _dot(a, b):
    return jnp.dot(a, b, preferred_element_type=F32)


def _dot_nt(a, b):
    return lax.dot_general(a, b, (((1,), (1,)), ((), ())), preferred_element_type=F32)


def _split3(x):
    hi = x.astype(BF16)
    r1 = x - hi.astype(F32)
    mid = r1.astype(BF16)
    lo = (r1 - mid.astype(F32)).astype(BF16)
    return hi, mid, lo


def _dot_split_lhs(x, m01):
    hi, mid, lo = _split3(x)
    return _dot(hi, m01) + _dot(mid, m01) + _dot(lo, m01)


def _dot_split_rhs(m01, x):
    hi, mid, lo = _split3(x)
    return _dot(m01, hi) + _dot(m01, mid) + _dot(m01, lo)


def _rms(x, gain):
    return x * lax.rsqrt(jnp.mean(x * x, axis=-1, keepdims=True) + EPS) * gain


def _bucket(dist):
    n = jnp.maximum(dist, 0)
    rel = jnp.log(jnp.maximum(n, 1).astype(F32) / float(BUCKET_EXACT)) / math.log(MAX_DISTANCE / BUCKET_EXACT)
    large = jnp.minimum(BUCKET_EXACT + (rel * float(N_BUCKETS - BUCKET_EXACT)).astype(I32), N_BUCKETS - 1)
    return jnp.where(n < BUCKET_EXACT, n, large)


def _lookup(bucket, vals):
    out = vals[N_BUCKETS - 1]
    for k in range(N_BUCKETS - 1):
        out = jnp.where(bucket == k, vals[k], out)
    return out


def _rms_kernel(x_ref, g_ref, o_ref):
    o_ref[...] = _rms(x_ref[...], g_ref[...]).astype(o_ref.dtype)


def rms_bf16(x, gain):
    M, D = x.shape
    tm = _pick(M, (512, 256, 128, 8))
    return pl.pallas_call(
        _rms_kernel,
        grid=(M // tm,),
        in_specs=[pl.BlockSpec((tm, D), lambda i: (i, 0)), pl.BlockSpec((1, D), lambda i: (0, 0))],
        out_specs=pl.BlockSpec((tm, D), lambda i: (i, 0)),
        out_shape=jax.ShapeDtypeStruct((M, D), BF16),
        compiler_params=_cparams(("parallel",)),
        name="rms",
    )(x, gain.reshape(1, D))


def _proj_kernel(h_ref, w_ref, g_ref, b_ref, o_ref, *, modes):
    h = h_ref[...]
    nt = len(modes)
    t = 0
    while t < nt:
        wd = 2 if t + 1 < nt else 1
        acc = _dot(h, w_ref[:, t * LANES:(t + wd) * LANES])
        for u in range(wd):
            c0 = (t + u) * LANES
            v = acc[:, u * LANES:(u + 1) * LANES]
            mode = modes[t + u]
            if mode == "norm":
                v = _rms(v, g_ref[:, c0:c0 + LANES])
            elif mode == "sigmoid":
                v = jax.nn.sigmoid(v)
            elif mode == "small":
                lane = _iota(v.shape, 1)
                z = v + b_ref[:, c0:c0 + LANES]
                logsig = jnp.minimum(z, 0.0) - jnp.log(1.0 + jnp.exp(-jnp.abs(z)))
                v = jnp.where(lane < BR_LANES, jax.nn.sigmoid(v),
                              jnp.where(lane < LOGF_LANE0 + FOX_HEADS, logsig, 0.0))
            o_ref[:, c0:c0 + LANES] = v
        t += wd


def proj(h, w, col0, N, gains, bias, modes, tn):
    M, K = h.shape
    tm = _pick(M, (1024, 512, 256, 128, 8))
    assert len(modes) * LANES == tn and N % tn == 0 and col0 % tn == 0
    jb0 = col0 // tn
    return pl.pallas_call(
        functools.partial(_proj_kernel, modes=modes),
        grid=(M // tm, N // tn),
        in_specs=[pl.BlockSpec((tm, K), lambda i, j: (i, 0)),
                  pl.BlockSpec((K, tn), lambda i, j: (0, jb0 + j)),
                  pl.BlockSpec((1, tn), lambda i, j: (0, j)),
                  pl.BlockSpec((1, tn), lambda i, j: (0, j))],
        out_specs=pl.BlockSpec((tm, tn), lambda i, j: (i, j)),
        out_shape=jax.ShapeDtypeStruct((M, N), F32),
        compiler_params=_cparams(("parallel", "arbitrary")),
        name="proj",
    )(h, w, gains, bias)


def _merge_kernel(on_ref, of_ref, ga_ref, gb_ref, wn_ref, wf_ref, o_ref):
    a = _dot(on_ref[...], wn_ref[...])
    b = _dot(of_ref[...], wf_ref[...])
    o_ref[...] = (ga_ref[...] * a + gb_ref[...] * b).astype(o_ref.dtype)


def merge(o_nsa, o_fox, gates, wbn, wbf):
    M, K = o_nsa.shape
    D = wbn.shape[1]
    tm = _pick(M, (1024, 512, 256, 128, 8))
    tn = _pick(D, (1024, 512, 256, 128))
    nj = D // tn
    return pl.pallas_call(
        _merge_kernel,
        grid=(M // tm, nj),
        in_specs=[pl.BlockSpec((tm, K), lambda i, j: (i, 0)),
                  pl.BlockSpec((tm, K), lambda i, j: (i, 0)),
                  pl.BlockSpec((tm, tn), lambda i, j: (i, j)),
                  pl.BlockSpec((tm, tn), lambda i, j: (i, j + nj)),
                  pl.BlockSpec((K, tn), lambda i, j: (0, j)),
                  pl.BlockSpec((K, tn), lambda i, j: (0, j))],
        out_specs=pl.BlockSpec((tm, tn), lambda i, j: (i, j)),
        out_shape=jax.ShapeDtypeStruct((M, D), BF16),
        compiler_params=_cparams(("parallel", "arbitrary")),
        name="merge",
    )(o_nsa, o_fox, gates, gates, wbn, wbf)


def _wo_kernel(m_ref, w_ref, x_ref, g_ref, x1_ref, h1_ref):
    x1 = x_ref[...] + _dot(m_ref[...], w_ref[...])
    x1_ref[...] = x1
    h1_ref[...] = _rms(x1, g_ref[...]).astype(h1_ref.dtype)


def out_proj(merged, wo, x, gain1):
    M, D = x.shape
    tm = _pick(M, (512, 256, 128, 8))
    return pl.pallas_call(
        _wo_kernel,
        grid=(M // tm,),
        in_specs=[pl.BlockSpec((tm, D), lambda i: (i, 0)),
                  pl.BlockSpec((D, D), lambda i: (0, 0)),
                  pl.BlockSpec((tm, D), lambda i: (i, 0)),
                  pl.BlockSpec((1, D), lambda i: (0, 0))],
        out_specs=[pl.BlockSpec((tm, D), lambda i: (i, 0)), pl.BlockSpec((tm, D), lambda i: (i, 0))],
        out_shape=[jax.ShapeDtypeStruct((M, D), F32), jax.ShapeDtypeStruct((M, D), BF16)],
        compiler_params=_cparams(("parallel",)),
        name="out_proj",
    )(merged, wo, x, gain1.reshape(1, D))


def _ffn_in_kernel(h_ref, wg_ref, wu_ref, o_ref):
    h = h_ref[...]
    gt = _dot(h, wg_ref[...])
    up = _dot(h, wu_ref[...])
    o_ref[...] = (gt * jax.nn.sigmoid(gt) * up).astype(o_ref.dtype)


def ffn_in(h1, wfi):
    M, D = h1.shape
    F = wfi.shape[1] // 2
    tm = _pick(M, (1024, 512, 256, 128, 8))
    tn = _pick(F, (512, 256, 128))
    nj = F // tn
    return pl.pallas_call(
        _ffn_in_kernel,
        grid=(M // tm, nj),
        in_specs=[pl.BlockSpec((tm, D), lambda i, j: (i, 0)),
                  pl.BlockSpec((D, tn), lambda i, j: (0, j)),
                  pl.BlockSpec((D, tn), lambda i, j: (0, j + nj))],
        out_specs=pl.BlockSpec((tm, tn), lambda i, j: (i, j)),
        out_shape=jax.ShapeDtypeStruct((M, F), BF16),
        compiler_params=_cparams(("parallel", "arbitrary")),
        name="ffn_in",
    )(h1, wfi, wfi)


def _ffn_out_kernel(a_ref, w_ref, x_ref, o_ref):
    o_ref[...] = x_ref[...] + _dot(a_ref[...], w_ref[...])


def ffn_out(act, wfo, x1):
    M, F = act.shape
    D = wfo.shape[1]
    tm = _pick(M, (1024, 512, 256, 128, 8))
    tn = _pick(D, (512, 256, 128))
    return pl.pallas_call(
        _ffn_out_kernel,
        grid=(M // tm, D // tn),
        in_specs=[pl.BlockSpec((tm, F), lambda i, j: (i, 0)),
                  pl.BlockSpec((F, tn), lambda i, j: (0, j)),
                  pl.BlockSpec((tm, tn), lambda i, j: (i, j))],
        out_specs=pl.BlockSpec((tm, tn), lambda i, j: (i, j)),
        out_shape=jax.ShapeDtypeStruct((M, D), F32),
        compiler_params=_cparams(("parallel", "arbitrary")),
        name="ffn_out",
    )(act, wfo, x1)


def _ple_kernel(x_ref, g_ref, wpg_ref, pe_ref, wple_ref, gn_ref, o_ref, hn_ref):
    x = x_ref[...]
    h2 = _rms(x, g_ref[...]).astype(BF16)
    gate = jax.nn.sigmoid(_dot(h2, wpg_ref[...]))
    x3 = x + gate * _dot(pe_ref[...].astype(BF16), wple_ref[...])
    o_ref[...] = x3
    hn_ref[...] = _rms(x3, gn_ref[...]).astype(hn_ref.dtype)


def ple_gate(x2, gain2, wpg, pe, wple, gain_next):
    M, D = x2.shape
    Pd = pe.shape[1]
    tm = _pick(M, (512, 256, 128, 8))
    return pl.pallas_call(
        _ple_kernel,
        grid=(M // tm,),
        in_specs=[pl.BlockSpec((tm, D), lambda i: (i, 0)),
                  pl.BlockSpec((1, D), lambda i: (0, 0)),
                  pl.BlockSpec((D, D), lambda i: (0, 0)),
                  pl.BlockSpec((tm, Pd), lambda i: (i, 0)),
                  pl.BlockSpec((Pd, D), lambda i: (0, 0)),
                  pl.BlockSpec((1, D), lambda i: (0, 0))],
        out_specs=[pl.BlockSpec((tm, D), lambda i: (i, 0)), pl.BlockSpec((tm, D), lambda i: (i, 0))],
        out_shape=[jax.ShapeDtypeStruct((M, D), F32), jax.ShapeDtypeStruct((M, D), BF16)],
        compiler_params=_cparams(("parallel",)),
        name="ple_gate",
    )(x2, gain2.reshape(1, D), wpg, pe, wple, gain_next.reshape(1, D))


SEL_BIAS_KINDS = 3
WIN_BIAS_KINDS = WINDOW // LANES + 2


def _bias_sw_kernel(tbl_ref, bs_ref, bw_ref):
    g = pl.program_id(0)
    kk = _iota((LANES, LANES), 0)
    qq = _iota((LANES, LANES), 1)
    bw_ref[0, WIN_BIAS_KINDS - 1] = jnp.full((LANES, R * LANES), NEG_INF, F32)
    for off in range(WIN_BIAS_KINDS - 1):
        d = off * LANES + qq - kk
        bkt = _bucket(d)
        in_window = (d >= 0) & (d < WINDOW)
        for r in range(R):
            cols = slice(r * LANES, (r + 1) * LANES)
            b = _lookup(bkt, [tbl_ref[k, g * R + r] for k in range(N_BUCKETS)])
            bw_ref[0, off, :, cols] = jnp.where(in_window, b, NEG_INF)
            if off < SEL_BIAS_KINDS - 1:
                bs_ref[0, off, :, cols] = b
    for r in range(R):
        bs_ref[0, SEL_BIAS_KINDS - 1, :, r * LANES:(r + 1) * LANES] = jnp.full(
            (LANES, LANES), tbl_ref[N_BUCKETS - 1, g * R + r], F32)


def bias_sel_win(table):
    shapes = [(G, SEL_BIAS_KINDS, LANES, R * LANES), (G, WIN_BIAS_KINDS, LANES, R * LANES)]
    return pl.pallas_call(
        _bias_sw_kernel,
        grid=(G,),
        in_specs=[pl.BlockSpec(memory_space=pltpu.SMEM)],
        out_specs=[pl.BlockSpec((1,) + s[1:], lambda g: (g, 0, 0, 0)) for s in shapes],
        out_shape=[jax.ShapeDtypeStruct(s, F32) for s in shapes],
        compiler_params=_cparams(("arbitrary",)),
        name="bias_sel_win",
    )(table)


def _bias_cmp_kernel(tbl_ref, bc_ref):
    g = pl.program_id(0)
    qi = pl.program_id(1)
    cc = _iota((LANES, LANES), 0)
    qq = _iota((LANES, LANES), 1)
    d = qi * LANES + qq - (cc * CMP_STRIDE + (CMP_LEN - 1))
    bkt = _bucket(d)
    for r in range(R):
        b = _lookup(bkt, [tbl_ref[k, g * R + r] for k in range(N_BUCKETS)])
        bc_ref[0, 0, :, r * LANES:(r + 1) * LANES] = jnp.where(d >= 0, b, NEG_INF)


def bias_cmp(table, n_qblk):
    return pl.pallas_call(
        _bias_cmp_kernel,
        grid=(G, n_qblk),
        in_specs=[pl.BlockSpec(memory_space=pltpu.SMEM)],
        out_specs=pl.BlockSpec((1, 1, LANES, R * LANES), lambda g, i: (g, i, 0, 0)),
        out_shape=jax.ShapeDtypeStruct((G, n_qblk, LANES, R * LANES), F32),
        compiler_params=_cparams(("arbitrary", "arbitrary")),
        name="bias_cmp",
    )(table)


def _cumsum_kernel(x_ref, cr_ref, cp_ref):
    T = x_ref.shape[1]
    ii = _iota((LANES, LANES), 0)
    jj = _iota((LANES, LANES), 1)
    tri = jnp.where(jj <= ii, 1.0, 0.0).astype(BF16)
    carry = jnp.zeros((1, LANES), F32)
    for blk in range(T // LANES):
        sl = slice(blk * LANES, (blk + 1) * LANES)
        c = _dot_split_rhs(tri, x_ref[0, sl, :]) + carry
        cr_ref[0, :, sl] = c.T[LOGF_LANE0:LOGF_LANE0 + FOX_HEADS, :]
        for h in range(FOX_HEADS):
            cp_ref[0, h, sl, :] = jnp.broadcast_to(c[:, LOGF_LANE0 + h:LOGF_LANE0 + h + 1], (LANES, LANES))
        carry = c[LANES - 1:LANES, :]


def cumsum_logf(small):
    B, T, _ = small.shape
    return pl.pallas_call(
        _cumsum_kernel,
        grid=(B,),
        in_specs=[pl.BlockSpec((1, T, LANES), lambda b: (b, 0, 0))],
        out_specs=[pl.BlockSpec((1, FOX_HEADS, T), lambda b: (b, 0, 0)),
                   pl.BlockSpec((1, FOX_HEADS, T, LANES), lambda b: (b, 0, 0, 0))],
        out_shape=[jax.ShapeDtypeStruct((B, FOX_HEADS, T), F32),
                   jax.ShapeDtypeStruct((B, FOX_HEADS, T, LANES), F32)],
        compiler_params=_cparams(("parallel",)),
        name="cumsum_logf",
    )(small)


def key_blocks_transposed(x, n_heads, blk):
    B, T, _ = x.shape
    return x.reshape(B, T // blk, blk, n_heads, HD).transpose(0, 3, 1, 4, 2).astype(BF16)


def _fox_prompt_kernel(q_ref, k_ref, vt_ref, cq_ref, ck_ref, o_ref, *, tk):
    tq = LANES
    qi = pl.program_id(1)
    nh = FOX_HEADS
    qf = q_ref[0]
    qs = [(qf[:, h * HD:(h + 1) * HD] * SCALE).astype(BF16) for h in range(nh)]
    kk = _iota((tk, tq), 0)
    qq = _iota((tk, tq), 1)

    def step(kb, carry, diagonal):
        k0 = pl.multiple_of(kb * tk, tk)
        scores = [_dot_nt(k_ref[0, pl.ds(k0, tk), h * HD:(h + 1) * HD], qs[h]) for h in range(nh)]
        probs = []
        for h in range(nh):
            m, l, _ = carry[h]
            s = scores[h] + (cq_ref[0, h, pl.ds(qi, 1), :] - ck_ref[0, h, pl.ds(k0, tk), :])
            if diagonal:
                s = jnp.where(k0 + kk <= qi * tq + qq, s, NEG_INF)
            m_new = jnp.maximum(m, jnp.max(s, axis=0, keepdims=True))
            alpha = jnp.exp(m - m_new)
            p = jnp.exp(s - m_new)
            probs.append((m_new, alpha, alpha * l + jnp.sum(p, axis=0, keepdims=True), p.astype(BF16)))
        out = []
        for h in range(nh):
            m_new, alpha, l, p = probs[h]
            out.append((m_new, l, alpha * carry[h][2] + _dot(vt_ref[0, h, kb], p)))
        return tuple(out)

    n_full = lax.div(qi * tq, tk)
    n_all = lax.div(qi * tq + tq + tk - 1, tk)
    one = (jnp.full((1, tq), NEG_INF, F32), jnp.zeros((1, tq), F32), jnp.zeros((HD, tq), F32))
    carry = lax.fori_loop(0, n_full, lambda kb, c: step(kb, c, False), (one,) * nh)
    res = lax.fori_loop(n_full, n_all, lambda kb, c: step(kb, c, True), carry)
    for h in range(nh):
        _, l, acc = res[h]
        o_ref[0, :, h * HD:(h + 1) * HD] = (acc / l).T.astype(o_ref.dtype)


def fox_prompt(qf, k_bf, v_t, c_row, c_rep):
    B, T, W = qf.shape
    tq = LANES
    tk = v_t.shape[-1]
    assert T % tk == 0 and tk % tq == 0
    c_row4 = c_row.reshape(B, FOX_HEADS, T // tq, tq)
    return pl.pallas_call(
        functools.partial(_fox_prompt_kernel, tk=tk),
        grid=(B, T // tq),
        in_specs=[pl.BlockSpec((1, tq, W), lambda b, i: (b, i, 0)),
                  pl.BlockSpec((1, T, W), lambda b, i: (b, 0, 0)),
                  pl.BlockSpec((1, FOX_HEADS, T // tk, HD, tk), lambda b, i: (b, 0, 0, 0, 0)),
                  pl.BlockSpec((1, FOX_HEADS, T // tq, tq), lambda b, i: (b, 0, 0, 0)),
                  pl.BlockSpec((1, FOX_HEADS, T, LANES), lambda b, i: (b, 0, 0, 0))],
        out_specs=pl.BlockSpec((1, tq, W), lambda b, i: (b, i, 0)),
        out_shape=jax.ShapeDtypeStruct((B, T, W), BF16),
        compiler_params=_cparams(("parallel", "arbitrary")),
        name="fox_prompt",
    )(qf, k_bf, v_t, c_row4, c_rep)


def _cmp_p_kernel(pt_ref, *refs, npg, row_per_head):
    nref = npg if row_per_head else npg * 2 * G
    page_refs = refs[:nref]
    w_ref, o_ref, lhs_ref = refs[nref:]
    for i in range(npg):
        for c in range(2 * G):
            for t in range(CMP_STRIDE):
                if row_per_head:
                    rows = page_refs[i][pl.ds(t, CHUNKS_PER_PAGE, stride=CMP_STRIDE), c, :]
                else:
                    rows = page_refs[i * 2 * G + c][pl.ds(t, CHUNKS_PER_PAGE, stride=CMP_STRIDE), :]
                lhs_ref[c, i * CHUNKS_PER_PAGE:(i + 1) * CHUNKS_PER_PAGE, t * HD:(t + 1) * HD] = rows
    for kv in range(2):
        for g in range(G):
            o_ref[0, kv, g] = _dot(lhs_ref[kv * G + g].astype(BF16), w_ref[kv])


def compress_partial(pages, page_table, layer, w1cat, row_per_head):
    NB, n_pages = page_table.shape
    npg = _pick(n_pages, (16, 8, 4, 2, 1))
    rows = npg * CHUNKS_PER_PAGE
    kdim = CMP_STRIDE * HD
    in_specs = []
    for i in range(npg):
        if row_per_head:
            in_specs.append(pl.BlockSpec((None, None, 128, None, 2 * G, HD),
                                         lambda b, p, pt, i=i: (pt[b, p * npg + i], layer, 0, 0, 0, 0)))
            continue
        for c in range(2 * G):
            in_specs.append(pl.BlockSpec((None, None, 128, HD),
                                         lambda b, p, pt, i=i, c=c: (pt[b, p * npg + i], layer, 0, c)))
    in_specs.append(pl.BlockSpec((2, kdim, 2 * HD), lambda b, p, pt: (0, 0, 0)))
    return pl.pallas_call(
        functools.partial(_cmp_p_kernel, npg=npg, row_per_head=row_per_head),
        grid_spec=pltpu.PrefetchScalarGridSpec(
            num_scalar_prefetch=1,
            grid=(NB, n_pages // npg),
            in_specs=in_specs,
            out_specs=pl.BlockSpec((1, 2, G, rows, 2 * HD), lambda b, p, pt: (b, 0, 0, p, 0)),
            scratch_shapes=[pltpu.VMEM((2 * G, rows, kdim), F32)]),
        out_shape=jax.ShapeDtypeStruct((NB, 2, G, n_pages * CHUNKS_PER_PAGE, 2 * HD), F32),
        compiler_params=_cparams(("parallel", "arbitrary")),
        name="compress_partial",
    )(page_table, *([pages] * len(in_specs[:-1])), w1cat)


def _gelu_tanh(x):
    return x * (0.5 * (1.0 + jnp.tanh(math.sqrt(2.0 / math.pi) * (x + 0.044715 * (x * x * x)))))


def _cmp_fin_kernel(p_ref, pos_ref, w1_ref, b1_ref, w2_ref, gk_ref, kc_ref, vc_ref):
    NC = p_ref.shape[3]
    row = _iota((NC, HD), 0)
    for kv in range(2):
        w1 = w1_ref[kv]
        posterm = (_dot(pos_ref[kv, 0], w1[:, :HD]) + _dot(pos_ref[kv, 1], w1[:, HD:]))[0:1, :] + b1_ref[kv]
        for g in range(G):
            P = p_ref[0, kv, g]
            pre = P[:, :HD] + pltpu.roll(P[:, HD:], NC - 1, 0) + posterm
            o = _dot(_gelu_tanh(pre).astype(BF16), w2_ref[kv])
            if kv == 0:
                o = _rms(o, gk_ref[...])
            o = jnp.where(row < NC - 1, o, 0.0)
            if kv == 0:
                kc_ref[0, g] = o
            else:
                vc_ref[0, g] = o


def compress_finish(part, pos_ab, w1cat, b1, w2, gain_kc):
    NB, _, _, NC, _ = part.shape
    kdim = CMP_STRIDE * HD
    full = lambda shape: pl.BlockSpec(shape, lambda b: (0,) * len(shape))
    return pl.pallas_call(
        _cmp_fin_kernel,
        grid=(NB,),
        in_specs=[pl.BlockSpec((1, 2, G, NC, 2 * HD), lambda b: (b, 0, 0, 0, 0)),
                  full((2, 2, SUBLANES, kdim)), full((2, kdim, 2 * HD)), full((2, 1, HD)),
                  full((2, HD, HD)), full((1, HD))],
        out_specs=[pl.BlockSpec((1, G, NC, HD), lambda b: (b, 0, 0, 0)),
                   pl.BlockSpec((1, G, NC, HD), lambda b: (b, 0, 0, 0))],
        out_shape=[jax.ShapeDtypeStruct((NB, G, NC, HD), F32), jax.ShapeDtypeStruct((NB, G, NC, HD), F32)],
        compiler_params=_cparams(("parallel",)),
        name="compress_finish",
    )(part, pos_ab, w1cat, b1, w2, gain_kc)


def _sel_overlap_matrix(nc, nsb, n_cb):
    ci = _iota((nc, nsb), 0)
    jb = _iota((nc, nsb), 1)
    ratio = SEL_BLOCK // CMP_STRIDE
    first = ratio * jb - (CMP_LEN // CMP_STRIDE) + 1
    n_ov = (SEL_BLOCK + CMP_LEN) // CMP_STRIDE - 1
    hit = (ci >= first) & (ci < first + n_ov) & (ci < n_cb)
    return jnp.where(hit, 1.0, 0.0).astype(BF16)


def _sel_overlap_matrix_t(nsb, nc, n_cb):
    jb = _iota((nsb, nc), 0)
    ci = _iota((nsb, nc), 1)
    first = (SEL_BLOCK // CMP_STRIDE) * jb - (CMP_LEN // CMP_STRIDE) + 1
    n_ov = (SEL_BLOCK + CMP_LEN) // CMP_STRIDE - 1
    hit = (ci >= first) & (ci < first + n_ov) & (ci < n_cb)
    return jnp.where(hit, 1.0, 0.0).astype(BF16)


def _div_pow2(x, d):
    if isinstance(x, int):
        return x // d
    return lax.shift_right_arithmetic(x, jnp.full(x.shape, d.bit_length() - 1, I32))


def _block_scores(imp_s, qpos, jb, n_sb):
    back = _div_pow2(qpos, SEL_BLOCK) - jb
    forced = (jb == 0) | ((back >= 0) & (back < N_LOCAL_BLOCKS))
    score = jnp.where(back >= 0, imp_s + jnp.where(forced, FORCE_BONUS, 0.0), -1.0)
    return jnp.where(jb < n_sb, score, -2.0)


def _nsa_prompt_kernel(q_ref, kc_ref, vct_ref, ks_ref, vst_ref, kw_ref, vwt_ref, br_ref, bc_ref, bs_ref, bw_ref,
                       exp_ref, o_ref, msk_ref, oc_ref, brt_ref, *, n_cb, n_sb, n_sel):
    g = pl.program_id(1)
    qi = pl.program_id(2)
    tq = LANES
    qf = q_ref[0]
    qt = jnp.concatenate([(qf[:, r * HD:(r + 1) * HD] * SCALE).T for r in range(R)], axis=1).astype(BF16)
    qpos = qi * tq + _iota((1, tq), 1)
    heads = lambda x: jnp.concatenate([x] * R, axis=1)

    has_c = heads(qpos >= CMP_LEN - 1)
    s = _dot(kc_ref[0, 0].astype(BF16), qt) + bc_ref[0, 0]
    e = jnp.exp(s - jnp.max(s, axis=0, keepdims=True))
    p = e * jnp.where(has_c, 1.0 / jnp.sum(e, axis=0, keepdims=True), 0.0)
    oc_ref[...] = _dot(vct_ref[0, 0].astype(BF16), p.astype(BF16))
    imp_c = p[:, 0:tq]
    for r in range(1, R):
        imp_c = imp_c + p[:, r * tq:(r + 1) * tq]

    nsb8 = -(-n_sb // SUBLANES) * SUBLANES
    imp_s = _dot_split_rhs(_sel_overlap_matrix_t(LANES, LANES, n_cb), imp_c)[0:nsb8]
    jb = _iota((nsb8, tq), 0)
    score = _block_scores(imp_s, qpos, jb, n_sb)
    rank = jnp.zeros((nsb8, tq), I32)
    for i in range(n_sb):
        row = score[i:i + 1, :]
        rank = rank + jnp.where(row > score, 1, jnp.where((row == score) & (jb > i), 1, 0))
    sel = jnp.where(rank < n_sel, 1.0, 0.0)
    if nsb8 < LANES:
        sel = jnp.concatenate([sel, jnp.zeros((LANES - nsb8, tq), F32)], axis=0)
    sel = sel.astype(BF16)

    n_keys = msk_ref.shape[0]
    chosen = _dot(exp_ref[...], sel)
    key_pos = _iota((n_keys, tq), 0)
    msk_ref[...] = jnp.where((chosen > 0.5) & (qi * tq + _iota((n_keys, tq), 1) >= key_pos), 0.0, NEG_INF)

    def branch(k_ref, vt_ref, b_ref, kind_of, masked):
        def body(j, carry):
            m, l, acc = carry
            k0 = pl.multiple_of(j * (2 * LANES), 2 * LANES)
            off = qi - 2 * j
            bias = jnp.concatenate([b_ref[0, kind_of(off)], b_ref[0, kind_of(off - 1)]], axis=0)
            s = _dot(k_ref[0, pl.ds(k0, 2 * LANES), :], qt) + bias
            if masked:
                s = s + heads(msk_ref[pl.ds(k0, 2 * LANES), :])
            m_new = jnp.maximum(m, jnp.max(s, axis=0, keepdims=True))
            alpha = jnp.exp(m - m_new)
            p = jnp.exp(s - m_new)
            l = alpha * l + jnp.sum(p, axis=0, keepdims=True)
            vblk = jnp.concatenate([vt_ref[0, 0, 2 * j], vt_ref[0, 0, 2 * j + 1]], axis=1)
            acc = alpha * acc + _dot(vblk, p.astype(BF16))
            return m_new, l, acc

        return body

    n_win = WINDOW // LANES
    sel_trip = branch(ks_ref, vst_ref, bs_ref, lambda off: jnp.clip(off, 0, SEL_BIAS_KINDS - 1), True)
    win_trip = branch(kw_ref, vwt_ref, bw_ref, lambda off: jnp.where((off < 0) | (off > n_win), n_win + 1, off),
                      False)
    init = (jnp.full((1, R * tq), NEG_INF, F32), jnp.zeros((1, R * tq), F32), jnp.zeros((HD, R * tq), F32))
    first_win = jnp.maximum(qi - n_win, 0) // 2
    sel_c = lax.fori_loop(0, first_win, sel_trip, init)
    sel_c, win_c = lax.fori_loop(first_win, qi // 2 + 1,
                                 lambda j, c: (sel_trip(j, c[0]), win_trip(j, c[1])), (sel_c, init))
    o_s = sel_c[2] * (1.0 / sel_c[1])
    o_w = win_c[2] * (1.0 / win_c[1])
    brt_ref[...] = br_ref[0].T
    gate = lambda i: jnp.concatenate([brt_ref[pl.ds((g * R + r) * 3 + i, 1), :] for r in range(R)], axis=1)
    o = gate(0) * oc_ref[...] + gate(1) * o_s + gate(2) * o_w
    for r in range(R):
        o_ref[0, :, r * HD:(r + 1) * HD] = o[:, r * tq:(r + 1) * tq].T.astype(o_ref.dtype)


def nsa_prompt(qa, kc, vc_t, ks_bf, vs_t, kw_bf, vw_t, small, b_cmp, b_sel, b_win):
    B, T, _ = qa.shape
    assert T % LANES == 0 and T // CMP_STRIDE == LANES and kc.shape[2] == LANES
    n_cb = T // CMP_STRIDE - CMP_LEN // CMP_STRIDE + 1
    n_sb = -(-T // SEL_BLOCK)
    n_sel = min(SEL_TOPN, n_sb)
    tq = LANES
    nkb = T // LANES
    assert nkb % 2 == 0 and n_sb <= LANES
    grp = lambda b, g, i: (b, g, 0, 0)
    expand = (jnp.arange(T, dtype=I32)[:, None] // SEL_BLOCK == jnp.arange(LANES, dtype=I32)[None, :]).astype(BF16)
    return pl.pallas_call(
        functools.partial(_nsa_prompt_kernel, n_cb=n_cb, n_sb=n_sb, n_sel=n_sel),
        grid=(B, G, T // tq),
        in_specs=[pl.BlockSpec((1, tq, R * HD), lambda b, g, i: (b, i, g)),
                  pl.BlockSpec((1, 1, LANES, HD), grp),
                  pl.BlockSpec((1, 1, HD, LANES), grp),
                  pl.BlockSpec((1, T, HD), lambda b, g, i: (b, 0, g)),
                  pl.BlockSpec((1, 1, nkb, HD, LANES), lambda b, g, i: (b, g, 0, 0, 0)),
                  pl.BlockSpec((1, T, HD), lambda b, g, i: (b, 0, g)),
                  pl.BlockSpec((1, 1, nkb, HD, LANES), lambda b, g, i: (b, g, 0, 0, 0)),
                  pl.BlockSpec((1, tq, LANES), lambda b, g, i: (b, i, 0)),
                  pl.BlockSpec((1, 1, LANES, R * LANES), lambda b, g, i: (g, i, 0, 0)),
                  pl.BlockSpec((1, SEL_BIAS_KINDS, LANES, R * LANES), lambda b, g, i: (g, 0, 0, 0)),
                  pl.BlockSpec((1, WIN_BIAS_KINDS, LANES, R * LANES), lambda b, g, i: (g, 0, 0, 0)),
                  pl.BlockSpec((T, LANES), lambda b, g, i: (0, 0))],
        out_specs=pl.BlockSpec((1, tq, R * HD), lambda b, g, i: (b, i, g)),
        out_shape=jax.ShapeDtypeStruct((B, T, NSA_HEADS * HD), BF16),
        scratch_shapes=[pltpu.VMEM((T, tq), F32), pltpu.VMEM((HD, R * tq), F32),
                        pltpu.VMEM((LANES, tq), F32)],
        compiler_params=_cparams(("parallel", "parallel", "arbitrary")),
        name="nsa_prompt",
    )(qa, kc, vc_t, ks_bf, vs_t, kw_bf, vw_t, small, b_cmp, b_sel, b_win, expand)


def _dec_cmp_kernel(q_ref, kc_ref, vc_ref, tblt_ref, oc_ref, idx_ref, *, qpos, n_cb, n_sb, n_sel, nsbp):
    NC = kc_ref.shape[2]
    q8 = q_ref[0].astype(BF16)
    row = _iota((NSA_HEADS, NC), 0)
    j = _iota((NSA_HEADS, NC), 1)
    s = jnp.where(row < R, _dot_nt(q8, kc_ref[0, 0].astype(BF16)), _dot_nt(q8, kc_ref[0, 1].astype(BF16)))
    dist = qpos - (j * CMP_STRIDE + (CMP_LEN - 1))
    valid = dist >= 0
    tblt = tblt_ref[...]
    bias = _lookup(_bucket(dist), [tblt[:, k:k + 1] for k in range(N_BUCKETS)])
    s = jnp.where(valid, s * SCALE + bias, NEG_INF)
    m = jnp.max(s, axis=1, keepdims=True)
    e = jnp.where(valid, jnp.exp(s - m), 0.0)
    l = jnp.sum(e, axis=1, keepdims=True)
    p = e / jnp.where(l > 0.0, l, 1.0)
    pb = p.astype(BF16)
    row_o = _iota((NSA_HEADS, HD), 0)
    oc_ref[0] = jnp.where(row_o < R, _dot(pb, vc_ref[0, 0].astype(BF16)), _dot(pb, vc_ref[0, 1].astype(BF16)))

    imp0 = jnp.sum(jnp.where(row < R, p, 0.0), axis=0, keepdims=True)
    imp1 = jnp.sum(jnp.where(row >= R, p, 0.0), axis=0, keepdims=True)
    imp = jnp.where(row == 0, imp0, jnp.where(row == 1, imp1, 0.0))
    imp_s = _dot_split_lhs(imp, _sel_overlap_matrix(NC, nsbp, n_cb))
    jb = _iota((NSA_HEADS, nsbp), 1)
    score = _block_scores(imp_s, qpos, jb, n_sb)
    jbf = jb.astype(F32)
    lane_o = _iota((NSA_HEADS, LANES), 1)
    out = jnp.zeros((NSA_HEADS, LANES), I32)
    for n in range(n_sel):
        mx = jnp.max(score, axis=1, keepdims=True)
        am = jnp.min(jnp.where(score == mx, jbf, float(nsbp)), axis=1, keepdims=True)
        out = jnp.where(lane_o == n, am.astype(I32), out)
        score = jnp.where(jbf == am, -3.0, score)
    idx_ref[0] = out


def dec_cmp(q8, kc, vc, tblt, qpos):
    DB = q8.shape[0]
    NC = kc.shape[2]
    L = qpos + 1
    n_cb = L // CMP_STRIDE - CMP_LEN // CMP_STRIDE + 1
    assert n_cb == NC - 1
    n_sb = -(-L // SEL_BLOCK)
    n_sel = min(SEL_TOPN, n_sb)
    nsbp = -(-n_sb // LANES) * LANES
    oc, idx = pl.pallas_call(
        functools.partial(_dec_cmp_kernel, qpos=qpos, n_cb=n_cb, n_sb=n_sb, n_sel=n_sel, nsbp=nsbp),
        grid=(DB,),
        in_specs=[pl.BlockSpec((1, NSA_HEADS, HD), lambda b: (b, 0, 0)),
                  pl.BlockSpec((1, G, NC, HD), lambda b: (b, 0, 0, 0)),
                  pl.BlockSpec((1, G, NC, HD), lambda b: (b, 0, 0, 0)),
                  pl.BlockSpec((NSA_HEADS, N_BUCKETS), lambda b: (0, 0))],
        out_specs=[pl.BlockSpec((1, NSA_HEADS, HD), lambda b: (b, 0, 0)),
                   pl.BlockSpec((1, NSA_HEADS, LANES), lambda b: (b, 0, 0))],
        out_shape=[jax.ShapeDtypeStruct((DB, NSA_HEADS, HD), F32), jax.ShapeDtypeStruct((DB, NSA_HEADS, LANES), I32)],
        compiler_params=_cparams(("parallel",)),
        name="dec_cmp",
    )(q8, kc, vc, tblt)
    return oc, idx[:, :G, :n_sel].reshape(DB, G * n_sel), n_sel


def _softmax_with_new(s, valid, s_new, new_ok):
    s = jnp.where(valid, s, NEG_INF)
    s_new = jnp.where(new_ok, s_new, NEG_INF)
    m = jnp.maximum(jnp.max(s, axis=1, keepdims=True), s_new)
    e = jnp.where(valid, jnp.exp(s - m), 0.0)
    en = jnp.where(new_ok, jnp.exp(s_new - m), 0.0)
    l = jnp.sum(e, axis=1, keepdims=True) + en
    inv = 1.0 / jnp.where(l > 0.0, l, 1.0)
    return e, en, inv


def _dec_selwin_kernel(pt_ref, idx_ref, *refs, n_sel, qpos, jb_new, w_buf):
    blk_refs = refs[:G * n_sel]
    q_ref, new_ref, win_ref, wnew_ref, br_ref, oc_ref, tblt_ref, o_ref = refs[G * n_sel:]
    rph = 4 * G
    b = pl.program_id(0)
    q8 = q_ref[0]
    row8 = _iota((NSA_HEADS, HD), 0)
    tblt = tblt_ref[...]
    tcols = [tblt[:, k:k + 1] for k in range(N_BUCKETS)]
    new8 = new_ref[0]
    wnew = wnew_ref[0]
    brrow = br_ref[0]
    oc = oc_ref[0]
    nk = n_sel * SEL_BLOCK
    lane = _iota((NSA_HEADS, nk), 1)
    outs = []
    for g in range(G):
        qg = jnp.where(_div_pow2(row8, R) == g, q8, 0.0)
        qb = qg.astype(BF16)
        qr = qb.astype(F32)
        kmat = jnp.concatenate([blk_refs[g * n_sel + n][pl.ds(2 * G + g, SEL_BLOCK, stride=rph), :]
                                for n in range(n_sel)], axis=0).astype(BF16)
        vmat = jnp.concatenate([blk_refs[g * n_sel + n][pl.ds(3 * G + g, SEL_BLOCK, stride=rph), :]
                                for n in range(n_sel)], axis=0).astype(BF16)
        idxv = jnp.zeros((NSA_HEADS, nk), I32)
        has_new = jnp.zeros((), jnp.bool_)
        for n in range(n_sel):
            sidx = idx_ref[b, g * n_sel + n]
            idxv = jnp.where(_div_pow2(lane, SEL_BLOCK) == n, sidx, idxv)
            has_new = has_new | (sidx == jb_new)
        dist = qpos - (idxv * SEL_BLOCK + (lane & (SEL_BLOCK - 1)))
        valid = (idxv < jb_new) & (dist >= 0)
        s = _dot_nt(qb, kmat) * SCALE + _lookup(_bucket(dist), tcols)
        kn = new8[2 * G + g:2 * G + g + 1, :].astype(BF16).astype(F32)
        vn = new8[3 * G + g:3 * G + g + 1, :].astype(BF16).astype(F32)
        s_new = jnp.sum(qr * kn, axis=1, keepdims=True) * SCALE + tcols[0]
        e, en, inv = _softmax_with_new(s, valid, s_new, has_new)
        o_s = (_dot(e.astype(BF16), vmat) + en * vn) * inv

        kwin = win_ref[pl.ds(g, w_buf, stride=2 * G), :].astype(BF16)
        vwin = win_ref[pl.ds(G + g, w_buf, stride=2 * G), :].astype(BF16)
        lw = _iota((NSA_HEADS, w_buf), 1)
        dist_w = w_buf - lw
        valid_w = (dist_w < WINDOW) & (qpos - dist_w >= 0)
        sw = _dot_nt(qb, kwin) * SCALE + _lookup(_bucket(dist_w), tcols)
        kwn = wnew[g:g + 1, :].astype(BF16).astype(F32)
        vwn = wnew[G + g:G + g + 1, :].astype(BF16).astype(F32)
        sw_new = jnp.sum(qr * kwn, axis=1, keepdims=True) * SCALE + tcols[0]
        ew, ewn, invw = _softmax_with_new(sw, valid_w, sw_new, jnp.ones((), jnp.bool_))
        o_w = (_dot(ew.astype(BF16), vwin) + ewn * vwn) * invw

        def bcol(i):
            return jnp.sum(jnp.where(_iota((NSA_HEADS, LANES), 1) == row8 * 3 + i, brrow, 0.0), axis=1, keepdims=True)

        outs.append(bcol(0) * oc + bcol(1) * o_s + bcol(2) * o_w)
    o_ref[0] = jnp.where(row8 < R, outs[0], outs[1])


def dec_selwin(q8, new8, wnew4, small_s, oc, tblt, cache_nsa_kv, state_win, page_table, idx, n_sel, layer, qpos):
    DB = q8.shape[0]
    n_pool, depth = cache_nsa_kv.shape[:2]
    page = cache_nsa_kv.shape[2]
    halves = page // SEL_BLOCK
    w_buf = state_win.shape[2]
    n_pages = page_table.shape[1]
    jb_new = qpos // SEL_BLOCK
    rph = 4 * G
    cache_rows = cache_nsa_kv.reshape(n_pool, depth, page * rph, HD)
    state_rows = state_win.reshape(depth, DB, w_buf * 2 * G, HD)
    in_specs = []
    args = []
    for g in range(G):
        for n in range(n_sel):
            def imap(b, pt, ix, g=g, n=n):
                jb = jnp.clip(ix[b, g * n_sel + n], 0, jb_new - 1)
                return (pt[b, jb // halves], layer, jb % halves, 0)
            in_specs.append(pl.BlockSpec((None, None, SEL_BLOCK * rph, HD), imap))
            args.append(cache_rows)
    row3 = lambda b, pt, ix: (b, 0, 0)
    in_specs += [pl.BlockSpec((1, NSA_HEADS, HD), row3), pl.BlockSpec((1, 4 * G, HD), row3),
                 pl.BlockSpec((None, None, w_buf * 2 * G, HD), lambda b, pt, ix: (layer, b, 0, 0)),
                 pl.BlockSpec((1, 2 * G, HD), row3), pl.BlockSpec((1, 1, LANES), row3),
                 pl.BlockSpec((1, NSA_HEADS, HD), row3),
                 pl.BlockSpec((NSA_HEADS, N_BUCKETS), lambda b, pt, ix: (0, 0))]
    args += [q8, new8, state_rows, wnew4, small_s.reshape(DB, 1, LANES), oc, tblt]
    return pl.pallas_call(
        functools.partial(_dec_selwin_kernel, n_sel=n_sel, qpos=qpos, jb_new=jb_new, w_buf=w_buf),
        grid_spec=pltpu.PrefetchScalarGridSpec(
            num_scalar_prefetch=2,
            grid=(DB,),
            in_specs=in_specs,
            out_specs=pl.BlockSpec((1, NSA_HEADS, HD), row3)),
        out_shape=jax.ShapeDtypeStruct((DB, NSA_HEADS, HD), F32),
        compiler_params=_cparams(("arbitrary",)),
        name="dec_selwin",
    )(page_table, idx, *args)


def _dec_fox_kernel(pt_ref, *refs, npg):
    kv_refs = refs[:npg]
    lf_refs = refs[npg:2 * npg]
    q_ref, new_ref, lfn_ref, o_ref, m_sc, l_sc, acc_sc, car_sc = refs[2 * npg:]
    p = pl.program_id(1)
    nh = FOX_HEADS
    page = kv_refs[0].shape[0]
    q8 = q_ref[0]
    qs = q8 * SCALE

    @pl.when(p == 0)
    def _():
        new = new_ref[0]
        s_new = jnp.sum(qs * new[0:nh], axis=1, keepdims=True)
        m_sc[...] = jnp.broadcast_to(s_new, (nh, LANES))
        l_sc[...] = jnp.ones((nh, LANES), F32)
        acc_sc[...] = new[nh:2 * nh]
        lane1 = _iota((nh, LANES), 1)
        row1 = _iota((nh, LANES), 0)
        car_sc[...] = jnp.sum(jnp.where(lane1 == LOGF_LANE0 + row1, lfn_ref[0], 0.0), axis=1, keepdims=True)

    def tree(x, op):
        while x.shape[0] > 1:
            half = x.shape[0] // 2
            x = op(x[:half], x[half:])
        return x[0]

    uu = _iota((LANES, LANES), 0)
    ss = _iota((LANES, LANES), 1)
    later = jnp.where(uu > ss, 1.0, 0.0).astype(BF16)
    ones = jnp.ones((HD, LANES), BF16)
    tok3 = _iota((page, nh, LANES), 0)
    lane3 = _iota((page, nh, LANES), 2)
    m = m_sc[...]
    l = l_sc[...]
    acc = acc_sc[...]
    car = car_sc[...]
    for i in range(npg):
        lft = lf_refs[i][...]
        decay = _dot_split_lhs(lft, later) + car
        lhs = jnp.where(lane3 == tok3, decay[None], 0.0) + kv_refs[i][:, 0] * qs[None]
        s = _dot(lhs.reshape(page * nh, HD).astype(BF16), ones).reshape(page, nh, LANES)
        m_new = jnp.maximum(m, tree(s, jnp.maximum))
        alpha = jnp.exp(m - m_new)
        pp = jnp.exp(s - m_new[None])
        l = alpha * l + tree(pp, jnp.add)
        acc = alpha * acc + tree(pp * kv_refs[i][:, 1], jnp.add)
        m = m_new
        car = car + jnp.sum(lft, axis=1, keepdims=True)
    m_sc[...] = m
    l_sc[...] = l
    acc_sc[...] = acc
    car_sc[...] = car

    @pl.when(p == pl.num_programs(1) - 1)
    def _():
        o_ref[0] = acc / l


def dec_fox(q8, new16, small_s, cache_fox_kv, logf_t, page_table, layer):
    DB = q8.shape[0]
    n_pool, depth, page = cache_fox_kv.shape[:3]
    assert page == LANES
    n_pages = page_table.shape[1]
    npg = _pick(n_pages, (8, 4, 2, 1))
    in_specs = []
    for i in range(npg):
        in_specs.append(pl.BlockSpec((None, None, page, 2, FOX_HEADS, HD),
                                     lambda b, p, pt, i=i: (pt[b, n_pages - 1 - (p * npg + i)], layer, 0, 0, 0, 0)))
    for i in range(npg):
        in_specs.append(pl.BlockSpec((None, None, FOX_HEADS, page),
                                     lambda b, p, pt, i=i: (pt[b, n_pages - 1 - (p * npg + i)], layer, 0, 0)))
    row3 = lambda b, p, pt: (b, 0, 0)
    in_specs += [pl.BlockSpec((1, FOX_HEADS, HD), row3), pl.BlockSpec((1, 2 * FOX_HEADS, HD), row3),
                 pl.BlockSpec((1, 1, LANES), row3)]
    return pl.pallas_call(
        functools.partial(_dec_fox_kernel, npg=npg),
        grid_spec=pltpu.PrefetchScalarGridSpec(
            num_scalar_prefetch=1,
            grid=(DB, n_pages // npg),
            in_specs=in_specs,
            out_specs=pl.BlockSpec((1, FOX_HEADS, HD), row3),
            scratch_shapes=[pltpu.VMEM((FOX_HEADS, LANES), F32), pltpu.VMEM((FOX_HEADS, LANES), F32),
                            pltpu.VMEM((FOX_HEADS, HD), F32), pltpu.VMEM((FOX_HEADS, 1), F32)]),
        out_shape=jax.ShapeDtypeStruct((DB, FOX_HEADS, HD), F32),
        compiler_params=_cparams(("parallel", "arbitrary")),
        name="dec_fox",
    )(page_table, *([cache_fox_kv] * npg), *([logf_t] * npg), q8, new16, small_s.reshape(DB, 1, LANES))


def _layer_weights(l, D, w_in, b_forget, qk_gain_nsa, qk_gain_fox):
    nq = NSA_HEADS * HD
    nkv = 6 * G * HD
    nfox = 3 * FOX_HEADS * HD
    o_br = nq + nkv
    o_fox = o_br + BR_LANES
    o_f = o_fox + nfox
    o_mg = o_f + FOX_HEADS
    w = w_in[l]
    ones = jnp.ones((HD,), F32)
    gn = qk_gain_nsa[l]
    gf = qk_gain_fox[l]

    def gains(rows):
        return jnp.concatenate(rows).reshape(1, -1)

    zeros = lambda n: jnp.zeros((1, n), F32)
    small_b = jnp.concatenate([jnp.zeros((BR_LANES,), F32), b_forget[l],
                               jnp.zeros((LANES - BR_LANES - FOX_HEADS,), F32)]).reshape(1, LANES)
    kvh = G
    tg = _pick(2 * D, (1024, 512, 256, 128))
    table = [
        ("foxkv", [(o_fox + FOX_HEADS * HD, o_f)], gains([gf[1]] * FOX_HEADS + [ones] * FOX_HEADS),
         zeros(2 * FOX_HEADS * HD), ("norm",) * FOX_HEADS + ("raw",) * FOX_HEADS, 2 * FOX_HEADS * HD),
        ("qa", [(0, nq)], gains([gn[0]] * NSA_HEADS), zeros(nq), ("norm",) * NSA_HEADS, nq),
        ("nsakv", [(nq, nq + 4 * kvh * HD)], gains([ones] * (2 * kvh) + [gn[2]] * kvh + [ones] * kvh),
         zeros(4 * kvh * HD), ("raw",) * (2 * kvh) + ("norm",) * kvh + ("raw",) * kvh, 4 * kvh * HD),
        ("qf", [(o_fox, o_fox + FOX_HEADS * HD)], gains([gf[0]] * FOX_HEADS), zeros(FOX_HEADS * HD),
         ("norm",) * FOX_HEADS, FOX_HEADS * HD),
        ("gates", [(o_mg, o_mg + 2 * D)], zeros(2 * D), zeros(2 * D), ("sigmoid",) * (tg // LANES), tg),
        ("win", [(nq + 4 * kvh * HD, o_br)], gains([gn[3]] * kvh + [ones] * kvh), zeros(2 * kvh * HD),
         ("norm",) * kvh + ("raw",) * kvh, 2 * kvh * HD),
        ("small", [(o_br, o_br + BR_LANES), (o_f, o_f + FOX_HEADS)], zeros(LANES), small_b, ("small",), LANES),
    ]
    pieces, segs, col = [], {}, 0
    for name, spans, gn_row, bias_row, modes, tn in table:
        width = gn_row.shape[1]
        got = 0
        for lo, hi in spans:
            pieces.append(w[:, lo:hi])
            got += hi - lo
        if got < width:
            pieces.append(jnp.zeros((D, width - got), F32))
        assert col % tn == 0
        segs[name] = (col, width, gn_row, bias_row, modes, tn)
        col += width
    return jnp.concatenate(pieces, axis=1).astype(BF16), segs


def _project_all(h, w_perm, segs):
    return {name: proj(h, w_perm, col0, n, gn, bs, modes, tn) for name, (col0, n, gn, bs, modes, tn) in segs.items()}


def _tail(x, o_nsa, o_fox, gates, pe, gains_l, gain_next, wbn, wbf, wo, wfi, wfo, wple, wpg):
    merged = merge(o_nsa, o_fox, gates, wbn, wbf)
    x1, h1 = out_proj(merged, wo, x, gains_l[1])
    act = ffn_in(h1, wfi)
    x2 = ffn_out(act, wfo, x1)
    return ple_gate(x2, gains_l[2], wpg, pe, wple, gain_next)


def kernel(x_prompt, x_sample, cache_nsa_kv, cache_fox_kv, cache_fox_logf, state_nsa_window, page_table, p_prompt, p_sample, rel_bias_table, norm_gains, w_in, b_forget, qk_gain_nsa, qk_gain_fox, cmp_pos, w_cmp1, b_cmp1, w_cmp2, w_branch_nsa, w_branch_fox, w_out, w_ffn_in, w_ffn_out, w_ple, w_ple_gate):
    B, T, D = x_prompt.shape
    DB, n_qs, _ = x_sample.shape
    assert n_qs == 1
    depth = w_in.shape[0]
    page = cache_nsa_kv.shape[2]
    n_pages = page_table.shape[1]
    past = n_pages * page
    w_buf = state_nsa_window.shape[2]
    kdim = CMP_STRIDE * HD

    xp = x_prompt.reshape(B * T, D)
    xs = x_sample.reshape(DB, D)
    b_sel, b_win = bias_sel_win(rel_bias_table)
    b_cmp = bias_cmp(rel_bias_table, T // LANES)
    fox_tk = _pick(T, (256, 128))
    tblt = rel_bias_table.T
    logf_t = jnp.swapaxes(cache_fox_logf, 2, 3)
    pt_prompt = jnp.arange(B * (T // page), dtype=I32).reshape(B, T // page)
    hp = rms_bf16(xp, norm_gains[0, 0])
    hs = rms_bf16(xs, norm_gains[0, 0])

    outs = [[] for _ in range(8)]
    for l in range(depth):
        w_perm, segs = _layer_weights(l, D, w_in, b_forget, qk_gain_nsa, qk_gain_fox)
        w1 = w_cmp1[l]
        w1cat = jnp.concatenate([w1[:, :kdim], w1[:, kdim:]], axis=2).astype(BF16)
        pos = cmp_pos[l].reshape(2, 2, 1, kdim)
        pos_ab = jnp.broadcast_to(pos, (2, 2, SUBLANES, kdim)).astype(BF16)
        cmp_args = (pos_ab, w1cat, b_cmp1[l].reshape(2, 1, HD), w_cmp2[l].astype(BF16),
                    qk_gain_nsa[l, 1].reshape(1, HD))
        gain_next = norm_gains[l + 1, 0] if l + 1 < depth else norm_gains[l, 0]
        tail_w = (norm_gains[l], gain_next, w_branch_nsa[l].astype(BF16), w_branch_fox[l].astype(BF16),
                  w_out[l].astype(BF16), w_ffn_in[l].astype(BF16), w_ffn_out[l].astype(BF16),
                  w_ple[l].astype(BF16), w_ple_gate[l].astype(BF16))

        P = _project_all(hp, w_perm, segs)
        small3 = P["small"].reshape(B, T, LANES)
        c_row, c_rep = cumsum_logf(small3)
        foxkv3 = P["foxkv"].reshape(B, T, -1)
        nfk = FOX_HEADS * HD
        o_fox_p = fox_prompt(P["qf"].reshape(B, T, -1), foxkv3[:, :, :nfk].astype(BF16),
                             key_blocks_transposed(foxkv3[:, :, nfk:], FOX_HEADS, fox_tk), c_row, c_rep)
        part = compress_partial(P["nsakv"].reshape(B * (T // page), 1, page, 4 * G * HD), pt_prompt, 0, w1cat, False)
        kc_p, vc_p = compress_finish(part, *cmp_args)
        nsakv3 = P["nsakv"].reshape(B, T, -1)
        win3 = P["win"].reshape(B, T, -1)
        ngk = G * HD
        o_nsa_p = nsa_prompt(P["qa"].reshape(B, T, -1), kc_p, jnp.swapaxes(vc_p, 2, 3),
                             nsakv3[:, :, 2 * ngk:3 * ngk].astype(BF16),
                             key_blocks_transposed(nsakv3[:, :, 3 * ngk:], G, LANES),
                             win3[:, :, :ngk].astype(BF16), key_blocks_transposed(win3[:, :, ngk:], G, LANES),
                             small3, b_cmp, b_sel, b_win)
        xp, hp = _tail(xp, o_nsa_p.reshape(B * T, -1), o_fox_p.reshape(B * T, -1), P["gates"],
                       p_prompt[l].reshape(B * T, -1), *tail_w)

        S = _project_all(hs, w_perm, segs)
        part_s = compress_partial(cache_nsa_kv.reshape(cache_nsa_kv.shape[0], depth, page, 2, 2 * G, HD),
                                  page_table, l, w1cat, True)
        kc_s, vc_s = compress_finish(part_s, *cmp_args)
        q8 = S["qa"].reshape(DB, NSA_HEADS, HD)
        oc_s, idx, n_sel = dec_cmp(q8, kc_s, vc_s, tblt, past)
        o_nsa_s = dec_selwin(q8, S["nsakv"].reshape(DB, 4 * G, HD), S["win"].reshape(DB, 2 * G, HD), S["small"],
                             oc_s, tblt, cache_nsa_kv, state_nsa_window, page_table, idx, n_sel, l, past)
        o_fox_s = dec_fox(S["qf"].reshape(DB, FOX_HEADS, HD), S["foxkv"].reshape(DB, 2 * FOX_HEADS, HD),
                          S["small"], cache_fox_kv, logf_t, page_table, l)
        xs, hs = _tail(xs, o_nsa_s.reshape(DB, -1).astype(BF16), o_fox_s.reshape(DB, -1).astype(BF16), S["gates"],
                       p_sample[l].reshape(DB, -1), *tail_w)

        lf0, lf1 = LOGF_LANE0, LOGF_LANE0 + FOX_HEADS
        outs[0].append(P["nsakv"].reshape(B, T, 4, G, HD))
        outs[1].append(P["win"].reshape(B, T, 2, G, HD)[:, T - min(WINDOW, T):])
        outs[2].append(P["foxkv"].reshape(B, T, 2, FOX_HEADS, HD))
        outs[3].append(P["small"][:, lf0:lf1].reshape(B, T, FOX_HEADS))
        outs[4].append(S["nsakv"].reshape(DB, 1, 4, G, HD))
        win_all = jnp.concatenate([state_nsa_window[l], S["win"].reshape(DB, 1, 2, G, HD)], axis=1)
        outs[5].append(win_all[:, w_buf + 1 - min(WINDOW, w_buf + 1):])
        outs[6].append(S["foxkv"].reshape(DB, 1, 2, FOX_HEADS, HD))
        outs[7].append(S["small"][:, lf0:lf1].reshape(DB, 1, FOX_HEADS))

    return (xp.reshape(B, T, D), xs.reshape(DB, 1, D)) + tuple(jnp.stack(o) for o in outs)
```

```python
import functools
import math

import jax
import jax.numpy as jnp
from jax import lax
from jax.experimental import pallas as pl
from jax.experimental.pallas import tpu as pltpu

F32 = jnp.float32
BF16 = jnp.bfloat16
I32 = jnp.int32

HEAD_DIM = 128
NSA_HEADS = 8
NSA_KV_HEADS = 2
NSA_GROUP = NSA_HEADS // NSA_KV_HEADS
FOX_HEADS = 8
CMP_LEN = 32
CMP_STRIDE = 16
SEL_BLOCK = 64
SEL_TOPN = 16
N_LOCAL_BLOCKS = 2
WINDOW = 512
N_BUCKETS = 32
BUCKET_EXACT = N_BUCKETS // 2
MAX_DISTANCE = 128
EPS = 1e-6
SCALE = HEAD_DIM ** -0.5
NEG_INF = -1e30
FORCE_BONUS = 1e4

LANES = 128
SUBLANES = 8
VMEM_LIMIT_MB = 52

G = NSA_KV_HEADS
R = NSA_GROUP
HD = HEAD_DIM
BR_LANES = 3 * NSA_HEADS
LOGF_LANE0 = BR_LANES
CHUNKS_PER_PAGE = 128 // CMP_STRIDE


def _cparams(sem, vmem_mb=VMEM_LIMIT_MB):
    return pltpu.CompilerParams(dimension_semantics=sem, vmem_limit_bytes=vmem_mb * 1024 * 1024)


def _pick(n, prefs):
    for p in prefs:
        if n % p == 0:
            return p
    return n


def _iota(shape, dim):
    return lax.broadcasted_iota(I32, shape, dim)


def _dot(a, b):
    return jnp.dot(a, b, preferred_element_type=F32)


def _dot_nt(a, b):
    return lax.dot_general(a, b, (((1,), (1,)), ((), ())), preferred_element_type=F32)


def _split3(x):
    hi = x.astype(BF16)
    r1 = x - hi.astype(F32)
    mid = r1.astype(BF16)
    lo = (r1 - mid.astype(F32)).astype(BF16)
    return hi, mid, lo


def _dot_split_lhs(x, m01):
    hi, mid, lo = _split3(x)
    return _dot(hi, m01) + _dot(mid, m01) + _dot(lo, m01)


def _dot_split_rhs(m01, x):
    hi, mid, lo = _split3(x)
    return _dot(m01, hi) + _dot(m01, mid) + _dot(m01, lo)


def _rms(x, gain):
    return x * lax.rsqrt(jnp.mean(x * x, axis=-1, keepdims=True) + EPS) * gain


def _bucket(dist):
    n = jnp.maximum(dist, 0)
    rel = jnp.log(jnp.maximum(n, 1).astype(F32) / float(BUCKET_EXACT)) / math.log(MAX_DISTANCE / BUCKET_EXACT)
    large = jnp.minimum(BUCKET_EXACT + (rel * float(N_BUCKETS - BUCKET_EXACT)).astype(I32), N_BUCKETS - 1)
    return jnp.where(n < BUCKET_EXACT, n, large)


def _lookup(bucket, vals):
    out = vals[N_BUCKETS - 1]
    for k in range(N_BUCKETS - 1):
        out = jnp.where(bucket == k, vals[k], out)
    return out


def _rms_kernel(x_ref, g_ref, o_ref):
    o_ref[...] = _rms(x_ref[...], g_ref[...]).astype(o_ref.dtype)


def rms_bf16(x, gain):
    M, D = x.shape
    tm = _pick(M, (512, 256, 128, 8))
    return pl.pallas_call(
        _rms_kernel,
        grid=(M // tm,),
        in_specs=[pl.BlockSpec((tm, D), lambda i: (i, 0)), pl.BlockSpec((1, D), lambda i: (0, 0))],
        out_specs=pl.BlockSpec((tm, D), lambda i: (i, 0)),
        out_shape=jax.ShapeDtypeStruct((M, D), BF16),
        compiler_params=_cparams(("parallel",)),
        name="rms",
    )(x, gain.reshape(1, D))


def _proj_kernel(h_ref, w_ref, g_ref, b_ref, o_ref, *maybe_bf16_ref, modes):
    h = h_ref[...]
    nt = len(modes)
    t = 0
    while t < nt:
        wd = 2 if t + 1 < nt else 1
        acc = _dot(h, w_ref[:, t * LANES:(t + wd) * LANES])
        for u in range(wd):
            c0 = (t + u) * LANES
            v = acc[:, u * LANES:(u + 1) * LANES]
            mode = modes[t + u]
            if mode == "norm":
                v = _rms(v, g_ref[:, c0:c0 + LANES])
            elif mode == "sigmoid":
                v = jax.nn.sigmoid(v)
            elif mode == "small":
                lane = _iota(v.shape, 1)
                z = v + b_ref[:, c0:c0 + LANES]
                logsig = jnp.minimum(z, 0.0) - jnp.log(1.0 + jnp.exp(-jnp.abs(z)))
                v = jnp.where(lane < BR_LANES, jax.nn.sigmoid(v),
                              jnp.where(lane < LOGF_LANE0 + FOX_HEADS, logsig, 0.0))
            o_ref[:, c0:c0 + LANES] = v
            for ob_ref in maybe_bf16_ref:
                ob_ref[:, c0:c0 + LANES] = v.astype(ob_ref.dtype)
        t += wd


def proj(h, w, col0, N, gains, bias, modes, tn, with_bf16_copy=False):
    M, K = h.shape
    out_bytes = 4 + (2 if with_bf16_copy else 0)
    budget = VMEM_LIMIT_MB * 1024 * 1024 * 4 // 5
    fits = [t for t in (1024, 512, 256, 128, 8)
            if M % t == 0 and 2 * (t * K * 2 + K * tn * 2 + t * tn * out_bytes) <= budget]
    tm = fits[0] if fits else M
    assert len(modes) * LANES == tn and N % tn == 0 and col0 % tn == 0
    jb0 = col0 // tn
    out_spec = pl.BlockSpec((tm, tn), lambda i, j: (i, j))
    f32_out = jax.ShapeDtypeStruct((M, N), F32)
    return pl.pallas_call(
        functools.partial(_proj_kernel, modes=modes),
        grid=(M // tm, N // tn),
        in_specs=[pl.BlockSpec((tm, K), lambda i, j: (i, 0)),
                  pl.BlockSpec((K, tn), lambda i, j: (0, jb0 + j)),
                  pl.BlockSpec((1, tn), lambda i, j: (0, j)),
                  pl.BlockSpec((1, tn), lambda i, j: (0, j))],
        out_specs=[out_spec, out_spec] if with_bf16_copy else out_spec,
        out_shape=[f32_out, jax.ShapeDtypeStruct((M, N), BF16)] if with_bf16_copy else f32_out,
        compiler_params=_cparams(("parallel", "arbitrary")),
        name="proj",
    )(h, w, gains, bias)


def _merge_kernel(on_ref, of_ref, ga_ref, gb_ref, wn_ref, wf_ref, o_ref):
    a = _dot(on_ref[...], wn_ref[...])
    b = _dot(of_ref[...], wf_ref[...])
    o_ref[...] = (ga_ref[...] * a + gb_ref[...] * b).astype(o_ref.dtype)


def merge(o_nsa, o_fox, gates, wbn, wbf):
    M, K = o_nsa.shape
    D = wbn.shape[1]
    tm = _pick(M, (1024, 512, 256, 128, 8))
    tn = _pick(D, (1024, 512, 256, 128))
    nj = D // tn
    return pl.pallas_call(
        _merge_kernel,
        grid=(M // tm, nj),
        in_specs=[pl.BlockSpec((tm, K), lambda i, j: (i, 0)),
                  pl.BlockSpec((tm, K), lambda i, j: (i, 0)),
                  pl.BlockSpec((tm, tn), lambda i, j: (i, j)),
                  pl.BlockSpec((tm, tn), lambda i, j: (i, j + nj)),
                  pl.BlockSpec((K, tn), lambda i, j: (0, j)),
                  pl.BlockSpec((K, tn), lambda i, j: (0, j))],
        out_specs=pl.BlockSpec((tm, tn), lambda i, j: (i, j)),
        out_shape=jax.ShapeDtypeStruct((M, D), BF16),
        compiler_params=_cparams(("parallel", "arbitrary")),
        name="merge",
    )(o_nsa, o_fox, gates, gates, wbn, wbf)


def _wo_kernel(m_ref, w_ref, x_ref, g_ref, x1_ref, h1_ref):
    x1 = x_ref[...] + _dot(m_ref[...], w_ref[...])
    x1_ref[...] = x1
    h1_ref[...] = _rms(x1, g_ref[...]).astype(h1_ref.dtype)


def out_proj(merged, wo, x, gain1):
    M, D = x.shape
    tm = _pick(M, (512, 256, 128, 8))
    return pl.pallas_call(
        _wo_kernel,
        grid=(M // tm,),
        in_specs=[pl.BlockSpec((tm, D), lambda i: (i, 0)),
                  pl.BlockSpec((D, D), lambda i: (0, 0)),
                  pl.BlockSpec((tm, D), lambda i: (i, 0)),
                  pl.BlockSpec((1, D), lambda i: (0, 0))],
        out_specs=[pl.BlockSpec((tm, D), lambda i: (i, 0)), pl.BlockSpec((tm, D), lambda i: (i, 0))],
        out_shape=[jax.ShapeDtypeStruct((M, D), F32), jax.ShapeDtypeStruct((M, D), BF16)],
        compiler_params=_cparams(("parallel",)),
        name="out_proj",
    )(merged, wo, x, gain1.reshape(1, D))


def _ffn_in_kernel(h_ref, wg_ref, wu_ref, o_ref):
    h = h_ref[...]
    gt = _dot(h, wg_ref[...])
    up = _dot(h, wu_ref[...])
    o_ref[...] = (gt * jax.nn.sigmoid(gt) * up).astype(o_ref.dtype)


def ffn_in(h1, wfi):
    M, D = h1.shape
    F = wfi.shape[1] // 2
    tm = _pick(M, (1024, 512, 256, 128, 8))
    tn = _pick(F, (512, 256, 128))
    nj = F // tn
    return pl.pallas_call(
        _ffn_in_kernel,
        grid=(M // tm, nj),
        in_specs=[pl.BlockSpec((tm, D), lambda i, j: (i, 0)),
                  pl.BlockSpec((D, tn), lambda i, j: (0, j)),
                  pl.BlockSpec((D, tn), lambda i, j: (0, j + nj))],
        out_specs=pl.BlockSpec((tm, tn), lambda i, j: (i, j)),
        out_shape=jax.ShapeDtypeStruct((M, F), BF16),
        compiler_params=_cparams(("parallel", "arbitrary")),
        name="ffn_in",
    )(h1, wfi, wfi)


def _ffn_out_kernel(a_ref, w_ref, x_ref, o_ref):
    o_ref[...] = x_ref[...] + _dot(a_ref[...], w_ref[...])


def ffn_out(act, wfo, x1):
    M, F = act.shape
    D = wfo.shape[1]
    tm = _pick(M, (1024, 512, 256, 128, 8))
    tn = _pick(D, (512, 256, 128))
    return pl.pallas_call(
        _ffn_out_kernel,
        grid=(M // tm, D // tn),
        in_specs=[pl.BlockSpec((tm, F), lambda i, j: (i, 0)),
                  pl.BlockSpec((F, tn), lambda i, j: (0, j)),
                  pl.BlockSpec((tm, tn), lambda i, j: (i, j))],
        out_specs=pl.BlockSpec((tm, tn), lambda i, j: (i, j)),
        out_shape=jax.ShapeDtypeStruct((M, D), F32),
        compiler_params=_cparams(("parallel", "arbitrary")),
        name="ffn_out",
    )(act, wfo, x1)


def _ple_kernel(x_ref, g_ref, wpg_ref, pe_ref, wple_ref, gn_ref, o_ref, hn_ref):
    x = x_ref[...]
    h2 = _rms(x, g_ref[...]).astype(BF16)
    gate = jax.nn.sigmoid(_dot(h2, wpg_ref[...]))
    x3 = x + gate * _dot(pe_ref[...].astype(BF16), wple_ref[...])
    o_ref[...] = x3
    hn_ref[...] = _rms(x3, gn_ref[...]).astype(hn_ref.dtype)


def ple_gate(x2, gain2, wpg, pe, wple, gain_next):
    M, D = x2.shape
    Pd = pe.shape[1]
    tm = _pick(M, (512, 256, 128, 8))
    return pl.pallas_call(
        _ple_kernel,
        grid=(M // tm,),
        in_specs=[pl.BlockSpec((tm, D), lambda i: (i, 0)),
                  pl.BlockSpec((1, D), lambda i: (0, 0)),
                  pl.BlockSpec((D, D), lambda i: (0, 0)),
                  pl.BlockSpec((tm, Pd), lambda i: (i, 0)),
                  pl.BlockSpec((Pd, D), lambda i: (0, 0)),
                  pl.BlockSpec((1, D), lambda i: (0, 0))],
        out_specs=[pl.BlockSpec((tm, D), lambda i: (i, 0)), pl.BlockSpec((tm, D), lambda i: (i, 0))],
        out_shape=[jax.ShapeDtypeStruct((M, D), F32), jax.ShapeDtypeStruct((M, D), BF16)],
        compiler_params=_cparams(("parallel",)),
        name="ple_gate",
    )(x2, gain2.reshape(1, D), wpg, pe, wple, gain_next.reshape(1, D))


SEL_BIAS_KINDS = 3
WIN_BIAS_KINDS = WINDOW // LANES + 2


def _bias_sw_kernel(tbl_ref, bs_ref, bw_ref):
    g = pl.program_id(0)
    kk = _iota((LANES, LANES), 0)
    qq = _iota((LANES, LANES), 1)
    bw_ref[0, WIN_BIAS_KINDS - 1] = jnp.full((LANES, R * LANES), NEG_INF, F32)
    for off in range(WIN_BIAS_KINDS - 1):
        d = off * LANES + qq - kk
        bkt = _bucket(d)
        in_window = (d >= 0) & (d < WINDOW)
        for r in range(R):
            cols = slice(r * LANES, (r + 1) * LANES)
            b = _lookup(bkt, [tbl_ref[k, g * R + r] for k in range(N_BUCKETS)])
            bw_ref[0, off, :, cols] = jnp.where(in_window, b, NEG_INF)
            if off < SEL_BIAS_KINDS - 1:
                bs_ref[0, off, :, cols] = b
    for r in range(R):
        bs_ref[0, SEL_BIAS_KINDS - 1, :, r * LANES:(r + 1) * LANES] = jnp.full(
            (LANES, LANES), tbl_ref[N_BUCKETS - 1, g * R + r], F32)


def bias_sel_win(table):
    shapes = [(G, SEL_BIAS_KINDS, LANES, R * LANES), (G, WIN_BIAS_KINDS, LANES, R * LANES)]
    return pl.pallas_call(
        _bias_sw_kernel,
        grid=(G,),
        in_specs=[pl.BlockSpec(memory_space=pltpu.SMEM)],
        out_specs=[pl.BlockSpec((1,) + s[1:], lambda g: (g, 0, 0, 0)) for s in shapes],
        out_shape=[jax.ShapeDtypeStruct(s, F32) for s in shapes],
        compiler_params=_cparams(("arbitrary",)),
        name="bias_sel_win",
    )(table)


def _bias_cmp_kernel(tbl_ref, bc_ref):
    g = pl.program_id(0)
    qi = pl.program_id(1)
    cc = _iota((LANES, LANES), 0)
    qq = _iota((LANES, LANES), 1)
    d = qi * LANES + qq - (cc * CMP_STRIDE + (CMP_LEN - 1))
    bkt = _bucket(d)
    for r in range(R):
        b = _lookup(bkt, [tbl_ref[k, g * R + r] for k in range(N_BUCKETS)])
        bc_ref[0, 0, :, r * LANES:(r + 1) * LANES] = jnp.where(d >= 0, b, NEG_INF)


def bias_cmp(table, n_qblk):
    return pl.pallas_call(
        _bias_cmp_kernel,
        grid=(G, n_qblk),
        in_specs=[pl.BlockSpec(memory_space=pltpu.SMEM)],
        out_specs=pl.BlockSpec((1, 1, LANES, R * LANES), lambda g, i: (g, i, 0, 0)),
        out_shape=jax.ShapeDtypeStruct((G, n_qblk, LANES, R * LANES), F32),
        compiler_params=_cparams(("arbitrary", "arbitrary")),
        name="bias_cmp",
    )(table)


def _cumsum_kernel(x_ref, cr_ref, cp_ref):
    T = x_ref.shape[1]
    ii = _iota((LANES, LANES), 0)
    jj = _iota((LANES, LANES), 1)
    tri = jnp.where(jj <= ii, 1.0, 0.0).astype(BF16)
    carry = jnp.zeros((1, LANES), F32)
    for blk in range(T // LANES):
        sl = slice(blk * LANES, (blk + 1) * LANES)
        c = _dot_split_rhs(tri, x_ref[0, sl, :]) + carry
        cr_ref[0, :, sl] = c.T[LOGF_LANE0:LOGF_LANE0 + FOX_HEADS, :]
        for h in range(FOX_HEADS):
            cp_ref[0, h, sl, :] = jnp.broadcast_to(c[:, LOGF_LANE0 + h:LOGF_LANE0 + h + 1], (LANES, LANES))
        carry = c[LANES - 1:LANES, :]


def cumsum_logf(small):
    B, T, _ = small.shape
    return pl.pallas_call(
        _cumsum_kernel,
        grid=(B,),
        in_specs=[pl.BlockSpec((1, T, LANES), lambda b: (b, 0, 0))],
        out_specs=[pl.BlockSpec((1, FOX_HEADS, T), lambda b: (b, 0, 0)),
                   pl.BlockSpec((1, FOX_HEADS, T, LANES), lambda b: (b, 0, 0, 0))],
        out_shape=[jax.ShapeDtypeStruct((B, FOX_HEADS, T), F32),
                   jax.ShapeDtypeStruct((B, FOX_HEADS, T, LANES), F32)],
        compiler_params=_cparams(("parallel",)),
        name="cumsum_logf",
    )(small)


def key_blocks_transposed(x, n_heads, blk):
    B, T, _ = x.shape
    return x.reshape(B, T // blk, blk, n_heads, HD).transpose(0, 3, 1, 4, 2).astype(BF16)


def _fox_prompt_kernel(q_ref, k_ref, vt_ref, cq_ref, ck_ref, o_ref, *, tk):
    tq = LANES
    qi = pl.program_id(1)
    nh = FOX_HEADS
    qf = q_ref[0]
    qs = [(qf[:, h * HD:(h + 1) * HD] * SCALE).astype(BF16) for h in range(nh)]
    kk = _iota((tk, tq), 0)
    qq = _iota((tk, tq), 1)

    def step(kb, carry, diagonal):
        k0 = pl.multiple_of(kb * tk, tk)
        scores = [_dot_nt(k_ref[0, pl.ds(k0, tk), h * HD:(h + 1) * HD], qs[h]) for h in range(nh)]
        probs = []
        for h in range(nh):
            m, l, _ = carry[h]
            s = scores[h] + (cq_ref[0, h, pl.ds(qi, 1), :] - ck_ref[0, h, pl.ds(k0, tk), :])
            if diagonal:
                s = jnp.where(k0 + kk <= qi * tq + qq, s, NEG_INF)
            m_new = jnp.maximum(m, jnp.max(s, axis=0, keepdims=True))
            alpha = jnp.exp(m - m_new)
            p = jnp.exp(s - m_new)
            probs.append((m_new, alpha, alpha * l + jnp.sum(p, axis=0, keepdims=True), p.astype(BF16)))
        out = []
        for h in range(nh):
            m_new, alpha, l, p = probs[h]
            out.append((m_new, l, alpha * carry[h][2] + _dot(vt_ref[0, h, kb], p)))
        return tuple(out)

    n_full = lax.div(qi * tq, tk)
    n_all = lax.div(qi * tq + tq + tk - 1, tk)
    one = (jnp.full((1, tq), NEG_INF, F32), jnp.zeros((1, tq), F32), jnp.zeros((HD, tq), F32))
    carry = lax.fori_loop(0, n_full, lambda kb, c: step(kb, c, False), (one,) * nh)
    res = lax.fori_loop(n_full, n_all, lambda kb, c: step(kb, c, True), carry)
    for h in range(nh):
        _, l, acc = res[h]
        o_ref[0, :, h * HD:(h + 1) * HD] = (acc / l).T.astype(o_ref.dtype)


def fox_prompt(qf, k_bf, v_t, c_row, c_rep):
    B, T, W = qf.shape
    tq = LANES
    tk = v_t.shape[-1]
    assert T % tk == 0 and tk % tq == 0
    c_row4 = c_row.reshape(B, FOX_HEADS, T // tq, tq)
    return pl.pallas_call(
        functools.partial(_fox_prompt_kernel, tk=tk),
        grid=(B, T // tq),
        in_specs=[pl.BlockSpec((1, tq, W), lambda b, i: (b, i, 0)),
                  pl.BlockSpec((1, T, W), lambda b, i: (b, 0, 0)),
                  pl.BlockSpec((1, FOX_HEADS, T // tk, HD, tk), lambda b, i: (b, 0, 0, 0, 0)),
                  pl.BlockSpec((1, FOX_HEADS, T // tq, tq), lambda b, i: (b, 0, 0, 0)),
                  pl.BlockSpec((1, FOX_HEADS, T, LANES), lambda b, i: (b, 0, 0, 0))],
        out_specs=pl.BlockSpec((1, tq, W), lambda b, i: (b, i, 0)),
        out_shape=jax.ShapeDtypeStruct((B, T, W), BF16),
        compiler_params=_cparams(("parallel", "arbitrary")),
        name="fox_prompt",
    )(qf, k_bf, v_t, c_row4, c_rep)


def _cmp_p_kernel(pt_ref, *refs, npg, row_per_head):
    nref = npg if row_per_head else npg * 2 * G
    page_refs = refs[:nref]
    w_ref, o_ref, lhs_ref = refs[nref:]
    for i in range(npg):
        for c in range(2 * G):
            for t in range(CMP_STRIDE):
                if row_per_head:
                    rows = page_refs[i][pl.ds(t, CHUNKS_PER_PAGE, stride=CMP_STRIDE), c, :]
                else:
                    rows = page_refs[i * 2 * G + c][pl.ds(t, CHUNKS_PER_PAGE, stride=CMP_STRIDE), :]
                lhs_ref[c, i * CHUNKS_PER_PAGE:(i + 1) * CHUNKS_PER_PAGE, t * HD:(t + 1) * HD] = rows
    for kv in range(2):
        for g in range(G):
            o_ref[0, kv, g] = _dot(lhs_ref[kv * G + g].astype(BF16), w_ref[kv])


def compress_partial(pages, page_table, layer, w1cat, row_per_head):
    NB, n_pages = page_table.shape
    npg = _pick(n_pages, (16, 8, 4, 2, 1))
    rows = npg * CHUNKS_PER_PAGE
    kdim = CMP_STRIDE * HD
    in_specs = []
    for i in range(npg):
        if row_per_head:
            in_specs.append(pl.BlockSpec((None, None, 128, None, 2 * G, HD),
                                         lambda b, p, pt, i=i: (pt[b, p * npg + i], layer, 0, 0, 0, 0)))
            continue
        for c in range(2 * G):
            in_specs.append(pl.BlockSpec((None, None, 128, HD),
                                         lambda b, p, pt, i=i, c=c: (pt[b, p * npg + i], layer, 0, c)))
    in_specs.append(pl.BlockSpec((2, kdim, 2 * HD), lambda b, p, pt: (0, 0, 0)))
    return pl.pallas_call(
        functools.partial(_cmp_p_kernel, npg=npg, row_per_head=row_per_head),
        grid_spec=pltpu.PrefetchScalarGridSpec(
            num_scalar_prefetch=1,
            grid=(NB, n_pages // npg),
            in_specs=in_specs,
            out_specs=pl.BlockSpec((1, 2, G, rows, 2 * HD), lambda b, p, pt: (b, 0, 0, p, 0)),
            scratch_shapes=[pltpu.VMEM((2 * G, rows, kdim), F32)]),
        out_shape=jax.ShapeDtypeStruct((NB, 2, G, n_pages * CHUNKS_PER_PAGE, 2 * HD), F32),
        compiler_params=_cparams(("parallel", "arbitrary")),
        name="compress_partial",
    )(page_table, *([pages] * len(in_specs[:-1])), w1cat)


def _gelu_tanh(x):
    return x * (0.5 * (1.0 + jnp.tanh(math.sqrt(2.0 / math.pi) * (x + 0.044715 * (x * x * x)))))


def _cmp_fin_kernel(p_ref, pos_ref, w1_ref, b1_ref, w2_ref, gk_ref, kc_ref, vc_ref):
    NC = p_ref.shape[3]
    row = _iota((NC, HD), 0)
    for kv in range(2):
        w1 = w1_ref[kv]
        posterm = (_dot(pos_ref[kv, 0], w1[:, :HD]) + _dot(pos_ref[kv, 1], w1[:, HD:]))[0:1, :] + b1_ref[kv]
        for g in range(G):
            P = p_ref[0, kv, g]
            pre = P[:, :HD] + pltpu.roll(P[:, HD:], NC - 1, 0) + posterm
            o = _dot(_gelu_tanh(pre).astype(BF16), w2_ref[kv])
            if kv == 0:
                o = _rms(o, gk_ref[...])
            o = jnp.where(row < NC - 1, o, 0.0)
            if kv == 0:
                kc_ref[0, g] = o
            else:
                vc_ref[0, g] = o


def compress_finish(part, pos_ab, w1cat, b1, w2, gain_kc):
    NB, _, _, NC, _ = part.shape
    kdim = CMP_STRIDE * HD
    full = lambda shape: pl.BlockSpec(shape, lambda b: (0,) * len(shape))
    return pl.pallas_call(
        _cmp_fin_kernel,
        grid=(NB,),
        in_specs=[pl.BlockSpec((1, 2, G, NC, 2 * HD), lambda b: (b, 0, 0, 0, 0)),
                  full((2, 2, SUBLANES, kdim)), full((2, kdim, 2 * HD)), full((2, 1, HD)),
                  full((2, HD, HD)), full((1, HD))],
        out_specs=[pl.BlockSpec((1, G, NC, HD), lambda b: (b, 0, 0, 0)),
                   pl.BlockSpec((1, G, NC, HD), lambda b: (b, 0, 0, 0))],
        out_shape=[jax.ShapeDtypeStruct((NB, G, NC, HD), F32), jax.ShapeDtypeStruct((NB, G, NC, HD), F32)],
        compiler_params=_cparams(("parallel",)),
        name="compress_finish",
    )(part, pos_ab, w1cat, b1, w2, gain_kc)


def _sel_overlap_matrix(nc, nsb, n_cb):
    ci = _iota((nc, nsb), 0)
    jb = _iota((nc, nsb), 1)
    ratio = SEL_BLOCK // CMP_STRIDE
    first = ratio * jb - (CMP_LEN // CMP_STRIDE) + 1
    n_ov = (SEL_BLOCK + CMP_LEN) // CMP_STRIDE - 1
    hit = (ci >= first) & (ci < first + n_ov) & (ci < n_cb)
    return jnp.where(hit, 1.0, 0.0).astype(BF16)


def _sel_overlap_matrix_t(nsb, nc, n_cb):
    jb = _iota((nsb, nc), 0)
    ci = _iota((nsb, nc), 1)
    first = (SEL_BLOCK // CMP_STRIDE) * jb - (CMP_LEN // CMP_STRIDE) + 1
    n_ov = (SEL_BLOCK + CMP_LEN) // CMP_STRIDE - 1
    hit = (ci >= first) & (ci < first + n_ov) & (ci < n_cb)
    return jnp.where(hit, 1.0, 0.0).astype(BF16)


def _div_pow2(x, d):
    if isinstance(x, int):
        return x // d
    return lax.shift_right_arithmetic(x, jnp.full(x.shape, d.bit_length() - 1, I32))


def _block_scores(imp_s, qpos, jb, n_sb):
    back = _div_pow2(qpos, SEL_BLOCK) - jb
    forced = (jb == 0) | ((back >= 0) & (back < N_LOCAL_BLOCKS))
    score = jnp.where(back >= 0, imp_s + jnp.where(forced, FORCE_BONUS, 0.0), -1.0)
    return jnp.where(jb < n_sb, score, -2.0)


def _nsa_prompt_kernel(q_ref, kc_ref, vct_ref, ks_ref, vst_ref, kw_ref, vwt_ref, br_ref, bc_ref, bs_ref, bw_ref,
                       exp_ref, o_ref, msk_ref, oc_ref, brt_ref, *, n_cb, n_sb, n_sel):
    g = pl.program_id(1)
    qi = pl.program_id(2)
    tq = LANES
    qf = q_ref[0]
    qt = jnp.concatenate([(qf[:, r * HD:(r + 1) * HD] * SCALE).T for r in range(R)], axis=1).astype(BF16)
    qpos = qi * tq + _iota((1, tq), 1)
    heads = lambda x: jnp.concatenate([x] * R, axis=1)

    has_c = heads(qpos >= CMP_LEN - 1)
    s = _dot(kc_ref[0, 0].astype(BF16), qt) + bc_ref[0, 0]
    e = jnp.exp(s - jnp.max(s, axis=0, keepdims=True))
    p = e * jnp.where(has_c, 1.0 / jnp.sum(e, axis=0, keepdims=True), 0.0)
    oc_ref[...] = _dot(vct_ref[0, 0].astype(BF16), p.astype(BF16))
    imp_c = p[:, 0:tq]
    for r in range(1, R):
        imp_c = imp_c + p[:, r * tq:(r + 1) * tq]

    nsb8 = -(-n_sb // SUBLANES) * SUBLANES
    imp_s = _dot_split_rhs(_sel_overlap_matrix_t(LANES, LANES, n_cb), imp_c)[0:nsb8]
    jb = _iota((nsb8, tq), 0)
    score = _block_scores(imp_s, qpos, jb, n_sb)
    rank = jnp.zeros((nsb8, tq), I32)
    for i in range(n_sb):
        row = score[i:i + 1, :]
        rank = rank + jnp.where(row > score, 1, jnp.where((row == score) & (jb > i), 1, 0))
    sel = jnp.where(rank < n_sel, 1.0, 0.0)
    if nsb8 < LANES:
        sel = jnp.concatenate([sel, jnp.zeros((LANES - nsb8, tq), F32)], axis=0)
    sel = sel.astype(BF16)

    n_keys = msk_ref.shape[0]
    chosen = _dot(exp_ref[...], sel)
    key_pos = _iota((n_keys, tq), 0)
    msk_ref[...] = jnp.where((chosen > 0.5) & (qi * tq + _iota((n_keys, tq), 1) >= key_pos), 0.0, NEG_INF)

    def branch(k_ref, vt_ref, b_ref, kind_of, masked):
        def body(j, carry):
            m, l, acc = carry
            k0 = pl.multiple_of(j * (2 * LANES), 2 * LANES)
            off = qi - 2 * j
            bias = jnp.concatenate([b_ref[0, kind_of(off)], b_ref[0, kind_of(off - 1)]], axis=0)
            s = _dot(k_ref[0, pl.ds(k0, 2 * LANES), :], qt) + bias
            if masked:
                s = s + heads(msk_ref[pl.ds(k0, 2 * LANES), :])
            m_new = jnp.maximum(m, jnp.max(s, axis=0, keepdims=True))
            alpha = jnp.exp(m - m_new)
            p = jnp.exp(s - m_new)
            l = alpha * l + jnp.sum(p, axis=0, keepdims=True)
            vblk = jnp.concatenate([vt_ref[0, 0, 2 * j], vt_ref[0, 0, 2 * j + 1]], axis=1)
            acc = alpha * acc + _dot(vblk, p.astype(BF16))
            return m_new, l, acc

        return body

    n_win = WINDOW // LANES
    sel_trip = branch(ks_ref, vst_ref, bs_ref, lambda off: jnp.clip(off, 0, SEL_BIAS_KINDS - 1), True)
    win_trip = branch(kw_ref, vwt_ref, bw_ref, lambda off: jnp.where((off < 0) | (off > n_win), n_win + 1, off),
                      False)
    init = (jnp.full((1, R * tq), NEG_INF, F32), jnp.zeros((1, R * tq), F32), jnp.zeros((HD, R * tq), F32))
    first_win = jnp.maximum(qi - n_win, 0) // 2
    sel_c = lax.fori_loop(0, first_win, sel_trip, init)
    sel_c, win_c = lax.fori_loop(first_win, qi // 2 + 1,
                                 lambda j, c: (sel_trip(j, c[0]), win_trip(j, c[1])), (sel_c, init))
    o_s = sel_c[2] * (1.0 / sel_c[1])
    o_w = win_c[2] * (1.0 / win_c[1])
    brt_ref[...] = br_ref[0].T
    gate = lambda i: jnp.concatenate([brt_ref[pl.ds((g * R + r) * 3 + i, 1), :] for r in range(R)], axis=1)
    o = gate(0) * oc_ref[...] + gate(1) * o_s + gate(2) * o_w
    for r in range(R):
        o_ref[0, :, r * HD:(r + 1) * HD] = o[:, r * tq:(r + 1) * tq].T.astype(o_ref.dtype)


def nsa_prompt(qa, kc, vc_t, ks_bf, ks_blk0, vs_t, kw_bf, kw_blk0, vw_t, small, b_cmp, b_sel, b_win):
    B, T, _ = qa.shape
    assert T % LANES == 0 and T // CMP_STRIDE == LANES and kc.shape[2] == LANES
    n_cb = T // CMP_STRIDE - CMP_LEN // CMP_STRIDE + 1
    n_sb = -(-T // SEL_BLOCK)
    n_sel = min(SEL_TOPN, n_sb)
    tq = LANES
    nkb = T // LANES
    assert nkb % 2 == 0 and n_sb <= LANES
    grp = lambda b, g, i: (b, g, 0, 0)
    expand = (jnp.arange(T, dtype=I32)[:, None] // SEL_BLOCK == jnp.arange(LANES, dtype=I32)[None, :]).astype(BF16)
    return pl.pallas_call(
        functools.partial(_nsa_prompt_kernel, n_cb=n_cb, n_sb=n_sb, n_sel=n_sel),
        grid=(B, G, T // tq),
        in_specs=[pl.BlockSpec((1, tq, R * HD), lambda b, g, i: (b, i, g)),
                  pl.BlockSpec((1, 1, LANES, HD), grp),
                  pl.BlockSpec((1, 1, HD, LANES), grp),
                  pl.BlockSpec((1, T, HD), lambda b, g, i: (b, 0, ks_blk0 + g)),
                  pl.BlockSpec((1, 1, nkb, HD, LANES), lambda b, g, i: (b, g, 0, 0, 0)),
                  pl.BlockSpec((1, T, HD), lambda b, g, i: (b, 0, kw_blk0 + g)),
                  pl.BlockSpec((1, 1, nkb, HD, LANES), lambda b, g, i: (b, g, 0, 0, 0)),
                  pl.BlockSpec((1, tq, LANES), lambda b, g, i: (b, i, 0)),
                  pl.BlockSpec((1, 1, LANES, R * LANES), lambda b, g, i: (g, i, 0, 0)),
                  pl.BlockSpec((1, SEL_BIAS_KINDS, LANES, R * LANES), lambda b, g, i: (g, 0, 0, 0)),
                  pl.BlockSpec((1, WIN_BIAS_KINDS, LANES, R * LANES), lambda b, g, i: (g, 0, 0, 0)),
                  pl.BlockSpec((T, LANES), lambda b, g, i: (0, 0))],
        out_specs=pl.BlockSpec((1, tq, R * HD), lambda b, g, i: (b, i, g)),
        out_shape=jax.ShapeDtypeStruct((B, T, NSA_HEADS * HD), BF16),
        scratch_shapes=[pltpu.VMEM((T, tq), F32), pltpu.VMEM((HD, R * tq), F32),
                        pltpu.VMEM((LANES, tq), F32)],
        compiler_params=_cparams(("parallel", "parallel", "arbitrary")),
        name="nsa_prompt",
    )(qa, kc, vc_t, ks_bf, vs_t, kw_bf, vw_t, small, b_cmp, b_sel, b_win, expand)


def _dec_cmp_kernel(q_ref, kc_ref, vc_ref, tblt_ref, oc_ref, idx_ref, *, qpos, n_cb, n_sb, n_sel, nsbp):
    NC = kc_ref.shape[2]
    q8 = q_ref[0].astype(BF16)
    row = _iota((NSA_HEADS, NC), 0)
    j = _iota((NSA_HEADS, NC), 1)
    s = jnp.where(row < R, _dot_nt(q8, kc_ref[0, 0].astype(BF16)), _dot_nt(q8, kc_ref[0, 1].astype(BF16)))
    dist = qpos - (j * CMP_STRIDE + (CMP_LEN - 1))
    valid = dist >= 0
    tblt = tblt_ref[...]
    bias = _lookup(_bucket(dist), [tblt[:, k:k + 1] for k in range(N_BUCKETS)])
    s = jnp.where(valid, s * SCALE + bias, NEG_INF)
    m = jnp.max(s, axis=1, keepdims=True)
    e = jnp.where(valid, jnp.exp(s - m), 0.0)
    l = jnp.sum(e, axis=1, keepdims=True)
    p = e / jnp.where(l > 0.0, l, 1.0)
    pb = p.astype(BF16)
    row_o = _iota((NSA_HEADS, HD), 0)
    oc_ref[0] = jnp.where(row_o < R, _dot(pb, vc_ref[0, 0].astype(BF16)), _dot(pb, vc_ref[0, 1].astype(BF16)))

    imp0 = jnp.sum(jnp.where(row < R, p, 0.0), axis=0, keepdims=True)
    imp1 = jnp.sum(jnp.where(row >= R, p, 0.0), axis=0, keepdims=True)
    imp = jnp.where(row == 0, imp0, jnp.where(row == 1, imp1, 0.0))
    imp_s = _dot_split_lhs(imp, _sel_overlap_matrix(NC, nsbp, n_cb))
    jb = _iota((NSA_HEADS, nsbp), 1)
    score = _block_scores(imp_s, qpos, jb, n_sb)
    jbf = jb.astype(F32)
    lane_o = _iota((NSA_HEADS, LANES), 1)
    out = jnp.zeros((NSA_HEADS, LANES), I32)
    for n in range(n_sel):
        mx = jnp.max(score, axis=1, keepdims=True)
        am = jnp.min(jnp.where(score == mx, jbf, float(nsbp)), axis=1, keepdims=True)
        out = jnp.where(lane_o == n, am.astype(I32), out)
        score = jnp.where(jbf == am, -3.0, score)
    idx_ref[0] = out


def dec_cmp(q8, kc, vc, tblt, qpos):
    DB = q8.shape[0]
    NC = kc.shape[2]
    L = qpos + 1
    n_cb = L // CMP_STRIDE - CMP_LEN // CMP_STRIDE + 1
    assert n_cb == NC - 1
    n_sb = -(-L // SEL_BLOCK)
    n_sel = min(SEL_TOPN, n_sb)
    nsbp = -(-n_sb // LANES) * LANES
    oc, idx = pl.pallas_call(
        functools.partial(_dec_cmp_kernel, qpos=qpos, n_cb=n_cb, n_sb=n_sb, n_sel=n_sel, nsbp=nsbp),
        grid=(DB,),
        in_specs=[pl.BlockSpec((1, NSA_HEADS, HD), lambda b: (b, 0, 0)),
                  pl.BlockSpec((1, G, NC, HD), lambda b: (b, 0, 0, 0)),
                  pl.BlockSpec((1, G, NC, HD), lambda b: (b, 0, 0, 0)),
                  pl.BlockSpec((NSA_HEADS, N_BUCKETS), lambda b: (0, 0))],
        out_specs=[pl.BlockSpec((1, NSA_HEADS, HD), lambda b: (b, 0, 0)),
                   pl.BlockSpec((1, NSA_HEADS, LANES), lambda b: (b, 0, 0))],
        out_shape=[jax.ShapeDtypeStruct((DB, NSA_HEADS, HD), F32), jax.ShapeDtypeStruct((DB, NSA_HEADS, LANES), I32)],
        compiler_params=_cparams(("parallel",)),
        name="dec_cmp",
    )(q8, kc, vc, tblt)
    return oc, idx[:, :G, :n_sel].reshape(DB, G * n_sel), n_sel


def _softmax_with_new(s, valid, s_new, new_ok):
    s = jnp.where(valid, s, NEG_INF)
    s_new = jnp.where(new_ok, s_new, NEG_INF)
    m = jnp.maximum(jnp.max(s, axis=1, keepdims=True), s_new)
    e = jnp.where(valid, jnp.exp(s - m), 0.0)
    en = jnp.where(new_ok, jnp.exp(s_new - m), 0.0)
    l = jnp.sum(e, axis=1, keepdims=True) + en
    inv = 1.0 / jnp.where(l > 0.0, l, 1.0)
    return e, en, inv


def _dec_selwin_kernel(pt_ref, idx_ref, *refs, n_sel, qpos, jb_new, w_buf):
    blk_refs = refs[:G * n_sel]
    q_ref, new_ref, win_ref, wnew_ref, br_ref, oc_ref, tblt_ref, o_ref = refs[G * n_sel:]
    rph = 4 * G
    b = pl.program_id(0)
    q8 = q_ref[0]
    row8 = _iota((NSA_HEADS, HD), 0)
    tblt = tblt_ref[...]
    tcols = [tblt[:, k:k + 1] for k in range(N_BUCKETS)]
    new8 = new_ref[0]
    wnew = wnew_ref[0]
    brrow = br_ref[0]
    oc = oc_ref[0]
    nk = n_sel * SEL_BLOCK
    lane = _iota((NSA_HEADS, nk), 1)
    outs = []
    for g in range(G):
        qg = jnp.where(_div_pow2(row8, R) == g, q8, 0.0)
        qb = qg.astype(BF16)
        qr = qb.astype(F32)
        kmat = jnp.concatenate([blk_refs[g * n_sel + n][pl.ds(2 * G + g, SEL_BLOCK, stride=rph), :]
                                for n in range(n_sel)], axis=0).astype(BF16)
        vmat = jnp.concatenate([blk_refs[g * n_sel + n][pl.ds(3 * G + g, SEL_BLOCK, stride=rph), :]
                                for n in range(n_sel)], axis=0).astype(BF16)
        idxv = jnp.zeros((NSA_HEADS, nk), I32)
        has_new = jnp.zeros((), jnp.bool_)
        for n in range(n_sel):
            sidx = idx_ref[b, g * n_sel + n]
            idxv = jnp.where(_div_pow2(lane, SEL_BLOCK) == n, sidx, idxv)
            has_new = has_new | (sidx == jb_new)
        dist = qpos - (idxv * SEL_BLOCK + (lane & (SEL_BLOCK - 1)))
        valid = (idxv < jb_new) & (dist >= 0)
        s = _dot_nt(qb, kmat) * SCALE + _lookup(_bucket(dist), tcols)
        kn = new8[2 * G + g:2 * G + g + 1, :].astype(BF16).astype(F32)
        vn = new8[3 * G + g:3 * G + g + 1, :].astype(BF16).astype(F32)
        s_new = jnp.sum(qr * kn, axis=1, keepdims=True) * SCALE + tcols[0]
        e, en, inv = _softmax_with_new(s, valid, s_new, has_new)
        o_s = (_dot(e.astype(BF16), vmat) + en * vn) * inv

        kwin = win_ref[pl.ds(g, w_buf, stride=2 * G), :].astype(BF16)
        vwin = win_ref[pl.ds(G + g, w_buf, stride=2 * G), :].astype(BF16)
        lw = _iota((NSA_HEADS, w_buf), 1)
        dist_w = w_buf - lw
        valid_w = (dist_w < WINDOW) & (qpos - dist_w >= 0)
        sw = _dot_nt(qb, kwin) * SCALE + _lookup(_bucket(dist_w), tcols)
        kwn = wnew[g:g + 1, :].astype(BF16).astype(F32)
        vwn = wnew[G + g:G + g + 1, :].astype(BF16).astype(F32)
        sw_new = jnp.sum(qr * kwn, axis=1, keepdims=True) * SCALE + tcols[0]
        ew, ewn, invw = _softmax_with_new(sw, valid_w, sw_new, jnp.ones((), jnp.bool_))
        o_w = (_dot(ew.astype(BF16), vwin) + ewn * vwn) * invw

        def bcol(i):
            return jnp.sum(jnp.where(_iota((NSA_HEADS, LANES), 1) == row8 * 3 + i, brrow, 0.0), axis=1, keepdims=True)

        outs.append(bcol(0) * oc + bcol(1) * o_s + bcol(2) * o_w)
    o_ref[0] = jnp.where(row8 < R, outs[0], outs[1])


def dec_selwin(q8, new8, wnew4, small_s, oc, tblt, cache_nsa_kv, state_win, page_table, idx, n_sel, layer, qpos):
    DB = q8.shape[0]
    n_pool, depth = cache_nsa_kv.shape[:2]
    page = cache_nsa_kv.shape[2]
    halves = page // SEL_BLOCK
    w_buf = state_win.shape[2]
    n_pages = page_table.shape[1]
    jb_new = qpos // SEL_BLOCK
    rph = 4 * G
    cache_rows = cache_nsa_kv.reshape(n_pool, depth, page * rph, HD)
    state_rows = state_win.reshape(depth, DB, w_buf * 2 * G, HD)
    in_specs = []
    args = []
    for g in range(G):
        for n in range(n_sel):
            def imap(b, pt, ix, g=g, n=n):
                jb = jnp.clip(ix[b, g * n_sel + n], 0, jb_new - 1)
                return (pt[b, jb // halves], layer, jb % halves, 0)
            in_specs.append(pl.BlockSpec((None, None, SEL_BLOCK * rph, HD), imap))
            args.append(cache_rows)
    row3 = lambda b, pt, ix: (b, 0, 0)
    in_specs += [pl.BlockSpec((1, NSA_HEADS, HD), row3), pl.BlockSpec((1, 4 * G, HD), row3),
                 pl.BlockSpec((None, None, w_buf * 2 * G, HD), lambda b, pt, ix: (layer, b, 0, 0)),
                 pl.BlockSpec((1, 2 * G, HD), row3), pl.BlockSpec((1, 1, LANES), row3),
                 pl.BlockSpec((1, NSA_HEADS, HD), row3),
                 pl.BlockSpec((NSA_HEADS, N_BUCKETS), lambda b, pt, ix: (0, 0))]
    args += [q8, new8, state_rows, wnew4, small_s.reshape(DB, 1, LANES), oc, tblt]
    return pl.pallas_call(
        functools.partial(_dec_selwin_kernel, n_sel=n_sel, qpos=qpos, jb_new=jb_new, w_buf=w_buf),
        grid_spec=pltpu.PrefetchScalarGridSpec(
            num_scalar_prefetch=2,
            grid=(DB,),
            in_specs=in_specs,
            out_specs=pl.BlockSpec((1, NSA_HEADS, HD), row3)),
        out_shape=jax.ShapeDtypeStruct((DB, NSA_HEADS, HD), F32),
        compiler_params=_cparams(("arbitrary",)),
        name="dec_selwin",
    )(page_table, idx, *args)


def _dec_fox_kernel(pt_ref, *refs, npg):
    kv_refs = refs[:npg]
    lf_refs = refs[npg:2 * npg]
    q_ref, new_ref, lfn_ref, o_ref, m_sc, l_sc, acc_sc, car_sc = refs[2 * npg:]
    p = pl.program_id(1)
    nh = FOX_HEADS
    page = kv_refs[0].shape[0]
    q8 = q_ref[0]
    qs = q8 * SCALE

    @pl.when(p == 0)
    def _():
        new = new_ref[0]
        s_new = jnp.sum(qs * new[0:nh], axis=1, keepdims=True)
        m_sc[...] = jnp.broadcast_to(s_new, (nh, LANES))
        l_sc[...] = jnp.ones((nh, LANES), F32)
        acc_sc[...] = new[nh:2 * nh]
        lane1 = _iota((nh, LANES), 1)
        row1 = _iota((nh, LANES), 0)
        car_sc[...] = jnp.sum(jnp.where(lane1 == LOGF_LANE0 + row1, lfn_ref[0], 0.0), axis=1, keepdims=True)

    def tree(x, op):
        while x.shape[0] > 1:
            half = x.shape[0] // 2
            x = op(x[:half], x[half:])
        return x[0]

    uu = _iota((LANES, LANES), 0)
    ss = _iota((LANES, LANES), 1)
    later = jnp.where(uu > ss, 1.0, 0.0).astype(BF16)
    ones = jnp.ones((HD, LANES), BF16)
    tok3 = _iota((page, nh, LANES), 0)
    lane3 = _iota((page, nh, LANES), 2)
    m = m_sc[...]
    l = l_sc[...]
    acc = acc_sc[...]
    car = car_sc[...]
    for i in range(npg):
        lft = lf_refs[i][...]
        decay = _dot_split_lhs(lft, later) + car
        lhs = jnp.where(lane3 == tok3, decay[None], 0.0) + kv_refs[i][:, 0] * qs[None]
        s = _dot(lhs.reshape(page * nh, HD).astype(BF16), ones).reshape(page, nh, LANES)
        m_new = jnp.maximum(m, tree(s, jnp.maximum))
        alpha = jnp.exp(m - m_new)
        pp = jnp.exp(s - m_new[None])
        l = alpha * l + tree(pp, jnp.add)
        acc = alpha * acc + tree(pp * kv_refs[i][:, 1], jnp.add)
        m = m_new
        car = car + jnp.sum(lft, axis=1, keepdims=True)
    m_sc[...] = m
    l_sc[...] = l
    acc_sc[...] = acc
    car_sc[...] = car

    @pl.when(p == pl.num_programs(1) - 1)
    def _():
        o_ref[0] = acc / l


def dec_fox(q8, new16, small_s, cache_fox_kv, logf_t, page_table, layer):
    DB = q8.shape[0]
    n_pool, depth, page = cache_fox_kv.shape[:3]
    assert page == LANES
    n_pages = page_table.shape[1]
    npg = _pick(n_pages, (8, 4, 2, 1))
    in_specs = []
    for i in range(npg):
        in_specs.append(pl.BlockSpec((None, None, page, 2, FOX_HEADS, HD),
                                     lambda b, p, pt, i=i: (pt[b, n_pages - 1 - (p * npg + i)], layer, 0, 0, 0, 0)))
    for i in range(npg):
        in_specs.append(pl.BlockSpec((None, None, FOX_HEADS, page),
                                     lambda b, p, pt, i=i: (pt[b, n_pages - 1 - (p * npg + i)], layer, 0, 0)))
    row3 = lambda b, p, pt: (b, 0, 0)
    in_specs += [pl.BlockSpec((1, FOX_HEADS, HD), row3), pl.BlockSpec((1, 2 * FOX_HEADS, HD), row3),
                 pl.BlockSpec((1, 1, LANES), row3)]
    return pl.pallas_call(
        functools.partial(_dec_fox_kernel, npg=npg),
        grid_spec=pltpu.PrefetchScalarGridSpec(
            num_scalar_prefetch=1,
            grid=(DB, n_pages // npg),
            in_specs=in_specs,
            out_specs=pl.BlockSpec((1, FOX_HEADS, HD), row3),
            scratch_shapes=[pltpu.VMEM((FOX_HEADS, LANES), F32), pltpu.VMEM((FOX_HEADS, LANES), F32),
                            pltpu.VMEM((FOX_HEADS, HD), F32), pltpu.VMEM((FOX_HEADS, 1), F32)]),
        out_shape=jax.ShapeDtypeStruct((DB, FOX_HEADS, HD), F32),
        compiler_params=_cparams(("parallel", "arbitrary")),
        name="dec_fox",
    )(page_table, *([cache_fox_kv] * npg), *([logf_t] * npg), q8, new16, small_s.reshape(DB, 1, LANES))


def _layer_weights(l, D, w_in, b_forget, qk_gain_nsa, qk_gain_fox):
    nq = NSA_HEADS * HD
    nkv = 6 * G * HD
    nfox = 3 * FOX_HEADS * HD
    o_br = nq + nkv
    o_fox = o_br + BR_LANES
    o_f = o_fox + nfox
    o_mg = o_f + FOX_HEADS
    w = w_in[l].astype(BF16)
    ones = jnp.ones((HD,), F32)
    gn = qk_gain_nsa[l]
    gf = qk_gain_fox[l]

    def gains(rows):
        return jnp.concatenate(rows).reshape(1, -1)

    zeros = lambda n: jnp.zeros((1, n), F32)
    small_b = jnp.concatenate([jnp.zeros((BR_LANES,), F32), b_forget[l],
                               jnp.zeros((LANES - BR_LANES - FOX_HEADS,), F32)]).reshape(1, LANES)
    kvh = G
    tg = _pick(2 * D, (1024, 512, 256, 128))
    table = [
        ("foxkv", [(o_fox + FOX_HEADS * HD, o_f)], gains([gf[1]] * FOX_HEADS + [ones] * FOX_HEADS),
         zeros(2 * FOX_HEADS * HD), ("norm",) * FOX_HEADS + ("raw",) * FOX_HEADS, 2 * FOX_HEADS * HD),
        ("qa", [(0, nq)], gains([gn[0]] * NSA_HEADS), zeros(nq), ("norm",) * NSA_HEADS, nq),
        ("nsakv", [(nq, nq + 4 * kvh * HD)], gains([ones] * (2 * kvh) + [gn[2]] * kvh + [ones] * kvh),
         zeros(4 * kvh * HD), ("raw",) * (2 * kvh) + ("norm",) * kvh + ("raw",) * kvh, 4 * kvh * HD),
        ("qf", [(o_fox, o_fox + FOX_HEADS * HD)], gains([gf[0]] * FOX_HEADS), zeros(FOX_HEADS * HD),
         ("norm",) * FOX_HEADS, FOX_HEADS * HD),
        ("gates", [(o_mg, o_mg + 2 * D)], zeros(2 * D), zeros(2 * D), ("sigmoid",) * (tg // LANES), tg),
        ("win", [(nq + 4 * kvh * HD, o_br)], gains([gn[3]] * kvh + [ones] * kvh), zeros(2 * kvh * HD),
         ("norm",) * kvh + ("raw",) * kvh, 2 * kvh * HD),
        ("small", [(o_br, o_br + BR_LANES), (o_f, o_f + FOX_HEADS)], zeros(LANES), small_b, ("small",), LANES),
    ]
    pieces, segs, col = [], {}, 0
    for name, spans, gn_row, bias_row, modes, tn in table:
        width = gn_row.shape[1]
        got = 0
        for lo, hi in spans:
            pieces.append(w[:, lo:hi])
            got += hi - lo
        if got < width:
            pieces.append(jnp.zeros((D, width - got), BF16))
        assert col % tn == 0
        segs[name] = (col, width, gn_row, bias_row, modes, tn)
        col += width
    return jnp.concatenate(pieces, axis=1), segs


def _project_all(h, w_perm, segs, bf16_copies=()):
    out = {}
    for name, (col0, n, gn, bs, modes, tn) in segs.items():
        if name in bf16_copies:
            out[name], out[name + "_bf16"] = proj(h, w_perm, col0, n, gn, bs, modes, tn, with_bf16_copy=True)
        else:
            out[name] = proj(h, w_perm, col0, n, gn, bs, modes, tn)
    return out


def _tail(x, o_nsa, o_fox, gates, pe, gains_l, gain_next, wbn, wbf, wo, wfi, wfo, wple, wpg):
    merged = merge(o_nsa, o_fox, gates, wbn, wbf)
    x1, h1 = out_proj(merged, wo, x, gains_l[1])
    act = ffn_in(h1, wfi)
    x2 = ffn_out(act, wfo, x1)
    return ple_gate(x2, gains_l[2], wpg, pe, wple, gain_next)


def kernel(x_prompt, x_sample, cache_nsa_kv, cache_fox_kv, cache_fox_logf, state_nsa_window, page_table, p_prompt, p_sample, rel_bias_table, norm_gains, w_in, b_forget, qk_gain_nsa, qk_gain_fox, cmp_pos, w_cmp1, b_cmp1, w_cmp2, w_branch_nsa, w_branch_fox, w_out, w_ffn_in, w_ffn_out, w_ple, w_ple_gate):
    B, T, D = x_prompt.shape
    DB, n_qs, _ = x_sample.shape
    assert n_qs == 1
    depth = w_in.shape[0]
    page = cache_nsa_kv.shape[2]
    n_pages = page_table.shape[1]
    past = n_pages * page
    w_buf = state_nsa_window.shape[2]
    kdim = CMP_STRIDE * HD

    xp = x_prompt.reshape(B * T, D)
    xs = x_sample.reshape(DB, D)
    b_sel, b_win = bias_sel_win(rel_bias_table)
    b_cmp = bias_cmp(rel_bias_table, T // LANES)
    fox_tk = _pick(T, (256, 128))
    tblt = rel_bias_table.T
    logf_t = jnp.swapaxes(cache_fox_logf, 2, 3)
    pt_prompt = jnp.arange(B * (T // page), dtype=I32).reshape(B, T // page)
    hp = rms_bf16(xp, norm_gains[0, 0])
    hs = rms_bf16(xs, norm_gains[0, 0])

    outs = [[] for _ in range(8)]
    for l in range(depth):
        w_perm, segs = _layer_weights(l, D, w_in, b_forget, qk_gain_nsa, qk_gain_fox)
        w1 = w_cmp1[l]
        w1cat = jnp.concatenate([w1[:, :kdim], w1[:, kdim:]], axis=2).astype(BF16)
        pos = cmp_pos[l].reshape(2, 2, 1, kdim)
        pos_ab = jnp.broadcast_to(pos, (2, 2, SUBLANES, kdim)).astype(BF16)
        cmp_args = (pos_ab, w1cat, b_cmp1[l].reshape(2, 1, HD), w_cmp2[l].astype(BF16),
                    qk_gain_nsa[l, 1].reshape(1, HD))
        gain_next = norm_gains[l + 1, 0] if l + 1 < depth else norm_gains[l, 0]
        tail_w = (norm_gains[l], gain_next, w_branch_nsa[l].astype(BF16), w_branch_fox[l].astype(BF16),
                  w_out[l].astype(BF16), w_ffn_in[l].astype(BF16), w_ffn_out[l].astype(BF16),
                  w_ple[l].astype(BF16), w_ple_gate[l].astype(BF16))

        P = _project_all(hp, w_perm, segs, bf16_copies=("foxkv", "nsakv", "win"))
        small3 = P["small"].reshape(B, T, LANES)
        c_row, c_rep = cumsum_logf(small3)
        foxkv_bf = P["foxkv_bf16"].reshape(B, T, -1)
        nfk = FOX_HEADS * HD
        o_fox_p = fox_prompt(P["qf"].reshape(B, T, -1), foxkv_bf,
                             key_blocks_transposed(foxkv_bf[:, :, nfk:], FOX_HEADS, fox_tk), c_row, c_rep)
        part = compress_partial(P["nsakv"].reshape(B * (T // page), 1, page, 4 * G * HD), pt_prompt, 0, w1cat, False)
        kc_p, vc_p = compress_finish(part, *cmp_args)
        nsakv_bf = P["nsakv_bf16"].reshape(B, T, -1)
        win_bf = P["win_bf16"].reshape(B, T, -1)
        ngk = G * HD
        o_nsa_p = nsa_prompt(P["qa"].reshape(B, T, -1), kc_p, jnp.swapaxes(vc_p, 2, 3),
                             nsakv_bf, 2 * G, key_blocks_transposed(nsakv_bf[:, :, 3 * ngk:], G, LANES),
                             win_bf, 0, key_blocks_transposed(win_bf[:, :, ngk:], G, LANES),
                             small3, b_cmp, b_sel, b_win)
        xp, hp = _tail(xp, o_nsa_p.reshape(B * T, -1), o_fox_p.reshape(B * T, -1), P["gates"],
                       p_prompt[l].reshape(B * T, -1), *tail_w)

        S = _project_all(hs, w_perm, segs)
        part_s = compress_partial(cache_nsa_kv.reshape(cache_nsa_kv.shape[0], depth, page, 2, 2 * G, HD),
                                  page_table, l, w1cat, True)
        kc_s, vc_s = compress_finish(part_s, *cmp_args)
        q8 = S["qa"].reshape(DB, NSA_HEADS, HD)
        oc_s, idx, n_sel = dec_cmp(q8, kc_s, vc_s, tblt, past)
        o_nsa_s = dec_selwin(q8, S["nsakv"].reshape(DB, 4 * G, HD), S["win"].reshape(DB, 2 * G, HD), S["small"],
                             oc_s, tblt, cache_nsa_kv, state_nsa_window, page_table, idx, n_sel, l, past)
        o_fox_s = dec_fox(S["qf"].reshape(DB, FOX_HEADS, HD), S["foxkv"].reshape(DB, 2 * FOX_HEADS, HD),
                          S["small"], cache_fox_kv, logf_t, page_table, l)
        xs, hs = _tail(xs, o_nsa_s.reshape(DB, -1).astype(BF16), o_fox_s.reshape(DB, -1).astype(BF16), S["gates"],
                       p_sample[l].reshape(DB, -1), *tail_w)

        lf0, lf1 = LOGF_LANE0, LOGF_LANE0 + FOX_HEADS
        outs[0].append(P["nsakv"].reshape(B, T, 4, G, HD))
        outs[1].append(P["win"].reshape(B, T, 2, G, HD)[:, T - min(WINDOW, T):])
        outs[2].append(P["foxkv"].reshape(B, T, 2, FOX_HEADS, HD))
        outs[3].append(P["small"][:, lf0:lf1].reshape(B, T, FOX_HEADS))
        outs[4].append(S["nsakv"].reshape(DB, 1, 4, G, HD))
        win_all = jnp.concatenate([state_nsa_window[l], S["win"].reshape(DB, 1, 2, G, HD)], axis=1)
        outs[5].append(win_all[:, w_buf + 1 - min(WINDOW, w_buf + 1):])
        outs[6].append(S["foxkv"].reshape(DB, 1, 2, FOX_HEADS, HD))
        outs[7].append(S["small"][:, lf0:lf1].reshape(DB, 1, FOX_HEADS))

    return (xp.reshape(B, T, D), xs.reshape(DB, 1, D)) + tuple(jnp.stack(o) for o in outs)
```

```python
import functools
import math

import jax
import jax.numpy as jnp
from jax import lax
from jax.experimental import pallas as pl
from jax.experimental.pallas import tpu as pltpu

F32 = jnp.float32
BF16 = jnp.bfloat16
I32 = jnp.int32

HEAD_DIM = 128
NSA_HEADS = 8
NSA_KV_HEADS = 2
NSA_GROUP = NSA_HEADS // NSA_KV_HEADS
FOX_HEADS = 8
CMP_LEN = 32
CMP_STRIDE = 16
SEL_BLOCK = 64
SEL_TOPN = 16
N_LOCAL_BLOCKS = 2
WINDOW = 512
N_BUCKETS = 32
BUCKET_EXACT = N_BUCKETS // 2
MAX_DISTANCE = 128
EPS = 1e-6
SCALE = HEAD_DIM ** -0.5
NEG_INF = -1e30
FORCE_BONUS = 1e4

LANES = 128
SUBLANES = 8
VMEM_LIMIT_MB = 52

G = NSA_KV_HEADS
R = NSA_GROUP
HD = HEAD_DIM
BR_LANES = 3 * NSA_HEADS
LOGF_LANE0 = BR_LANES
CHUNKS_PER_PAGE = 128 // CMP_STRIDE


def _cparams(sem, vmem_mb=VMEM_LIMIT_MB):
    return pltpu.CompilerParams(dimension_semantics=sem, vmem_limit_bytes=vmem_mb * 1024 * 1024)


def _pick(n, prefs):
    for p in prefs:
        if n % p == 0:
            return p
    return n


def _iota(shape, dim):
    return lax.broadcasted_iota(I32, shape, dim)


def _dot(a, b):
    return jnp.dot(a, b, preferred_element_type=F32)


def _dot_nt(a, b):
    return lax.dot_general(a, b, (((1,), (1,)), ((), ())), preferred_element_type=F32)


def _split3(x):
    hi = x.astype(BF16)
    r1 = x - hi.astype(F32)
    mid = r1.astype(BF16)
    lo = (r1 - mid.astype(F32)).astype(BF16)
    return hi, mid, lo


def _dot_split_lhs(x, m01):
    hi, mid, lo = _split3(x)
    return _dot(hi, m01) + _dot(mid, m01) + _dot(lo, m01)


def _dot_split_rhs(m01, x):
    hi, mid, lo = _split3(x)
    return _dot(m01, hi) + _dot(m01, mid) + _dot(m01, lo)


def _rms(x, gain):
    return x * lax.rsqrt(jnp.mean(x * x, axis=-1, keepdims=True) + EPS) * gain


def _bucket(dist):
    n = jnp.maximum(dist, 0)
    rel = jnp.log(jnp.maximum(n, 1).astype(F32) / float(BUCKET_EXACT)) / math.log(MAX_DISTANCE / BUCKET_EXACT)
    large = jnp.minimum(BUCKET_EXACT + (rel * float(N_BUCKETS - BUCKET_EXACT)).astype(I32), N_BUCKETS - 1)
    return jnp.where(n < BUCKET_EXACT, n, large)


def _lookup(bucket, vals):
    out = vals[N_BUCKETS - 1]
    for k in range(N_BUCKETS - 1):
        out = jnp.where(bucket == k, vals[k], out)
    return out


def _rms_kernel(x_ref, g_ref, o_ref):
    o_ref[...] = _rms(x_ref[...], g_ref[...]).astype(o_ref.dtype)


def rms_bf16(x, gain):
    M, D = x.shape
    tm = _pick(M, (512, 256, 128, 8))
    return pl.pallas_call(
        _rms_kernel,
        grid=(M // tm,),
        in_specs=[pl.BlockSpec((tm, D), lambda i: (i, 0)), pl.BlockSpec((1, D), lambda i: (0, 0))],
        out_specs=pl.BlockSpec((tm, D), lambda i: (i, 0)),
        out_shape=jax.ShapeDtypeStruct((M, D), BF16),
        compiler_params=_cparams(("parallel",)),
        name="rms",
    )(x, gain.reshape(1, D))


def _proj_kernel(h_ref, w_ref, g_ref, b_ref, o_ref, *maybe_bf16_ref, modes):
    h = h_ref[...]
    nt = len(modes)
    t = 0
    while t < nt:
        wd = 2 if t + 1 < nt else 1
        acc = _dot(h, w_ref[:, t * LANES:(t + wd) * LANES])
        for u in range(wd):
            c0 = (t + u) * LANES
            v = acc[:, u * LANES:(u + 1) * LANES]
            mode = modes[t + u]
            if mode == "norm":
                v = _rms(v, g_ref[:, c0:c0 + LANES])
            elif mode == "sigmoid":
                v = jax.nn.sigmoid(v)
            elif mode == "small":
                lane = _iota(v.shape, 1)
                z = v + b_ref[:, c0:c0 + LANES]
                logsig = jnp.minimum(z, 0.0) - jnp.log(1.0 + jnp.exp(-jnp.abs(z)))
                v = jnp.where(lane < BR_LANES, jax.nn.sigmoid(v),
                              jnp.where(lane < LOGF_LANE0 + FOX_HEADS, logsig, 0.0))
            o_ref[:, c0:c0 + LANES] = v
            for ob_ref in maybe_bf16_ref:
                ob_ref[:, c0:c0 + LANES] = v.astype(ob_ref.dtype)
        t += wd


def proj(h, w, col0, N, gains, bias, modes, tn, with_bf16_copy=False):
    M, K = h.shape
    out_bytes = 4 + (2 if with_bf16_copy else 0)
    budget = VMEM_LIMIT_MB * 1024 * 1024 * 4 // 5
    fits = [t for t in (1024, 512, 256, 128, 8)
            if M % t == 0 and 2 * (t * K * 2 + K * tn * 2 + t * tn * out_bytes) <= budget]
    tm = fits[0] if fits else M
    assert len(modes) * LANES == tn and N % tn == 0 and col0 % tn == 0
    jb0 = col0 // tn
    out_spec = pl.BlockSpec((tm, tn), lambda i, j: (i, j))
    f32_out = jax.ShapeDtypeStruct((M, N), F32)
    return pl.pallas_call(
        functools.partial(_proj_kernel, modes=modes),
        grid=(M // tm, N // tn),
        in_specs=[pl.BlockSpec((tm, K), lambda i, j: (i, 0)),
                  pl.BlockSpec((K, tn), lambda i, j: (0, jb0 + j)),
                  pl.BlockSpec((1, tn), lambda i, j: (0, j)),
                  pl.BlockSpec((1, tn), lambda i, j: (0, j))],
        out_specs=[out_spec, out_spec] if with_bf16_copy else out_spec,
        out_shape=[f32_out, jax.ShapeDtypeStruct((M, N), BF16)] if with_bf16_copy else f32_out,
        compiler_params=_cparams(("parallel", "arbitrary")),
        name="proj",
    )(h, w, gains, bias)


def _merge_kernel(on_ref, of_ref, ga_ref, gb_ref, wn_ref, wf_ref, o_ref):
    a = _dot(on_ref[...], wn_ref[...])
    b = _dot(of_ref[...], wf_ref[...])
    o_ref[...] = (ga_ref[...] * a + gb_ref[...] * b).astype(o_ref.dtype)


def merge(o_nsa, o_fox, gates, wbn, wbf):
    M, K = o_nsa.shape
    D = wbn.shape[1]
    tm = _pick(M, (1024, 512, 256, 128, 8))
    tn = _pick(D, (1024, 512, 256, 128))
    nj = D // tn
    return pl.pallas_call(
        _merge_kernel,
        grid=(M // tm, nj),
        in_specs=[pl.BlockSpec((tm, K), lambda i, j: (i, 0)),
                  pl.BlockSpec((tm, K), lambda i, j: (i, 0)),
                  pl.BlockSpec((tm, tn), lambda i, j: (i, j)),
                  pl.BlockSpec((tm, tn), lambda i, j: (i, j + nj)),
                  pl.BlockSpec((K, tn), lambda i, j: (0, j)),
                  pl.BlockSpec((K, tn), lambda i, j: (0, j))],
        out_specs=pl.BlockSpec((tm, tn), lambda i, j: (i, j)),
        out_shape=jax.ShapeDtypeStruct((M, D), BF16),
        compiler_params=_cparams(("parallel", "arbitrary")),
        name="merge",
    )(o_nsa, o_fox, gates, gates, wbn, wbf)


def _wo_kernel(m_ref, w_ref, x_ref, g_ref, x1_ref, h1_ref):
    x1 = x_ref[...] + _dot(m_ref[...], w_ref[...])
    x1_ref[...] = x1
    h1_ref[...] = _rms(x1, g_ref[...]).astype(h1_ref.dtype)


def out_proj(merged, wo, x, gain1):
    M, D = x.shape
    tm = _pick(M, (512, 256, 128, 8))
    return pl.pallas_call(
        _wo_kernel,
        grid=(M // tm,),
        in_specs=[pl.BlockSpec((tm, D), lambda i: (i, 0)),
                  pl.BlockSpec((D, D), lambda i: (0, 0)),
                  pl.BlockSpec((tm, D), lambda i: (i, 0)),
                  pl.BlockSpec((1, D), lambda i: (0, 0))],
        out_specs=[pl.BlockSpec((tm, D), lambda i: (i, 0)), pl.BlockSpec((tm, D), lambda i: (i, 0))],
        out_shape=[jax.ShapeDtypeStruct((M, D), F32), jax.ShapeDtypeStruct((M, D), BF16)],
        compiler_params=_cparams(("parallel",)),
        name="out_proj",
    )(merged, wo, x, gain1.reshape(1, D))


def _ffn_in_kernel(h_ref, wg_ref, wu_ref, o_ref):
    h = h_ref[...]
    gt = _dot(h, wg_ref[...])
    up = _dot(h, wu_ref[...])
    o_ref[...] = (gt * jax.nn.sigmoid(gt) * up).astype(o_ref.dtype)


def ffn_in(h1, wfi):
    M, D = h1.shape
    F = wfi.shape[1] // 2
    tm = _pick(M, (1024, 512, 256, 128, 8))
    tn = _pick(F, (512, 256, 128))
    nj = F // tn
    return pl.pallas_call(
        _ffn_in_kernel,
        grid=(M // tm, nj),
        in_specs=[pl.BlockSpec((tm, D), lambda i, j: (i, 0)),
                  pl.BlockSpec((D, tn), lambda i, j: (0, j)),
                  pl.BlockSpec((D, tn), lambda i, j: (0, j + nj))],
        out_specs=pl.BlockSpec((tm, tn), lambda i, j: (i, j)),
        out_shape=jax.ShapeDtypeStruct((M, F), BF16),
        compiler_params=_cparams(("parallel", "arbitrary")),
        name="ffn_in",
    )(h1, wfi, wfi)


def _ffn_out_kernel(a_ref, w_ref, x_ref, o_ref):
    o_ref[...] = x_ref[...] + _dot(a_ref[...], w_ref[...])


def ffn_out(act, wfo, x1):
    M, F = act.shape
    D = wfo.shape[1]
    tm = _pick(M, (1024, 512, 256, 128, 8))
    tn = _pick(D, (512, 256, 128))
    return pl.pallas_call(
        _ffn_out_kernel,
        grid=(M // tm, D // tn),
        in_specs=[pl.BlockSpec((tm, F), lambda i, j: (i, 0)),
                  pl.BlockSpec((F, tn), lambda i, j: (0, j)),
                  pl.BlockSpec((tm, tn), lambda i, j: (i, j))],
        out_specs=pl.BlockSpec((tm, tn), lambda i, j: (i, j)),
        out_shape=jax.ShapeDtypeStruct((M, D), F32),
        compiler_params=_cparams(("parallel", "arbitrary")),
        name="ffn_out",
    )(act, wfo, x1)


def _ple_kernel(x_ref, g_ref, wpg_ref, pe_ref, wple_ref, gn_ref, o_ref, hn_ref):
    x = x_ref[...]
    h2 = _rms(x, g_ref[...]).astype(BF16)
    gate = jax.nn.sigmoid(_dot(h2, wpg_ref[...]))
    x3 = x + gate * _dot(pe_ref[...].astype(BF16), wple_ref[...])
    o_ref[...] = x3
    hn_ref[...] = _rms(x3, gn_ref[...]).astype(hn_ref.dtype)


def ple_gate(x2, gain2, wpg, pe, wple, gain_next):
    M, D = x2.shape
    Pd = pe.shape[1]
    tm = _pick(M, (512, 256, 128, 8))
    return pl.pallas_call(
        _ple_kernel,
        grid=(M // tm,),
        in_specs=[pl.BlockSpec((tm, D), lambda i: (i, 0)),
                  pl.BlockSpec((1, D), lambda i: (0, 0)),
                  pl.BlockSpec((D, D), lambda i: (0, 0)),
                  pl.BlockSpec((tm, Pd), lambda i: (i, 0)),
                  pl.BlockSpec((Pd, D), lambda i: (0, 0)),
                  pl.BlockSpec((1, D), lambda i: (0, 0))],
        out_specs=[pl.BlockSpec((tm, D), lambda i: (i, 0)), pl.BlockSpec((tm, D), lambda i: (i, 0))],
        out_shape=[jax.ShapeDtypeStruct((M, D), F32), jax.ShapeDtypeStruct((M, D), BF16)],
        compiler_params=_cparams(("parallel",)),
        name="ple_gate",
    )(x2, gain2.reshape(1, D), wpg, pe, wple, gain_next.reshape(1, D))


SEL_BIAS_KINDS = 3
WIN_BIAS_KINDS = WINDOW // LANES + 2


def _bias_sw_kernel(tbl_ref, bs_ref, bw_ref):
    g = pl.program_id(0)
    kk = _iota((LANES, LANES), 0)
    qq = _iota((LANES, LANES), 1)
    bw_ref[0, WIN_BIAS_KINDS - 1] = jnp.full((LANES, R * LANES), NEG_INF, F32)
    for off in range(WIN_BIAS_KINDS - 1):
        d = off * LANES + qq - kk
        bkt = _bucket(d)
        in_window = (d >= 0) & (d < WINDOW)
        for r in range(R):
            cols = slice(r * LANES, (r + 1) * LANES)
            b = _lookup(bkt, [tbl_ref[k, g * R + r] for k in range(N_BUCKETS)])
            bw_ref[0, off, :, cols] = jnp.where(in_window, b, NEG_INF)
            if off < SEL_BIAS_KINDS - 1:
                bs_ref[0, off, :, cols] = b
    for r in range(R):
        bs_ref[0, SEL_BIAS_KINDS - 1, :, r * LANES:(r + 1) * LANES] = jnp.full(
            (LANES, LANES), tbl_ref[N_BUCKETS - 1, g * R + r], F32)


def bias_sel_win(table):
    shapes = [(G, SEL_BIAS_KINDS, LANES, R * LANES), (G, WIN_BIAS_KINDS, LANES, R * LANES)]
    return pl.pallas_call(
        _bias_sw_kernel,
        grid=(G,),
        in_specs=[pl.BlockSpec(memory_space=pltpu.SMEM)],
        out_specs=[pl.BlockSpec((1,) + s[1:], lambda g: (g, 0, 0, 0)) for s in shapes],
        out_shape=[jax.ShapeDtypeStruct(s, F32) for s in shapes],
        compiler_params=_cparams(("arbitrary",)),
        name="bias_sel_win",
    )(table)


def _bias_cmp_kernel(tbl_ref, bc_ref):
    g = pl.program_id(0)
    qi = pl.program_id(1)
    cc = _iota((LANES, LANES), 0)
    qq = _iota((LANES, LANES), 1)
    d = qi * LANES + qq - (cc * CMP_STRIDE + (CMP_LEN - 1))
    bkt = _bucket(d)
    for r in range(R):
        b = _lookup(bkt, [tbl_ref[k, g * R + r] for k in range(N_BUCKETS)])
        bc_ref[0, 0, :, r * LANES:(r + 1) * LANES] = jnp.where(d >= 0, b, NEG_INF)


def bias_cmp(table, n_qblk):
    return pl.pallas_call(
        _bias_cmp_kernel,
        grid=(G, n_qblk),
        in_specs=[pl.BlockSpec(memory_space=pltpu.SMEM)],
        out_specs=pl.BlockSpec((1, 1, LANES, R * LANES), lambda g, i: (g, i, 0, 0)),
        out_shape=jax.ShapeDtypeStruct((G, n_qblk, LANES, R * LANES), F32),
        compiler_params=_cparams(("arbitrary", "arbitrary")),
        name="bias_cmp",
    )(table)


def _cumsum_kernel(x_ref, cr_ref, cp_ref):
    T = x_ref.shape[1]
    ii = _iota((LANES, LANES), 0)
    jj = _iota((LANES, LANES), 1)
    tri = jnp.where(jj <= ii, 1.0, 0.0).astype(BF16)
    carry = jnp.zeros((1, LANES), F32)
    for blk in range(T // LANES):
        sl = slice(blk * LANES, (blk + 1) * LANES)
        c = _dot_split_rhs(tri, x_ref[0, sl, :]) + carry
        cr_ref[0, :, sl] = c.T[LOGF_LANE0:LOGF_LANE0 + FOX_HEADS, :]
        for h in range(FOX_HEADS):
            cp_ref[0, h, sl, :] = jnp.broadcast_to(c[:, LOGF_LANE0 + h:LOGF_LANE0 + h + 1], (LANES, LANES))
        carry = c[LANES - 1:LANES, :]


def cumsum_logf(small):
    B, T, _ = small.shape
    return pl.pallas_call(
        _cumsum_kernel,
        grid=(B,),
        in_specs=[pl.BlockSpec((1, T, LANES), lambda b: (b, 0, 0))],
        out_specs=[pl.BlockSpec((1, FOX_HEADS, T), lambda b: (b, 0, 0)),
                   pl.BlockSpec((1, FOX_HEADS, T, LANES), lambda b: (b, 0, 0, 0))],
        out_shape=[jax.ShapeDtypeStruct((B, FOX_HEADS, T), F32),
                   jax.ShapeDtypeStruct((B, FOX_HEADS, T, LANES), F32)],
        compiler_params=_cparams(("parallel",)),
        name="cumsum_logf",
    )(small)


def key_blocks_transposed(x, n_heads, blk):
    B, T, _ = x.shape
    return x.reshape(B, T // blk, blk, n_heads, HD).transpose(0, 3, 1, 4, 2).astype(BF16)


def _fox_prompt_kernel(q_ref, k_ref, vt_ref, cq_ref, ck_ref, o_ref, *, tk):
    tq = LANES
    qi = pl.program_id(1)
    nh = FOX_HEADS
    qf = q_ref[0]
    qs = [(qf[:, h * HD:(h + 1) * HD] * SCALE).astype(BF16) for h in range(nh)]
    kk = _iota((tk, tq), 0)
    qq = _iota((tk, tq), 1)

    def step(kb, carry, diagonal):
        k0 = pl.multiple_of(kb * tk, tk)
        scores = [_dot_nt(k_ref[0, pl.ds(k0, tk), h * HD:(h + 1) * HD], qs[h]) for h in range(nh)]
        probs = []
        for h in range(nh):
            m, l, _ = carry[h]
            s = scores[h] + (cq_ref[0, h, pl.ds(qi, 1), :] - ck_ref[0, h, pl.ds(k0, tk), :])
            if diagonal:
                s = jnp.where(k0 + kk <= qi * tq + qq, s, NEG_INF)
            m_new = jnp.maximum(m, jnp.max(s, axis=0, keepdims=True))
            alpha = jnp.exp(m - m_new)
            p = jnp.exp(s - m_new)
            probs.append((m_new, alpha, alpha * l + jnp.sum(p, axis=0, keepdims=True), p.astype(BF16)))
        out = []
        for h in range(nh):
            m_new, alpha, l, p = probs[h]
            out.append((m_new, l, alpha * carry[h][2] + _dot(vt_ref[0, h, kb], p)))
        return tuple(out)

    n_full = lax.div(qi * tq, tk)
    n_all = lax.div(qi * tq + tq + tk - 1, tk)
    one = (jnp.full((1, tq), NEG_INF, F32), jnp.zeros((1, tq), F32), jnp.zeros((HD, tq), F32))
    carry = lax.fori_loop(0, n_full, lambda kb, c: step(kb, c, False), (one,) * nh)
    res = lax.fori_loop(n_full, n_all, lambda kb, c: step(kb, c, True), carry)
    for h in range(nh):
        _, l, acc = res[h]
        o_ref[0, :, h * HD:(h + 1) * HD] = (acc / l).T.astype(o_ref.dtype)


def fox_prompt(qf, k_bf, v_t, c_row, c_rep):
    B, T, W = qf.shape
    tq = LANES
    tk = v_t.shape[-1]
    assert T % tk == 0 and tk % tq == 0
    c_row4 = c_row.reshape(B, FOX_HEADS, T // tq, tq)
    return pl.pallas_call(
        functools.partial(_fox_prompt_kernel, tk=tk),
        grid=(B, T // tq),
        in_specs=[pl.BlockSpec((1, tq, W), lambda b, i: (b, i, 0)),
                  pl.BlockSpec((1, T, W), lambda b, i: (b, 0, 0)),
                  pl.BlockSpec((1, FOX_HEADS, T // tk, HD, tk), lambda b, i: (b, 0, 0, 0, 0)),
                  pl.BlockSpec((1, FOX_HEADS, T // tq, tq), lambda b, i: (b, 0, 0, 0)),
                  pl.BlockSpec((1, FOX_HEADS, T, LANES), lambda b, i: (b, 0, 0, 0))],
        out_specs=pl.BlockSpec((1, tq, W), lambda b, i: (b, i, 0)),
        out_shape=jax.ShapeDtypeStruct((B, T, W), BF16),
        compiler_params=_cparams(("parallel", "arbitrary")),
        name="fox_prompt",
    )(qf, k_bf, v_t, c_row4, c_rep)


def _cmp_p_kernel(pt_ref, *refs, npg, row_per_head):
    nref = npg if row_per_head else npg * 2 * G
    page_refs = refs[:nref]
    w_ref, o_ref, lhs_ref = refs[nref:]
    for i in range(npg):
        for c in range(2 * G):
            for t in range(CMP_STRIDE):
                if row_per_head:
                    rows = page_refs[i][pl.ds(t, CHUNKS_PER_PAGE, stride=CMP_STRIDE), c, :]
                else:
                    rows = page_refs[i * 2 * G + c][pl.ds(t, CHUNKS_PER_PAGE, stride=CMP_STRIDE), :]
                lhs_ref[c, i * CHUNKS_PER_PAGE:(i + 1) * CHUNKS_PER_PAGE, t * HD:(t + 1) * HD] = rows
    for kv in range(2):
        for g in range(G):
            o_ref[0, kv, g] = _dot(lhs_ref[kv * G + g].astype(BF16), w_ref[kv])


def compress_partial(pages, page_table, layer, w1cat, row_per_head):
    NB, n_pages = page_table.shape
    npg = _pick(n_pages, (16, 8, 4, 2, 1))
    rows = npg * CHUNKS_PER_PAGE
    kdim = CMP_STRIDE * HD
    in_specs = []
    for i in range(npg):
        if row_per_head:
            in_specs.append(pl.BlockSpec((None, None, 128, None, 2 * G, HD),
                                         lambda b, p, pt, i=i: (pt[b, p * npg + i], layer, 0, 0, 0, 0)))
            continue
        for c in range(2 * G):
            in_specs.append(pl.BlockSpec((None, None, 128, HD),
                                         lambda b, p, pt, i=i, c=c: (pt[b, p * npg + i], layer, 0, c)))
    in_specs.append(pl.BlockSpec((2, kdim, 2 * HD), lambda b, p, pt: (0, 0, 0)))
    return pl.pallas_call(
        functools.partial(_cmp_p_kernel, npg=npg, row_per_head=row_per_head),
        grid_spec=pltpu.PrefetchScalarGridSpec(
            num_scalar_prefetch=1,
            grid=(NB, n_pages // npg),
            in_specs=in_specs,
            out_specs=pl.BlockSpec((1, 2, G, rows, 2 * HD), lambda b, p, pt: (b, 0, 0, p, 0)),
            scratch_shapes=[pltpu.VMEM((2 * G, rows, kdim), F32)]),
        out_shape=jax.ShapeDtypeStruct((NB, 2, G, n_pages * CHUNKS_PER_PAGE, 2 * HD), F32),
        compiler_params=_cparams(("parallel", "arbitrary")),
        name="compress_partial",
    )(page_table, *([pages] * len(in_specs[:-1])), w1cat)


def _gelu_tanh(x):
    return x * (0.5 * (1.0 + jnp.tanh(math.sqrt(2.0 / math.pi) * (x + 0.044715 * (x * x * x)))))


def _cmp_fin_kernel(p_ref, pos_ref, w1_ref, b1_ref, w2_ref, gk_ref, kc_ref, vc_ref):
    NC = p_ref.shape[3]
    row = _iota((NC, HD), 0)
    for kv in range(2):
        w1 = w1_ref[kv]
        posterm = (_dot(pos_ref[kv, 0], w1[:, :HD]) + _dot(pos_ref[kv, 1], w1[:, HD:]))[0:1, :] + b1_ref[kv]
        for g in range(G):
            P = p_ref[0, kv, g]
            pre = P[:, :HD] + pltpu.roll(P[:, HD:], NC - 1, 0) + posterm
            o = _dot(_gelu_tanh(pre).astype(BF16), w2_ref[kv])
            if kv == 0:
                o = _rms(o, gk_ref[...])
            o = jnp.where(row < NC - 1, o, 0.0)
            if kv == 0:
                kc_ref[0, g] = o
            else:
                vc_ref[0, g] = o


def compress_finish(part, pos_ab, w1cat, b1, w2, gain_kc):
    NB, _, _, NC, _ = part.shape
    kdim = CMP_STRIDE * HD
    full = lambda shape: pl.BlockSpec(shape, lambda b: (0,) * len(shape))
    return pl.pallas_call(
        _cmp_fin_kernel,
        grid=(NB,),
        in_specs=[pl.BlockSpec((1, 2, G, NC, 2 * HD), lambda b: (b, 0, 0, 0, 0)),
                  full((2, 2, SUBLANES, kdim)), full((2, kdim, 2 * HD)), full((2, 1, HD)),
                  full((2, HD, HD)), full((1, HD))],
        out_specs=[pl.BlockSpec((1, G, NC, HD), lambda b: (b, 0, 0, 0)),
                   pl.BlockSpec((1, G, NC, HD), lambda b: (b, 0, 0, 0))],
        out_shape=[jax.ShapeDtypeStruct((NB, G, NC, HD), F32), jax.ShapeDtypeStruct((NB, G, NC, HD), F32)],
        compiler_params=_cparams(("parallel",)),
        name="compress_finish",
    )(part, pos_ab, w1cat, b1, w2, gain_kc)


def _sel_overlap_matrix(nc, nsb, n_cb):
    ci = _iota((nc, nsb), 0)
    jb = _iota((nc, nsb), 1)
    ratio = SEL_BLOCK // CMP_STRIDE
    first = ratio * jb - (CMP_LEN // CMP_STRIDE) + 1
    n_ov = (SEL_BLOCK + CMP_LEN) // CMP_STRIDE - 1
    hit = (ci >= first) & (ci < first + n_ov) & (ci < n_cb)
    return jnp.where(hit, 1.0, 0.0).astype(BF16)


def _sel_overlap_matrix_t(nsb, nc, n_cb):
    jb = _iota((nsb, nc), 0)
    ci = _iota((nsb, nc), 1)
    first = (SEL_BLOCK // CMP_STRIDE) * jb - (CMP_LEN // CMP_STRIDE) + 1
    n_ov = (SEL_BLOCK + CMP_LEN) // CMP_STRIDE - 1
    hit = (ci >= first) & (ci < first + n_ov) & (ci < n_cb)
    return jnp.where(hit, 1.0, 0.0).astype(BF16)


def _div_pow2(x, d):
    if isinstance(x, int):
        return x // d
    return lax.shift_right_arithmetic(x, jnp.full(x.shape, d.bit_length() - 1, I32))


def _block_scores(imp_s, qpos, jb, n_sb):
    back = _div_pow2(qpos, SEL_BLOCK) - jb
    forced = (jb == 0) | ((back >= 0) & (back < N_LOCAL_BLOCKS))
    score = jnp.where(back >= 0, imp_s + jnp.where(forced, FORCE_BONUS, 0.0), -1.0)
    return jnp.where(jb < n_sb, score, -2.0)


def _nsa_prompt_kernel(q_ref, kc_ref, vct_ref, ks_ref, vst_ref, kw_ref, vwt_ref, br_ref, bc_ref, bs_ref, bw_ref,
                       exp_ref, o_ref, msk_ref, oc_ref, brt_ref, *, n_cb, n_sb, n_sel):
    g = pl.program_id(1)
    qi = pl.program_id(2)
    tq = LANES
    qf = q_ref[0]
    qt = jnp.concatenate([(qf[:, r * HD:(r + 1) * HD] * SCALE).T for r in range(R)], axis=1).astype(BF16)
    qpos = qi * tq + _iota((1, tq), 1)
    heads = lambda x: jnp.concatenate([x] * R, axis=1)

    has_c = heads(qpos >= CMP_LEN - 1)
    s = _dot(kc_ref[0, 0].astype(BF16), qt) + bc_ref[0, 0]
    e = jnp.exp(s - jnp.max(s, axis=0, keepdims=True))
    p = e * jnp.where(has_c, 1.0 / jnp.sum(e, axis=0, keepdims=True), 0.0)
    oc_ref[...] = _dot(vct_ref[0, 0].astype(BF16), p.astype(BF16))
    imp_c = p[:, 0:tq]
    for r in range(1, R):
        imp_c = imp_c + p[:, r * tq:(r + 1) * tq]

    nsb8 = -(-n_sb // SUBLANES) * SUBLANES
    imp_s = _dot_split_rhs(_sel_overlap_matrix_t(LANES, LANES, n_cb), imp_c)[0:nsb8]
    jb = _iota((nsb8, tq), 0)
    score = _block_scores(imp_s, qpos, jb, n_sb)
    rank = jnp.zeros((nsb8, tq), I32)
    for i in range(n_sb):
        row = score[i:i + 1, :]
        rank = rank + jnp.where(row > score, 1, jnp.where((row == score) & (jb > i), 1, 0))
    sel = jnp.where(rank < n_sel, 1.0, 0.0)
    if nsb8 < LANES:
        sel = jnp.concatenate([sel, jnp.zeros((LANES - nsb8, tq), F32)], axis=0)
    sel = sel.astype(BF16)

    n_keys = msk_ref.shape[0]
    chosen = _dot(exp_ref[...], sel)
    key_pos = _iota((n_keys, tq), 0)
    msk_ref[...] = jnp.where((chosen > 0.5) & (qi * tq + _iota((n_keys, tq), 1) >= key_pos), 0.0, NEG_INF)

    def branch(k_ref, vt_ref, b_ref, kind_of, masked):
        def body(j, carry):
            m, l, acc = carry
            k0 = pl.multiple_of(j * (2 * LANES), 2 * LANES)
            off = qi - 2 * j
            bias = jnp.concatenate([b_ref[0, kind_of(off)], b_ref[0, kind_of(off - 1)]], axis=0)
            s = _dot(k_ref[0, pl.ds(k0, 2 * LANES), :], qt) + bias
            if masked:
                s = s + heads(msk_ref[pl.ds(k0, 2 * LANES), :])
            m_new = jnp.maximum(m, jnp.max(s, axis=0, keepdims=True))
            alpha = jnp.exp(m - m_new)
            p = jnp.exp(s - m_new)
            l = alpha * l + jnp.sum(p, axis=0, keepdims=True)
            vblk = jnp.concatenate([vt_ref[0, 0, 2 * j], vt_ref[0, 0, 2 * j + 1]], axis=1)
            acc = alpha * acc + _dot(vblk, p.astype(BF16))
            return m_new, l, acc

        return body

    n_win = WINDOW // LANES
    sel_trip = branch(ks_ref, vst_ref, bs_ref, lambda off: jnp.clip(off, 0, SEL_BIAS_KINDS - 1), True)
    win_trip = branch(kw_ref, vwt_ref, bw_ref, lambda off: jnp.where((off < 0) | (off > n_win), n_win + 1, off),
                      False)
    init = (jnp.full((1, R * tq), NEG_INF, F32), jnp.zeros((1, R * tq), F32), jnp.zeros((HD, R * tq), F32))
    first_win = jnp.maximum(qi - n_win, 0) // 2
    sel_c = lax.fori_loop(0, first_win, sel_trip, init)
    sel_c, win_c = lax.fori_loop(first_win, qi // 2 + 1,
                                 lambda j, c: (sel_trip(j, c[0]), win_trip(j, c[1])), (sel_c, init))
    o_s = sel_c[2] * (1.0 / sel_c[1])
    o_w = win_c[2] * (1.0 / win_c[1])
    brt_ref[...] = br_ref[0].T
    gate = lambda i: jnp.concatenate([brt_ref[pl.ds((g * R + r) * 3 + i, 1), :] for r in range(R)], axis=1)
    o = gate(0) * oc_ref[...] + gate(1) * o_s + gate(2) * o_w
    for r in range(R):
        o_ref[0, :, r * HD:(r + 1) * HD] = o[:, r * tq:(r + 1) * tq].T.astype(o_ref.dtype)


def nsa_prompt(qa, kc, vc_t, ks_bf, ks_blk0, vs_t, kw_bf, kw_blk0, vw_t, small, b_cmp, b_sel, b_win):
    B, T, _ = qa.shape
    assert T % LANES == 0 and T // CMP_STRIDE == LANES and kc.shape[2] == LANES
    n_cb = T // CMP_STRIDE - CMP_LEN // CMP_STRIDE + 1
    n_sb = -(-T // SEL_BLOCK)
    n_sel = min(SEL_TOPN, n_sb)
    tq = LANES
    nkb = T // LANES
    assert nkb % 2 == 0 and n_sb <= LANES
    grp = lambda b, g, i: (b, g, 0, 0)
    expand = (jnp.arange(T, dtype=I32)[:, None] // SEL_BLOCK == jnp.arange(LANES, dtype=I32)[None, :]).astype(BF16)
    return pl.pallas_call(
        functools.partial(_nsa_prompt_kernel, n_cb=n_cb, n_sb=n_sb, n_sel=n_sel),
        grid=(B, G, T // tq),
        in_specs=[pl.BlockSpec((1, tq, R * HD), lambda b, g, i: (b, i, g)),
                  pl.BlockSpec((1, 1, LANES, HD), grp),
                  pl.BlockSpec((1, 1, HD, LANES), grp),
                  pl.BlockSpec((1, T, HD), lambda b, g, i: (b, 0, ks_blk0 + g)),
                  pl.BlockSpec((1, 1, nkb, HD, LANES), lambda b, g, i: (b, g, 0, 0, 0)),
                  pl.BlockSpec((1, T, HD), lambda b, g, i: (b, 0, kw_blk0 + g)),
                  pl.BlockSpec((1, 1, nkb, HD, LANES), lambda b, g, i: (b, g, 0, 0, 0)),
                  pl.BlockSpec((1, tq, LANES), lambda b, g, i: (b, i, 0)),
                  pl.BlockSpec((1, 1, LANES, R * LANES), lambda b, g, i: (g, i, 0, 0)),
                  pl.BlockSpec((1, SEL_BIAS_KINDS, LANES, R * LANES), lambda b, g, i: (g, 0, 0, 0)),
                  pl.BlockSpec((1, WIN_BIAS_KINDS, LANES, R * LANES), lambda b, g, i: (g, 0, 0, 0)),
                  pl.BlockSpec((T, LANES), lambda b, g, i: (0, 0))],
        out_specs=pl.BlockSpec((1, tq, R * HD), lambda b, g, i: (b, i, g)),
        out_shape=jax.ShapeDtypeStruct((B, T, NSA_HEADS * HD), BF16),
        scratch_shapes=[pltpu.VMEM((T, tq), F32), pltpu.VMEM((HD, R * tq), F32),
                        pltpu.VMEM((LANES, tq), F32)],
        compiler_params=_cparams(("parallel", "parallel", "arbitrary")),
        name="nsa_prompt",
    )(qa, kc, vc_t, ks_bf, vs_t, kw_bf, vw_t, small, b_cmp, b_sel, b_win, expand)


def _dec_cmp_kernel(q_ref, kc_ref, vc_ref, tblt_ref, oc_ref, idx_ref, *, qpos, n_cb, n_sb, n_sel, nsbp):
    NC = kc_ref.shape[2]
    q8 = q_ref[0].astype(BF16)
    row = _iota((NSA_HEADS, NC), 0)
    j = _iota((NSA_HEADS, NC), 1)
    s = jnp.where(row < R, _dot_nt(q8, kc_ref[0, 0].astype(BF16)), _dot_nt(q8, kc_ref[0, 1].astype(BF16)))
    dist = qpos - (j * CMP_STRIDE + (CMP_LEN - 1))
    valid = dist >= 0
    tblt = tblt_ref[...]
    bias = _lookup(_bucket(dist), [tblt[:, k:k + 1] for k in range(N_BUCKETS)])
    s = jnp.where(valid, s * SCALE + bias, NEG_INF)
    m = jnp.max(s, axis=1, keepdims=True)
    e = jnp.where(valid, jnp.exp(s - m), 0.0)
    l = jnp.sum(e, axis=1, keepdims=True)
    p = e / jnp.where(l > 0.0, l, 1.0)
    pb = p.astype(BF16)
    row_o = _iota((NSA_HEADS, HD), 0)
    oc_ref[0] = jnp.where(row_o < R, _dot(pb, vc_ref[0, 0].astype(BF16)), _dot(pb, vc_ref[0, 1].astype(BF16)))

    imp0 = jnp.sum(jnp.where(row < R, p, 0.0), axis=0, keepdims=True)
    imp1 = jnp.sum(jnp.where(row >= R, p, 0.0), axis=0, keepdims=True)
    imp = jnp.where(row == 0, imp0, jnp.where(row == 1, imp1, 0.0))
    imp_s = _dot_split_lhs(imp, _sel_overlap_matrix(NC, nsbp, n_cb))
    jb = _iota((NSA_HEADS, nsbp), 1)
    score = _block_scores(imp_s, qpos, jb, n_sb)
    jbf = jb.astype(F32)
    lane_o = _iota((NSA_HEADS, LANES), 1)
    out = jnp.zeros((NSA_HEADS, LANES), I32)
    for n in range(n_sel):
        mx = jnp.max(score, axis=1, keepdims=True)
        am = jnp.min(jnp.where(score == mx, jbf, float(nsbp)), axis=1, keepdims=True)
        out = jnp.where(lane_o == n, am.astype(I32), out)
        score = jnp.where(jbf == am, -3.0, score)
    idx_ref[0] = out


def dec_cmp(q8, kc, vc, tblt, qpos):
    DB = q8.shape[0]
    NC = kc.shape[2]
    L = qpos + 1
    n_cb = L // CMP_STRIDE - CMP_LEN // CMP_STRIDE + 1
    assert n_cb == NC - 1
    n_sb = -(-L // SEL_BLOCK)
    n_sel = min(SEL_TOPN, n_sb)
    nsbp = -(-n_sb // LANES) * LANES
    oc, idx = pl.pallas_call(
        functools.partial(_dec_cmp_kernel, qpos=qpos, n_cb=n_cb, n_sb=n_sb, n_sel=n_sel, nsbp=nsbp),
        grid=(DB,),
        in_specs=[pl.BlockSpec((1, NSA_HEADS, HD), lambda b: (b, 0, 0)),
                  pl.BlockSpec((1, G, NC, HD), lambda b: (b, 0, 0, 0)),
                  pl.BlockSpec((1, G, NC, HD), lambda b: (b, 0, 0, 0)),
                  pl.BlockSpec((NSA_HEADS, N_BUCKETS), lambda b: (0, 0))],
        out_specs=[pl.BlockSpec((1, NSA_HEADS, HD), lambda b: (b, 0, 0)),
                   pl.BlockSpec((1, NSA_HEADS, LANES), lambda b: (b, 0, 0))],
        out_shape=[jax.ShapeDtypeStruct((DB, NSA_HEADS, HD), F32), jax.ShapeDtypeStruct((DB, NSA_HEADS, LANES), I32)],
        compiler_params=_cparams(("parallel",)),
        name="dec_cmp",
    )(q8, kc, vc, tblt)
    return oc, idx[:, :G, :n_sel].reshape(DB, G * n_sel), n_sel


def _softmax_with_new(s, valid, s_new, new_ok):
    s = jnp.where(valid, s, NEG_INF)
    s_new = jnp.where(new_ok, s_new, NEG_INF)
    m = jnp.maximum(jnp.max(s, axis=1, keepdims=True), s_new)
    e = jnp.where(valid, jnp.exp(s - m), 0.0)
    en = jnp.where(new_ok, jnp.exp(s_new - m), 0.0)
    l = jnp.sum(e, axis=1, keepdims=True) + en
    inv = 1.0 / jnp.where(l > 0.0, l, 1.0)
    return e, en, inv


def _dec_selwin_kernel(pt_ref, idx_ref, *refs, n_sel, qpos, jb_new, w_buf):
    blk_refs = refs[:G * n_sel]
    q_ref, new_ref, win_ref, wnew_ref, br_ref, oc_ref, tblt_ref, o_ref = refs[G * n_sel:]
    rph = 4 * G
    b = pl.program_id(0)
    q8 = q_ref[0]
    row8 = _iota((NSA_HEADS, HD), 0)
    tblt = tblt_ref[...]
    tcols = [tblt[:, k:k + 1] for k in range(N_BUCKETS)]
    new8 = new_ref[0]
    wnew = wnew_ref[0]
    brrow = br_ref[0]
    oc = oc_ref[0]
    nk = n_sel * SEL_BLOCK
    lane = _iota((NSA_HEADS, nk), 1)
    outs = []
    for g in range(G):
        qg = jnp.where(_div_pow2(row8, R) == g, q8, 0.0)
        qb = qg.astype(BF16)
        qr = qb.astype(F32)
        kmat = jnp.concatenate([blk_refs[g * n_sel + n][pl.ds(2 * G + g, SEL_BLOCK, stride=rph), :]
                                for n in range(n_sel)], axis=0).astype(BF16)
        vmat = jnp.concatenate([blk_refs[g * n_sel + n][pl.ds(3 * G + g, SEL_BLOCK, stride=rph), :]
                                for n in range(n_sel)], axis=0).astype(BF16)
        idxv = jnp.zeros((NSA_HEADS, nk), I32)
        has_new = jnp.zeros((), jnp.bool_)
        for n in range(n_sel):
            sidx = idx_ref[b, g * n_sel + n]
            idxv = jnp.where(_div_pow2(lane, SEL_BLOCK) == n, sidx, idxv)
            has_new = has_new | (sidx == jb_new)
        dist = qpos - (idxv * SEL_BLOCK + (lane & (SEL_BLOCK - 1)))
        valid = (idxv < jb_new) & (dist >= 0)
        s = _dot_nt(qb, kmat) * SCALE + _lookup(_bucket(dist), tcols)
        kn = new8[2 * G + g:2 * G + g + 1, :].astype(BF16).astype(F32)
        vn = new8[3 * G + g:3 * G + g + 1, :].astype(BF16).astype(F32)
        s_new = jnp.sum(qr * kn, axis=1, keepdims=True) * SCALE + tcols[0]
        e, en, inv = _softmax_with_new(s, valid, s_new, has_new)
        o_s = (_dot(e.astype(BF16), vmat) + en * vn) * inv

        kwin = win_ref[pl.ds(g, w_buf, stride=2 * G), :].astype(BF16)
        vwin = win_ref[pl.ds(G + g, w_buf, stride=2 * G), :].astype(BF16)
        lw = _iota((NSA_HEADS, w_buf), 1)
        dist_w = w_buf - lw
        valid_w = (dist_w < WINDOW) & (qpos - dist_w >= 0)
        sw = _dot_nt(qb, kwin) * SCALE + _lookup(_bucket(dist_w), tcols)
        kwn = wnew[g:g + 1, :].astype(BF16).astype(F32)
        vwn = wnew[G + g:G + g + 1, :].astype(BF16).astype(F32)
        sw_new = jnp.sum(qr * kwn, axis=1, keepdims=True) * SCALE + tcols[0]
        ew, ewn, invw = _softmax_with_new(sw, valid_w, sw_new, jnp.ones((), jnp.bool_))
        o_w = (_dot(ew.astype(BF16), vwin) + ewn * vwn) * invw

        def bcol(i):
            return jnp.sum(jnp.where(_iota((NSA_HEADS, LANES), 1) == row8 * 3 + i, brrow, 0.0), axis=1, keepdims=True)

        outs.append(bcol(0) * oc + bcol(1) * o_s + bcol(2) * o_w)
    o_ref[0] = jnp.where(row8 < R, outs[0], outs[1])


def dec_selwin(q8, new8, wnew4, small_s, oc, tblt, cache_nsa_kv, state_win, page_table, idx, n_sel, layer, qpos):
    DB = q8.shape[0]
    n_pool, depth = cache_nsa_kv.shape[:2]
    page = cache_nsa_kv.shape[2]
    halves = page // SEL_BLOCK
    w_buf = state_win.shape[2]
    n_pages = page_table.shape[1]
    jb_new = qpos // SEL_BLOCK
    rph = 4 * G
    cache_rows = cache_nsa_kv.reshape(n_pool, depth, page * rph, HD)
    state_rows = state_win.reshape(depth, DB, w_buf * 2 * G, HD)
    in_specs = []
    args = []
    for g in range(G):
        for n in range(n_sel):
            def imap(b, pt, ix, g=g, n=n):
                jb = jnp.clip(ix[b, g * n_sel + n], 0, jb_new - 1)
                return (pt[b, jb // halves], layer, jb % halves, 0)
            in_specs.append(pl.BlockSpec((None, None, SEL_BLOCK * rph, HD), imap))
            args.append(cache_rows)
    row3 = lambda b, pt, ix: (b, 0, 0)
    in_specs += [pl.BlockSpec((1, NSA_HEADS, HD), row3), pl.BlockSpec((1, 4 * G, HD), row3),
                 pl.BlockSpec((None, None, w_buf * 2 * G, HD), lambda b, pt, ix: (layer, b, 0, 0)),
                 pl.BlockSpec((1, 2 * G, HD), row3), pl.BlockSpec((1, 1, LANES), row3),
                 pl.BlockSpec((1, NSA_HEADS, HD), row3),
                 pl.BlockSpec((NSA_HEADS, N_BUCKETS), lambda b, pt, ix: (0, 0))]
    args += [q8, new8, state_rows, wnew4, small_s.reshape(DB, 1, LANES), oc, tblt]
    return pl.pallas_call(
        functools.partial(_dec_selwin_kernel, n_sel=n_sel, qpos=qpos, jb_new=jb_new, w_buf=w_buf),
        grid_spec=pltpu.PrefetchScalarGridSpec(
            num_scalar_prefetch=2,
            grid=(DB,),
            in_specs=in_specs,
            out_specs=pl.BlockSpec((1, NSA_HEADS, HD), row3)),
        out_shape=jax.ShapeDtypeStruct((DB, NSA_HEADS, HD), F32),
        compiler_params=_cparams(("arbitrary",)),
        name="dec_selwin",
    )(page_table, idx, *args)


def _dec_fox_kernel(pt_ref, *refs, npg):
    kv_refs = refs[:npg]
    lf_refs = refs[npg:2 * npg]
    q_ref, new_ref, lfn_ref, o_ref, m_sc, l_sc, acc_sc, car_sc = refs[2 * npg:]
    p = pl.program_id(1)
    nh = FOX_HEADS
    page = kv_refs[0].shape[0]
    q8 = q_ref[0]
    qs = q8 * SCALE

    @pl.when(p == 0)
    def _():
        new = new_ref[0]
        s_new = jnp.sum(qs * new[0:nh], axis=1, keepdims=True)
        m_sc[...] = jnp.broadcast_to(s_new, (nh, LANES))
        l_sc[...] = jnp.ones((nh, LANES), F32)
        acc_sc[...] = new[nh:2 * nh]
        lane1 = _iota((nh, LANES), 1)
        row1 = _iota((nh, LANES), 0)
        car_sc[...] = jnp.sum(jnp.where(lane1 == LOGF_LANE0 + row1, lfn_ref[0], 0.0), axis=1, keepdims=True)

    def tree(x, op):
        while x.shape[0] > 1:
            half = x.shape[0] // 2
            x = op(x[:half], x[half:])
        return x[0]

    uu = _iota((LANES, LANES), 0)
    ss = _iota((LANES, LANES), 1)
    later = jnp.where(uu > ss, 1.0, 0.0).astype(BF16)
    ones = jnp.ones((HD, LANES), BF16)
    tok3 = _iota((page, nh, LANES), 0)
    lane3 = _iota((page, nh, LANES), 2)
    m = m_sc[...]
    l = l_sc[...]
    acc = acc_sc[...]
    car = car_sc[...]
    for i in range(npg):
        lft = lf_refs[i][...]
        decay = _dot_split_lhs(lft, later) + car
        lhs = jnp.where(lane3 == tok3, decay[None], 0.0) + kv_refs[i][:, 0] * qs[None]
        s = _dot(lhs.reshape(page * nh, HD).astype(BF16), ones).reshape(page, nh, LANES)
        m_new = jnp.maximum(m, tree(s, jnp.maximum))
        alpha = jnp.exp(m - m_new)
        pp = jnp.exp(s - m_new[None])
        l = alpha * l + tree(pp, jnp.add)
        acc = alpha * acc + tree(pp * kv_refs[i][:, 1], jnp.add)
        m = m_new
        car = car + jnp.sum(lft, axis=1, keepdims=True)
    m_sc[...] = m
    l_sc[...] = l
    acc_sc[...] = acc
    car_sc[...] = car

    @pl.when(p == pl.num_programs(1) - 1)
    def _():
        o_ref[0] = acc / l


def dec_fox(q8, new16, small_s, cache_fox_kv, logf_t, page_table, layer):
    DB = q8.shape[0]
    n_pool, depth, page = cache_fox_kv.shape[:3]
    assert page == LANES
    n_pages = page_table.shape[1]
    npg = _pick(n_pages, (8, 4, 2, 1))
    in_specs = []
    for i in range(npg):
        in_specs.append(pl.BlockSpec((None, None, page, 2, FOX_HEADS, HD),
                                     lambda b, p, pt, i=i: (pt[b, n_pages - 1 - (p * npg + i)], layer, 0, 0, 0, 0)))
    for i in range(npg):
        in_specs.append(pl.BlockSpec((None, None, FOX_HEADS, page),
                                     lambda b, p, pt, i=i: (pt[b, n_pages - 1 - (p * npg + i)], layer, 0, 0)))
    row3 = lambda b, p, pt: (b, 0, 0)
    in_specs += [pl.BlockSpec((1, FOX_HEADS, HD), row3), pl.BlockSpec((1, 2 * FOX_HEADS, HD), row3),
                 pl.BlockSpec((1, 1, LANES), row3)]
    return pl.pallas_call(
        functools.partial(_dec_fox_kernel, npg=npg),
        grid_spec=pltpu.PrefetchScalarGridSpec(
            num_scalar_prefetch=1,
            grid=(DB, n_pages // npg),
            in_specs=in_specs,
            out_specs=pl.BlockSpec((1, FOX_HEADS, HD), row3),
            scratch_shapes=[pltpu.VMEM((FOX_HEADS, LANES), F32), pltpu.VMEM((FOX_HEADS, LANES), F32),
                            pltpu.VMEM((FOX_HEADS, HD), F32), pltpu.VMEM((FOX_HEADS, 1), F32)]),
        out_shape=jax.ShapeDtypeStruct((DB, FOX_HEADS, HD), F32),
        compiler_params=_cparams(("parallel", "arbitrary")),
        name="dec_fox",
    )(page_table, *([cache_fox_kv] * npg), *([logf_t] * npg), q8, new16, small_s.reshape(DB, 1, LANES))


def _layer_weights(l, D, w_in, b_forget, qk_gain_nsa, qk_gain_fox):
    nq = NSA_HEADS * HD
    nkv = 6 * G * HD
    nfox = 3 * FOX_HEADS * HD
    o_br = nq + nkv
    o_fox = o_br + BR_LANES
    o_f = o_fox + nfox
    o_mg = o_f + FOX_HEADS
    w = w_in[l]
    ones = jnp.ones((HD,), F32)
    gn = qk_gain_nsa[l]
    gf = qk_gain_fox[l]

    def gains(rows):
        return jnp.concatenate(rows).reshape(1, -1)

    zeros = lambda n: jnp.zeros((1, n), F32)
    small_b = jnp.concatenate([jnp.zeros((BR_LANES,), F32), b_forget[l],
                               jnp.zeros((LANES - BR_LANES - FOX_HEADS,), F32)]).reshape(1, LANES)
    kvh = G
    tg = _pick(2 * D, (1024, 512, 256, 128))
    table = [
        ("foxkv", [(o_fox + FOX_HEADS * HD, o_f)], gains([gf[1]] * FOX_HEADS + [ones] * FOX_HEADS),
         zeros(2 * FOX_HEADS * HD), ("norm",) * FOX_HEADS + ("raw",) * FOX_HEADS, 2 * FOX_HEADS * HD),
        ("qa", [(0, nq)], gains([gn[0]] * NSA_HEADS), zeros(nq), ("norm",) * NSA_HEADS, nq),
        ("nsakv", [(nq, nq + 4 * kvh * HD)], gains([ones] * (2 * kvh) + [gn[2]] * kvh + [ones] * kvh),
         zeros(4 * kvh * HD), ("raw",) * (2 * kvh) + ("norm",) * kvh + ("raw",) * kvh, 4 * kvh * HD),
        ("qf", [(o_fox, o_fox + FOX_HEADS * HD)], gains([gf[0]] * FOX_HEADS), zeros(FOX_HEADS * HD),
         ("norm",) * FOX_HEADS, FOX_HEADS * HD),
        ("gates", [(o_mg, o_mg + 2 * D)], zeros(2 * D), zeros(2 * D), ("sigmoid",) * (tg // LANES), tg),
        ("win", [(nq + 4 * kvh * HD, o_br)], gains([gn[3]] * kvh + [ones] * kvh), zeros(2 * kvh * HD),
         ("norm",) * kvh + ("raw",) * kvh, 2 * kvh * HD),
        ("small", [(o_br, o_br + BR_LANES), (o_f, o_f + FOX_HEADS)], zeros(LANES), small_b, ("small",), LANES),
    ]
    pieces, segs, col = [], {}, 0
    for name, spans, gn_row, bias_row, modes, tn in table:
        width = gn_row.shape[1]
        got = 0
        for lo, hi in spans:
            pieces.append(w[:, lo:hi])
            got += hi - lo
        if got < width:
            pieces.append(jnp.zeros((D, width - got), F32))
        assert col % tn == 0
        segs[name] = (col, width, gn_row, bias_row, modes, tn)
        col += width
    return jnp.concatenate(pieces, axis=1).astype(BF16), segs


def _project_all(h, w_perm, segs, bf16_copies=()):
    out = {}
    for name, (col0, n, gn, bs, modes, tn) in segs.items():
        if name in bf16_copies:
            out[name], out[name + "_bf16"] = proj(h, w_perm, col0, n, gn, bs, modes, tn, with_bf16_copy=True)
        else:
            out[name] = proj(h, w_perm, col0, n, gn, bs, modes, tn)
    return out


def _tail(x, o_nsa, o_fox, gates, pe, gains_l, gain_next, wbn, wbf, wo, wfi, wfo, wple, wpg):
    merged = merge(o_nsa, o_fox, gates, wbn, wbf)
    x1, h1 = out_proj(merged, wo, x, gains_l[1])
    act = ffn_in(h1, wfi)
    x2 = ffn_out(act, wfo, x1)
    return ple_gate(x2, gains_l[2], wpg, pe, wple, gain_next)


def kernel(x_prompt, x_sample, cache_nsa_kv, cache_fox_kv, cache_fox_logf, state_nsa_window, page_table, p_prompt, p_sample, rel_bias_table, norm_gains, w_in, b_forget, qk_gain_nsa, qk_gain_fox, cmp_pos, w_cmp1, b_cmp1, w_cmp2, w_branch_nsa, w_branch_fox, w_out, w_ffn_in, w_ffn_out, w_ple, w_ple_gate):
    B, T, D = x_prompt.shape
    DB, n_qs, _ = x_sample.shape
    assert n_qs == 1
    depth = w_in.shape[0]
    page = cache_nsa_kv.shape[2]
    n_pages = page_table.shape[1]
    past = n_pages * page
    w_buf = state_nsa_window.shape[2]
    kdim = CMP_STRIDE * HD

    xp = x_prompt.reshape(B * T, D)
    xs = x_sample.reshape(DB, D)
    b_sel, b_win = bias_sel_win(rel_bias_table)
    b_cmp = bias_cmp(rel_bias_table, T // LANES)
    fox_tk = _pick(T, (256, 128))
    tblt = rel_bias_table.T
    logf_t = jnp.swapaxes(cache_fox_logf, 2, 3)
    pt_prompt = jnp.arange(B * (T // page), dtype=I32).reshape(B, T // page)
    hp = rms_bf16(xp, norm_gains[0, 0])
    hs = rms_bf16(xs, norm_gains[0, 0])

    outs = [[] for _ in range(8)]
    for l in range(depth):
        w_perm, segs = _layer_weights(l, D, w_in, b_forget, qk_gain_nsa, qk_gain_fox)
        w1 = w_cmp1[l]
        w1cat = jnp.concatenate([w1[:, :kdim], w1[:, kdim:]], axis=2).astype(BF16)
        pos = cmp_pos[l].reshape(2, 2, 1, kdim)
        pos_ab = jnp.broadcast_to(pos, (2, 2, SUBLANES, kdim)).astype(BF16)
        cmp_args = (pos_ab, w1cat, b_cmp1[l].reshape(2, 1, HD), w_cmp2[l].astype(BF16),
                    qk_gain_nsa[l, 1].reshape(1, HD))
        gain_next = norm_gains[l + 1, 0] if l + 1 < depth else norm_gains[l, 0]
        tail_w = (norm_gains[l], gain_next, w_branch_nsa[l].astype(BF16), w_branch_fox[l].astype(BF16),
                  w_out[l].astype(BF16), w_ffn_in[l].astype(BF16), w_ffn_out[l].astype(BF16),
                  w_ple[l].astype(BF16), w_ple_gate[l].astype(BF16))

        P = _project_all(hp, w_perm, segs, bf16_copies=("foxkv", "nsakv", "win"))
        small3 = P["small"].reshape(B, T, LANES)
        c_row, c_rep = cumsum_logf(small3)
        foxkv_bf = P["foxkv_bf16"].reshape(B, T, -1)
        nfk = FOX_HEADS * HD
        o_fox_p = fox_prompt(P["qf"].reshape(B, T, -1), foxkv_bf,
                             key_blocks_transposed(foxkv_bf[:, :, nfk:], FOX_HEADS, fox_tk), c_row, c_rep)
        part = compress_partial(P["nsakv"].reshape(B * (T // page), 1, page, 4 * G * HD), pt_prompt, 0, w1cat, False)
        kc_p, vc_p = compress_finish(part, *cmp_args)
        nsakv_bf = P["nsakv_bf16"].reshape(B, T, -1)
        win_bf = P["win_bf16"].reshape(B, T, -1)
        ngk = G * HD
        o_nsa_p = nsa_prompt(P["qa"].reshape(B, T, -1), kc_p, jnp.swapaxes(vc_p, 2, 3),
                             nsakv_bf, 2 * G, key_blocks_transposed(nsakv_bf[:, :, 3 * ngk:], G, LANES),
                             win_bf, 0, key_blocks_transposed(win_bf[:, :, ngk:], G, LANES),
                             small3, b_cmp, b_sel, b_win)
        xp, hp = _tail(xp, o_nsa_p.reshape(B * T, -1), o_fox_p.reshape(B * T, -1), P["gates"],
                       p_prompt[l].reshape(B * T, -1), *tail_w)

        S = _project_all(hs, w_perm, segs)
        part_s = compress_partial(cache_nsa_kv.reshape(cache_nsa_kv.shape[0], depth, page, 2, 2 * G, HD),
                                  page_table, l, w1cat, True)
        kc_s, vc_s = compress_finish(part_s, *cmp_args)
        q8 = S["qa"].reshape(DB, NSA_HEADS, HD)
        oc_s, idx, n_sel = dec_cmp(q8, kc_s, vc_s, tblt, past)
        o_nsa_s = dec_selwin(q8, S["nsakv"].reshape(DB, 4 * G, HD), S["win"].reshape(DB, 2 * G, HD), S["small"],
                             oc_s, tblt, cache_nsa_kv, state_nsa_window, page_table, idx, n_sel, l, past)
        o_fox_s = dec_fox(S["qf"].reshape(DB, FOX_HEADS, HD), S["foxkv"].reshape(DB, 2 * FOX_HEADS, HD),
                          S["small"], cache_fox_kv, logf_t, page_table, l)
        xs, hs = _tail(xs, o_nsa_s.reshape(DB, -1).astype(BF16), o_fox_s.reshape(DB, -1).astype(BF16), S["gates"],
                       p_sample[l].reshape(DB, -1), *tail_w)

        lf0, lf1 = LOGF_LANE0, LOGF_LANE0 + FOX_HEADS
        outs[0].append(P["nsakv"].reshape(B, T, 4, G, HD))
        outs[1].append(P["win"].reshape(B, T, 2, G, HD)[:, T - min(WINDOW, T):])
        outs[2].append(P["foxkv"].reshape(B, T, 2, FOX_HEADS, HD))
        outs[3].append(P["small"][:, lf0:lf1].reshape(B, T, FOX_HEADS))
        outs[4].append(S["nsakv"].reshape(DB, 1, 4, G, HD))
        win_all = jnp.concatenate([state_nsa_window[l], S["win"].reshape(DB, 1, 2, G, HD)], axis=1)
        outs[5].append(win_all[:, w_buf + 1 - min(WINDOW, w_buf + 1):])
        outs[6].append(S["foxkv"].reshape(DB, 1, 2, FOX_HEADS, HD))
        outs[7].append(S["small"][:, lf0:lf1].reshape(DB, 1, FOX_HEADS))

    return (xp.reshape(B, T, D), xs.reshape(DB, 1, D)) + tuple(jnp.stack(o) for o in outs)
```
